```python
import jax, jax.numpy as jnp
from jax import lax
import numpy as np

D_MODEL = 2048
BATCH = 2
SEQ = 8192
DEPTH = 1
DEC_BATCH = 8
DEC_SEQ = 2048
PAST_LEN = 128

D_MIX = D_MODEL
D_GMLP = D_MIX // 2
GMLP_HEADS = 8
GMLP_HEAD_DIM = D_GMLP // GMLP_HEADS
CHUNK = 128
D_LRU = D_MIX - D_GMLP
LRU_HEADS = 4
LRU_HEAD_DIM = D_LRU // LRU_HEADS
CONV_WIDTH = 4
LRU_C = 8.0
D_IN = 2 * D_GMLP + 2 * D_LRU
N_EXPERTS = 32
TOP_K = 4
D_EXPERT = D_MODEL
SWIGLU_LIMIT = 7.0
SWIGLU_ALPHA = 1.702
MOE_BLOCK = 128
EPS = 1e-6

kernel_name = 'hybrid_gmlp_rglru_moe_encoder'


def _rmsnorm(x, g):
    xf = x.astype(jnp.float32)
    y = xf * lax.rsqrt(jnp.mean(xf * xf, axis=-1, keepdims=True) + EPS)
    return (y * g.astype(jnp.float32)).astype(x.dtype)


def _layernorm(x, g, b):
    xf = x.astype(jnp.float32)
    mu = jnp.mean(xf, axis=-1, keepdims=True)
    var = jnp.mean(jnp.square(xf - mu), axis=-1, keepdims=True)
    y = (xf - mu) * lax.rsqrt(var + EPS) * g.astype(jnp.float32) + b.astype(jnp.float32)
    return y.astype(x.dtype)


def _gmlp_group(u, v, ln_g, ln_b, w_s, b_s):
    bsz, seq, _ = u.shape
    u = jax.nn.gelu(u)
    v = _layernorm(jax.nn.gelu(v), ln_g, ln_b)
    vh = v.reshape(bsz, seq // CHUNK, CHUNK, GMLP_HEADS, GMLP_HEAD_DIM)
    mixed = jnp.einsum('hij,bcjhd->bcihd', w_s, vh) + b_s.T[:, :, None]
    return u * mixed.reshape(bsz, seq, D_GMLP)


def _lru_combine(left, right):
    a_l, b_l = left
    a_r, b_r = right
    return a_l * a_r, a_r * b_l + b_r


def _rglru_direction(x, w_r, b_r, w_i, b_i, lam, reverse):
    bsz, seq, _ = x.shape
    xh = x.reshape(bsz, seq, LRU_HEADS, LRU_HEAD_DIM)
    r = jax.nn.sigmoid((jnp.einsum('bshd,hde->bshe', xh, w_r).reshape(bsz, seq, D_LRU) + b_r).astype(jnp.float32))
    i = jax.nn.sigmoid((jnp.einsum('bshd,hde->bshe', xh, w_i).reshape(bsz, seq, D_LRU) + b_i).astype(jnp.float32))
    log_a = -LRU_C * r * jax.nn.softplus(-lam.astype(jnp.float32))
    a = jnp.exp(log_a)
    b = jnp.sqrt(-jnp.expm1(2.0 * log_a)) * i * x.astype(jnp.float32)
    _, h = lax.associative_scan(_lru_combine, (a, b), reverse=reverse, axis=1)
    return h


def _recurrent_group(xr, gr, conv_w, conv_b, w_r, b_r, w_i, b_i, lam):
    pad = (CONV_WIDTH // 2, CONV_WIDTH - 1 - CONV_WIDTH // 2)
    xc = lax.conv_general_dilated(xr, conv_w[:, None, :], (1,), [pad],
                                  dimension_numbers=('NWC', 'WIO', 'NWC'),
                                  feature_group_count=D_LRU) + conv_b
    h = (_rglru_direction(xc, w_r[0], b_r[0], w_i[0], b_i[0], lam[0], False)
         + _rglru_direction(xc, w_r[1], b_r[1], w_i[1], b_i[1], lam[1], True))
    return jax.nn.gelu(gr) * h.astype(xr.dtype)


def _moe_ffn(h, router_w, router_b, w_gate_up, b_gate_up, w_down, b_down):
    n_tok, d = h.shape
    logits = jnp.dot(h, router_w).astype(jnp.float32) + router_b.astype(jnp.float32)
    top_vals, top_idx = lax.top_k(logits, TOP_K)
    gates = jax.nn.softmax(top_vals, axis=-1)
    m = n_tok * TOP_K
    flat_e = top_idx.reshape(m).astype(jnp.int32)
    flat_tok = jnp.arange(m, dtype=jnp.int32) // TOP_K
    flat_g = gates.reshape(m)
    order = jnp.argsort(flat_e)
    sorted_e = flat_e[order]
    counts = jnp.bincount(flat_e, length=N_EXPERTS).astype(jnp.int32)
    padded = (counts + MOE_BLOCK - 1) // MOE_BLOCK * MOE_BLOCK
    pad_end = jnp.cumsum(padded)
    pad_start = pad_end - padded
    start = jnp.cumsum(counts) - counts
    dest = pad_start[sorted_e] + jnp.arange(m, dtype=jnp.int32) - start[sorted_e]
    n_blocks = m // MOE_BLOCK + N_EXPERTS
    n_slots = n_blocks * MOE_BLOCK
    slot_tok = jnp.full((n_slots,), n_tok, jnp.int32).at[dest].set(flat_tok[order])
    slot_g = jnp.zeros((n_slots,), jnp.float32).at[dest].set(flat_g[order])
    block_e = jnp.minimum(jnp.searchsorted(pad_end, jnp.arange(n_blocks, dtype=jnp.int32) * MOE_BLOCK, side='right'),
                          N_EXPERTS - 1)
    h_pad = jnp.concatenate([h, jnp.zeros((1, d), h.dtype)], axis=0)

    def run_block(args):
        tok, g, e = args
        xb = h_pad[tok]
        gu = jnp.dot(xb, w_gate_up[e]) + b_gate_up[e]
        gate = jnp.minimum(gu[:, :D_EXPERT], SWIGLU_LIMIT)
        up = jnp.clip(gu[:, D_EXPERT:], -SWIGLU_LIMIT, SWIGLU_LIMIT)
        act = (up + 1) * (gate * jax.nn.sigmoid(SWIGLU_ALPHA * gate))
        out = jnp.dot(act, w_down[e]) + b_down[e]
        return out * g[:, None].astype(out.dtype)

    outs = lax.map(run_block, (slot_tok.reshape(n_blocks, MOE_BLOCK),
                               slot_g.reshape(n_blocks, MOE_BLOCK), block_e))
    y = jax.ops.segment_sum(outs.reshape(n_slots, d), slot_tok, num_segments=n_tok + 1)
    return y[:n_tok]


def _layer(x, mix_norm_g, w_in, gmlp_ln_g, gmlp_ln_b, gmlp_ws, gmlp_bs, conv_w, conv_b,
           lru_wr, lru_br, lru_wi, lru_bi, lru_lam, out_norm_a, out_norm_b, w_out,
           ffn_norm_g, router_w, router_b, w_gate_up, b_gate_up, w_down, b_down):
    bsz, seq, d = x.shape
    z = jnp.dot(_rmsnorm(x, mix_norm_g), w_in)
    u = z[..., :D_GMLP]
    v = z[..., D_GMLP:2 * D_GMLP]
    xr = z[..., 2 * D_GMLP:2 * D_GMLP + D_LRU]
    gr = z[..., 2 * D_GMLP + D_LRU:]
    out_a = _gmlp_group(u, v, gmlp_ln_g, gmlp_ln_b, gmlp_ws, gmlp_bs)
    out_b = _recurrent_group(xr, gr, conv_w, conv_b, lru_wr, lru_br, lru_wi, lru_bi, lru_lam)
    merged = jnp.concatenate([_rmsnorm(out_a, out_norm_a), _rmsnorm(out_b, out_norm_b)], axis=-1)
    x = x + jnp.dot(merged, w_out)
    hf = _rmsnorm(x, ffn_norm_g).reshape(bsz * seq, d)
    x = x + _moe_ffn(hf, router_w, router_b, w_gate_up, b_gate_up, w_down, b_down).reshape(bsz, seq, d)
    return x


def _trunk(x, mix_norm_g, w_in, gmlp_ln_g, gmlp_ln_b, gmlp_ws, gmlp_bs, conv_w, conv_b,
           lru_wr, lru_br, lru_wi, lru_bi, lru_lam, out_norm_a, out_norm_b, w_out,
           ffn_norm_g, router_w, router_b, w_gate_up, b_gate_up, w_down, b_down, final_norm_g):
    for l in range(DEPTH):
        x = _layer(x, mix_norm_g[l], w_in[l], gmlp_ln_g[l], gmlp_ln_b[l], gmlp_ws[l], gmlp_bs[l],
                   conv_w[l], conv_b[l], lru_wr[l], lru_br[l], lru_wi[l], lru_bi[l], lru_lam[l],
                   out_norm_a[l], out_norm_b[l], w_out[l], ffn_norm_g[l], router_w[l], router_b[l],
                   w_gate_up[l], b_gate_up[l], w_down[l], b_down[l])
    return _rmsnorm(x, final_norm_g)


def setup_inputs(seed: int = 0) -> dict:
    key = jax.random.key(seed)
    ks = jax.random.split(key, 26)
    f32 = jnp.float32

    def nrm(k, shape, scale):
        return jax.random.normal(k, shape, f32) * scale

    a0 = jax.random.uniform(ks[14], (DEPTH, 2, D_LRU), f32, minval=0.9, maxval=0.999) ** (1.0 / LRU_C)
    return {
        'x_prompt': nrm(ks[0], (BATCH, SEQ, D_MODEL), 1.0),
        'x_sample': nrm(ks[1], (DEC_BATCH, DEC_SEQ, D_MODEL), 1.0),
        'mix_norm_g': 1.0 + nrm(ks[2], (DEPTH, D_MODEL), 0.02),
        'w_in': nrm(ks[3], (DEPTH, D_MODEL, D_IN), D_MODEL ** -0.5),
        'gmlp_ln_g': 1.0 + nrm(ks[4], (DEPTH, D_GMLP), 0.02),
        'gmlp_ln_b': nrm(ks[5], (DEPTH, D_GMLP), 0.02),
        'gmlp_ws': nrm(ks[6], (DEPTH, GMLP_HEADS, CHUNK, CHUNK), CHUNK ** -0.5),
        'gmlp_bs': 1.0 + nrm(ks[7], (DEPTH, GMLP_HEADS, CHUNK), 0.02),
        'conv_w': nrm(ks[8], (DEPTH, CONV_WIDTH, D_LRU), CONV_WIDTH ** -0.5),
        'conv_b': nrm(ks[9], (DEPTH, D_LRU), 0.02),
        'lru_wr': nrm(ks[10], (DEPTH, 2, LRU_HEADS, LRU_HEAD_DIM, LRU_HEAD_DIM), LRU_HEAD_DIM ** -0.5),
        'lru_br': nrm(ks[11], (DEPTH, 2, D_LRU), 0.02),
        'lru_wi': nrm(ks[12], (DEPTH, 2, LRU_HEADS, LRU_HEAD_DIM, LRU_HEAD_DIM), LRU_HEAD_DIM ** -0.5),
        'lru_bi': nrm(ks[13], (DEPTH, 2, D_LRU), 0.02),
        'lru_lam': jnp.log(a0) - jnp.log1p(-a0),
        'out_norm_a': 1.0 + nrm(ks[15], (DEPTH, D_GMLP), 0.02),
        'out_norm_b': 1.0 + nrm(ks[16], (DEPTH, D_LRU), 0.02),
        'w_out': nrm(ks[17], (DEPTH, D_MIX, D_MODEL), D_MIX ** -0.5),
        'ffn_norm_g': 1.0 + nrm(ks[18], (DEPTH, D_MODEL), 0.02),
        'router_w': nrm(ks[19], (DEPTH, D_MODEL, N_EXPERTS), D_MODEL ** -0.5),
        'router_b': nrm(ks[20], (DEPTH, N_EXPERTS), 0.01),
        'w_gate_up': nrm(ks[21], (DEPTH, N_EXPERTS, D_MODEL, 2 * D_EXPERT), D_MODEL ** -0.5),
        'b_gate_up': nrm(ks[22], (DEPTH, N_EXPERTS, 2 * D_EXPERT), 0.02),
        'w_down': nrm(ks[23], (DEPTH, N_EXPERTS, D_EXPERT, D_MODEL), D_EXPERT ** -0.5),
        'b_down': nrm(ks[24], (DEPTH, N_EXPERTS, D_MODEL), 0.02),
        'final_norm_g': 1.0 + nrm(ks[25], (D_MODEL,), 0.02),
    }


def reference(x_prompt, x_sample, mix_norm_g, w_in, gmlp_ln_g, gmlp_ln_b, gmlp_ws, gmlp_bs,
              conv_w, conv_b, lru_wr, lru_br, lru_wi, lru_bi, lru_lam, out_norm_a, out_norm_b,
              w_out, ffn_norm_g, router_w, router_b, w_gate_up, b_gate_up, w_down, b_down,
              final_norm_g):
    y_prompt = _trunk(x_prompt, mix_norm_g, w_in, gmlp_ln_g, gmlp_ln_b, gmlp_ws, gmlp_bs, conv_w, conv_b,
                      lru_wr, lru_br, lru_wi, lru_bi, lru_lam, out_norm_a, out_norm_b, w_out,
                      ffn_norm_g, router_w, router_b, w_gate_up, b_gate_up, w_down, b_down, final_norm_g)
    y_sample = _trunk(x_sample, mix_norm_g, w_in, gmlp_ln_g, gmlp_ln_b, gmlp_ws, gmlp_bs, conv_w, conv_b,
                      lru_wr, lru_br, lru_wi, lru_bi, lru_lam, out_norm_a, out_norm_b, w_out,
                      ffn_norm_g, router_w, router_b, w_gate_up, b_gate_up, w_down, b_down, final_norm_g)
    return (y_prompt, y_sample)
```

```python
import functools

import jax
import jax.numpy as jnp
from jax import lax
from jax.experimental import pallas as pl
from jax.experimental.pallas import tpu as pltpu

TOP_K = 4
LRU_C = 8.0
SWIGLU_LIMIT = 7.0
SWIGLU_ALPHA = 1.702
EPS = 1e-6

LANES = 128
VMEM_LIMIT = 56 * 1024 * 1024
NEG_BIG = -1e30
ROW_BLOCK = 1024
SUB_BLOCK = 256

F32 = jnp.float32
BF16 = jnp.bfloat16
U32 = jnp.uint32


def _params(sem):
    return pltpu.CompilerParams(dimension_semantics=sem, vmem_limit_bytes=VMEM_LIMIT)


def _gelu(x):
    return 0.5 * x * (1.0 + jnp.tanh(0.7978845608028654 * (x + 0.044715 * (x * x * x))))


def _rms(x, g):
    return x * lax.rsqrt(jnp.mean(x * x, axis=-1, keepdims=True) + EPS) * g


def _pack_bf16_pair(lo, hi):
    lo_bits = lax.bitcast_convert_type(lo.astype(BF16).astype(F32), U32)
    hi_bits = lax.bitcast_convert_type(hi.astype(BF16).astype(F32), U32)
    return (lo_bits >> 16) | (hi_bits & jnp.uint32(0xFFFF0000))


def _unpack_lo(v):
    return lax.bitcast_convert_type(v << 16, F32)


def _unpack_hi(v):
    return lax.bitcast_convert_type(v & jnp.uint32(0xFFFF0000), F32)


def _inproj_kernel(x_ref, g_ref, w_ref, z_ref, *, n_chunk):
    xn = _rms(x_ref[...], g_ref[...]).astype(BF16)
    for n in range(w_ref.shape[1] // n_chunk):
        cols = slice(n * n_chunk, (n + 1) * n_chunk)
        z_ref[:, cols] = jnp.dot(xn, w_ref[:, cols], preferred_element_type=F32).astype(z_ref.dtype)


def _inproj(x, g, w, tm):
    t, d = x.shape
    d_in = w.shape[1]
    return pl.pallas_call(
        functools.partial(_inproj_kernel, n_chunk=min(1024, d_in)),
        grid=(t // tm,),
        in_specs=[
            pl.BlockSpec((tm, d), lambda i: (i, 0)),
            pl.BlockSpec((1, d), lambda i: (0, 0)),
            pl.BlockSpec((d, d_in), lambda i: (0, 0), pipeline_mode=pl.Buffered(1)),
        ],
        out_specs=pl.BlockSpec((tm, d_in), lambda i: (i, 0)),
        out_shape=jax.ShapeDtypeStruct((t, d_in), BF16),
        compiler_params=_params(("parallel",)),
        name="inproj",
    )(x, g, w)


def _gmlp_kernel(u_ref, v_ref, lng_ref, lnb_ref, ws_ref, bs_ref, ga_ref, o_ref, mix_ref, *, heads, chunk):
    v = _gelu(v_ref[...].astype(F32))
    mu = jnp.mean(v, axis=-1, keepdims=True)
    vc = v - mu
    var = jnp.mean(vc * vc, axis=-1, keepdims=True)
    vn = (vc * lax.rsqrt(var + EPS) * lng_ref[...] + lnb_ref[...]).astype(BF16)
    tm, dg = vn.shape
    hd = dg // heads
    for c in range(tm // chunk):
        rows = slice(c * chunk, (c + 1) * chunk)
        for h in range(heads):
            cols = slice(h * hd, (h + 1) * hd)
            mixed = jnp.dot(ws_ref[h], vn[rows, cols], preferred_element_type=F32) + bs_ref[h]
            mix_ref[rows, cols] = mixed
    out = _gelu(u_ref[...].astype(F32)) * mix_ref[...]
    o_ref[...] = _rms(out, ga_ref[...]).astype(o_ref.dtype)


def _gmlp(z, ln_g, ln_b, ws, bs_b, ga, tm):
    t = z.shape[0]
    heads, chunk, _ = ws.shape
    dg = ln_g.shape[1]
    vec = pl.BlockSpec((1, dg), lambda i: (0, 0))
    return pl.pallas_call(
        functools.partial(_gmlp_kernel, heads=heads, chunk=chunk),
        grid=(t // tm,),
        in_specs=[
            pl.BlockSpec((tm, dg), lambda i: (i, 0)),
            pl.BlockSpec((tm, dg), lambda i: (i, 1)),
            vec, vec,
            pl.BlockSpec(ws.shape, lambda i: (0, 0, 0)),
            pl.BlockSpec(bs_b.shape, lambda i: (0, 0, 0)),
            vec,
        ],
        out_specs=pl.BlockSpec((tm, dg), lambda i: (i, 0)),
        out_shape=jax.ShapeDtypeStruct((t, dg), BF16),
        scratch_shapes=[pltpu.VMEM((tm, dg), F32)],
        compiler_params=_params(("parallel",)),
        name="gmlp",
    )(z, z, ln_g, ln_b, ws, bs_b, ga)


def _shift_rows(x, s, fill, reverse):
    n = x.shape[0]
    row = lax.broadcasted_iota(jnp.int32, x.shape, 0) % 8
    if reverse:
        return jnp.where(row >= 8 - s, fill, pltpu.roll(x, n - s, 0))
    return jnp.where(row < s, fill, pltpu.roll(x, s, 0))


def _lru_scan_chunk(a, b, h, reverse):
    for s in (1, 2, 4):
        b = a * _shift_rows(b, s, 0.0, reverse) + b
        a = a * _shift_rows(a, s, 1.0, reverse)
    n_grp = a.shape[0] // 8
    out = [None] * n_grp
    order = range(n_grp - 1, -1, -1) if reverse else range(n_grp)
    edge = 0 if reverse else 7
    for gi in order:
        rows = slice(gi * 8, gi * 8 + 8)
        hg = b[rows] + a[rows] * h
        out[gi] = hg
        h = hg[edge:edge + 1]
    return jnp.concatenate(out, axis=0), h


def _rglru_kernel(xr_ref, gr_ref, cw_ref, cb_ref, wri_ref, bri_ref, lam_ref, o_ref, hf_ref, *, tc):
    seq, hd = xr_ref.shape
    n_ck = seq // tc
    halo = 16
    cw = cw_ref[...]
    cb = cb_ref[...]

    def conv_chunk(c):
        t0 = pl.multiple_of(c * tc, tc)
        main = xr_ref[pl.ds(t0, tc), :].astype(F32)
        p0 = pl.multiple_of(jnp.maximum(t0 - halo, 0), halo)
        n0 = pl.multiple_of(jnp.minimum(t0 + tc, seq - halo), halo)
        prev = jnp.where(c > 0, xr_ref[pl.ds(p0, halo), :].astype(F32), 0.0)
        nxt = jnp.where(c < n_ck - 1, xr_ref[pl.ds(n0, halo), :].astype(F32), 0.0)
        ext = jnp.concatenate([prev, main, nxt], axis=0)
        n = ext.shape[0]
        acc = cb + cw[2:3] * main
        acc = acc + cw[0:1] * pltpu.roll(ext, 2, 0)[halo:halo + tc]
        acc = acc + cw[1:2] * pltpu.roll(ext, 1, 0)[halo:halo + tc]
        acc = acc + cw[3:4] * pltpu.roll(ext, n - 1, 0)[halo:halo + tc]
        return t0, acc

    def direction(d, c, h):
        t0, xc = conv_chunk(c)
        g = jnp.dot(xc.astype(BF16), wri_ref[d], preferred_element_type=F32) + bri_ref[d]
        r = jax.nn.sigmoid(g[:, :hd])
        i = jax.nn.sigmoid(g[:, hd:])
        nl = -lam_ref[d]
        softplus = jnp.maximum(nl, 0.0) + jnp.log1p(jnp.exp(-jnp.abs(nl)))
        log_a = (-LRU_C * softplus) * r
        a = jnp.exp(log_a)
        b = jnp.sqrt(-jnp.tanh(log_a) * (a * a + 1.0)) * i * xc
        hs, h = _lru_scan_chunk(a, b, h, reverse=(d == 1))
        return t0, hs, h

    def fwd(c, h):
        t0, hs, h = direction(0, c, h)
        hf_ref[pl.ds(t0, tc), :] = hs
        return h

    def bwd(k, h):
        t0, hs, h = direction(1, n_ck - 1 - k, h)
        tot = hf_ref[pl.ds(t0, tc), :] + hs
        o_ref[pl.ds(t0, tc), :] = (_gelu(gr_ref[pl.ds(t0, tc), :].astype(F32)) * tot).astype(o_ref.dtype)
        return h

    h0 = jnp.zeros((1, hd), F32)
    lax.fori_loop(0, n_ck, fwd, h0)
    lax.fori_loop(0, n_ck, bwd, h0)


def _rglru(z, prev_out, row0, n_seq, seq, col_x, col_g, cw, cb, wri, bri, lam, tc):
    t = z.shape[0]
    _, heads, hd, _ = wri.shape
    sb0 = row0 // seq
    args = [z, z, cw, cb, wri, bri, lam]
    in_specs = [
        pl.BlockSpec((seq, hd), lambda b, h: (sb0 + b, col_x + h)),
        pl.BlockSpec((seq, hd), lambda b, h: (sb0 + b, col_g + h)),
        pl.BlockSpec((cw.shape[0], hd), lambda b, h: (0, h)),
        pl.BlockSpec((1, hd), lambda b, h: (0, h)),
        pl.BlockSpec((2, None, hd, 2 * hd), lambda b, h: (0, h, 0, 0)),
        pl.BlockSpec((2, None, 1, 2 * hd), lambda b, h: (0, h, 0, 0)),
        pl.BlockSpec((2, None, 1, hd), lambda b, h: (0, h, 0, 0)),
    ]
    aliases = {}
    kern = functools.partial(_rglru_kernel, tc=tc)
    if prev_out is not None:
        args.append(prev_out)
        in_specs.append(pl.BlockSpec(memory_space=pl.ANY))
        aliases = {len(args) - 1: 0}
        kern = lambda *refs: _rglru_kernel(*refs[:7], *refs[8:], tc=tc)
    return pl.pallas_call(
        kern,
        grid=(n_seq, heads),
        in_specs=in_specs,
        out_specs=pl.BlockSpec((seq, hd), lambda b, h: (sb0 + b, h)),
        out_shape=jax.ShapeDtypeStruct((t, heads * hd), BF16),
        scratch_shapes=[pltpu.VMEM((seq, hd), F32)],
        input_output_aliases=aliases,
        compiler_params=_params(("parallel", "parallel")),
        name="rglru",
    )(*args)


def _outproj_kernel(x_ref, a_ref, b_ref, gb_ref, wo_ref, gf_ref, rw_ref, rb_ref,
                    x1_ref, hp_ref, ri_ref, rg_ref, cnt_ref, carry_ref):
    step = pl.program_id(0)

    @pl.when(step == 0)
    def _():
        carry_ref[...] = jnp.zeros_like(carry_ref)

    da = a_ref.shape[1]
    obn = _rms(b_ref[...].astype(F32), gb_ref[...]).astype(BF16)
    y = jnp.dot(a_ref[...], wo_ref[:da], preferred_element_type=F32)
    y = y + jnp.dot(obn, wo_ref[da:], preferred_element_type=F32)
    x1 = x_ref[...] + y
    x1_ref[...] = x1
    hf = _rms(x1, gf_ref[...])
    half = hf.shape[1] // 2
    hp_ref[...] = _pack_bf16_pair(hf[:, :half], hf[:, half:])

    logits = jnp.dot(hf.astype(BF16), rw_ref[...], preferred_element_type=F32) + rb_ref[...]
    tm = logits.shape[0]
    lane = lax.broadcasted_iota(jnp.int32, logits.shape, 1)
    lane_f = lane.astype(F32)
    work = logits
    sels, vals = [], []
    for _ in range(TOP_K):
        m = jnp.max(work, axis=-1, keepdims=True)
        sel = jnp.min(jnp.where(work == m, lane_f, float(LANES)), axis=-1, keepdims=True)
        hit = lane_f == sel
        sels.append(hit)
        vals.append(m)
        work = jnp.where(hit, -jnp.inf, work)
    exps = [jnp.exp(v - vals[0]) for v in vals]
    denom = exps[0]
    for e in exps[1:]:
        denom = denom + e

    onehot = sels[0]
    for hit in sels[1:]:
        onehot = onehot | hit
    cnt = onehot.astype(F32)
    rr = lax.broadcasted_iota(jnp.int32, (tm, tm), 0)
    cc = lax.broadcasted_iota(jnp.int32, (tm, tm), 1)
    tril = (rr > cc).astype(BF16)
    before = jnp.dot(tril, cnt.astype(BF16), preferred_element_type=F32) + carry_ref[...]

    ri = jnp.zeros(logits.shape, F32)
    rg = jnp.zeros(logits.shape, F32)
    for k in range(TOP_K):
        e_k = jnp.sum(jnp.where(sels[k], lane_f, 0.0), axis=-1, keepdims=True)
        rank_k = jnp.sum(jnp.where(sels[k], before, 0.0), axis=-1, keepdims=True)
        ri = jnp.where(lane == k, e_k, ri)
        ri = jnp.where(lane == TOP_K + k, rank_k, ri)
        rg = jnp.where(lane == k, exps[k] / denom, rg)
    ri_ref[...] = ri.astype(jnp.int32)
    rg_ref[...] = rg
    carry_ref[...] = carry_ref[...] + jnp.sum(cnt, axis=0, keepdims=True)
    cnt_ref[...] = carry_ref[...].astype(jnp.int32)


def _outproj(x, out_a, out_b, gb, wo, gf, rw, rb, tm):
    t, d = x.shape
    da = out_a.shape[1]
    db = out_b.shape[1]
    row = lambda i: (i, 0)
    fixed = lambda i: (0, 0)
    return pl.pallas_call(
        _outproj_kernel,
        grid=(t // tm,),
        in_specs=[
            pl.BlockSpec((tm, d), row),
            pl.BlockSpec((tm, da), row),
            pl.BlockSpec((tm, db), row),
            pl.BlockSpec((1, db), fixed),
            pl.BlockSpec(wo.shape, fixed, pipeline_mode=pl.Buffered(1)),
            pl.BlockSpec((1, d), fixed),
            pl.BlockSpec(rw.shape, fixed),
            pl.BlockSpec((1, LANES), fixed),
        ],
        out_specs=[
            pl.BlockSpec((tm, d), row),
            pl.BlockSpec((tm, d // 2), row),
            pl.BlockSpec((tm, LANES), row),
            pl.BlockSpec((tm, LANES), row),
            pl.BlockSpec((1, LANES), fixed),
        ],
        out_shape=[
            jax.ShapeDtypeStruct((t, d), F32),
            jax.ShapeDtypeStruct((t, d // 2), U32),
            jax.ShapeDtypeStruct((t, LANES), jnp.int32),
            jax.ShapeDtypeStruct((t, LANES), F32),
            jax.ShapeDtypeStruct((1, LANES), jnp.int32),
        ],
        scratch_shapes=[pltpu.VMEM((1, LANES), F32)],
        compiler_params=_params(("arbitrary",)),
        name="outproj_router",
    )(x, out_a, out_b, gb, wo, gf, rw, rb)


def _dispatch_kernel(zrow_ref, zcnt_ref, dest_ref, hp_ref, xs_ref, zero_ref, sem, zsem, *, sub):
    step = pl.program_id(0)
    tm = hp_ref.shape[0]
    n_exp = zrow_ref.shape[0]

    def zero_copy(e):
        return pltpu.make_async_copy(zero_ref, xs_ref.at[pl.ds(pl.multiple_of(zrow_ref[e], sub), sub), :], zsem)

    @pl.when(step == 0)
    def _():
        zero_ref[...] = jnp.zeros_like(zero_ref)
        for e in range(n_exp):
            @pl.when(zcnt_ref[e] > 0)
            def _():
                zero_copy(e).start()
        for e in range(n_exp):
            @pl.when(zcnt_ref[e] > 0)
            def _():
                zero_copy(e).wait()

    def row_copy(r, k):
        d = dest_ref[0, 0, r * TOP_K + k]
        return pltpu.make_async_copy(hp_ref.at[pl.ds(r, 1), :], xs_ref.at[pl.ds(d, 1), :], sem)

    def issue(r, _):
        for k in range(TOP_K):
            row_copy(r, k).start()
        return 0

    def drain(r, _):
        for k in range(TOP_K):
            row_copy(r, k).wait()
        return 0

    lax.fori_loop(0, tm, issue, 0)
    lax.fori_loop(0, tm, drain, 0)


def _dispatch(zrow, zcnt, dest, hp, n_slots, sub, tm):
    t, half = hp.shape
    dest3 = dest.reshape(t // tm, 1, tm * TOP_K)
    grid_spec = pltpu.PrefetchScalarGridSpec(
        num_scalar_prefetch=2,
        grid=(t // tm,),
        in_specs=[
            pl.BlockSpec((1, 1, tm * TOP_K), lambda i, *_: (i, 0, 0), memory_space=pltpu.SMEM),
            pl.BlockSpec((tm, half), lambda i, *_: (i, 0)),
        ],
        out_specs=pl.BlockSpec(memory_space=pl.ANY),
        scratch_shapes=[pltpu.VMEM((sub, half), U32), pltpu.SemaphoreType.DMA, pltpu.SemaphoreType.DMA],
    )
    return pl.pallas_call(
        functools.partial(_dispatch_kernel, sub=sub),
        grid_spec=grid_spec,
        out_shape=jax.ShapeDtypeStruct((n_slots, half), U32),
        compiler_params=_params(("arbitrary",)),
        name="dispatch",
    )(zrow, zcnt, dest3, hp)


def _experts_kernel(sbe_ref, nsub_ref, nval_ref, xs_ref, wg_ref, wu_ref, bg_ref, bu_ref, wd_ref, bd_ref,
                    os_ref, xb_ref, acc_ref, wgb_ref, wub_ref, wdb_ref, *, sub):
    s = pl.program_id(0)
    j = pl.program_id(1)
    n_j = pl.num_programs(1)
    n_sub = jnp.where(s < nval_ref[0], nsub_ref[s], 0)
    half = xs_ref.shape[1]

    def rows_of(i):
        return pl.ds(pl.multiple_of(i * sub, sub), sub)

    @pl.when(j == 0)
    def _():
        def unpack(i, _):
            v = xs_ref[rows_of(i), :]
            xb_ref[rows_of(i), :half] = _unpack_lo(v).astype(BF16)
            xb_ref[rows_of(i), half:] = _unpack_hi(v).astype(BF16)
            return 0
        lax.fori_loop(0, n_sub, unpack, 0)

    @pl.when(n_sub > 0)
    def _():
        wgb_ref[...] = wg_ref[...].astype(BF16)
        wub_ref[...] = wu_ref[...].astype(BF16)
        wdb_ref[...] = wd_ref[...].astype(BF16)

    def ffn(i, _):
        x = xb_ref[rows_of(i), :]
        gate = jnp.dot(x, wgb_ref[...], preferred_element_type=F32) + bg_ref[...]
        up = jnp.dot(x, wub_ref[...], preferred_element_type=F32) + bu_ref[...]
        gate = jnp.minimum(gate, SWIGLU_LIMIT)
        up = jnp.clip(up, -SWIGLU_LIMIT, SWIGLU_LIMIT)
        act = (up + 1.0) * (gate * jax.nn.sigmoid(SWIGLU_ALPHA * gate))
        part = jnp.dot(act.astype(BF16), wdb_ref[...], preferred_element_type=F32)

        @pl.when(j == 0)
        def _():
            acc_ref[rows_of(i), :] = part + bd_ref[...]

        @pl.when(j > 0)
        def _():
            acc_ref[rows_of(i), :] += part
        return 0

    lax.fori_loop(0, n_sub, ffn, 0)

    @pl.when(j == n_j - 1)
    def _():
        def pack(i, _):
            o = acc_ref[rows_of(i), :]
            os_ref[rows_of(i), :] = _pack_bf16_pair(o[:, :half], o[:, half:])
            return 0
        lax.fori_loop(0, n_sub, pack, 0)


def _experts(sbe, nsub, nval, xs, w_gu, b_gu, w_d, b_d, rblk, sub, tj):
    n_slots, half = xs.shape
    n_exp, d, two_de = w_gu.shape
    de = two_de // 2
    n_j = de // tj
    n_sb = n_slots // rblk

    def live(s, nval):
        return jnp.minimum(s, nval[0] - 1)

    def jj(s, j, nval):
        return jnp.where(s < nval[0], j, n_j - 1)

    grid_spec = pltpu.PrefetchScalarGridSpec(
        num_scalar_prefetch=3,
        grid=(n_sb, n_j),
        in_specs=[
            pl.BlockSpec((rblk, half), lambda s, j, sbe, nsub, nval: (live(s, nval), 0)),
            pl.BlockSpec((None, d, tj), lambda s, j, sbe, nsub, nval: (sbe[s], 0, jj(s, j, nval))),
            pl.BlockSpec((None, d, tj), lambda s, j, sbe, nsub, nval: (sbe[s], 0, n_j + jj(s, j, nval))),
            pl.BlockSpec((None, 1, tj), lambda s, j, sbe, nsub, nval: (sbe[s], 0, jj(s, j, nval))),
            pl.BlockSpec((None, 1, tj), lambda s, j, sbe, nsub, nval: (sbe[s], 0, n_j + jj(s, j, nval))),
            pl.BlockSpec((None, tj, d), lambda s, j, sbe, nsub, nval: (sbe[s], jj(s, j, nval), 0)),
            pl.BlockSpec((None, 1, d), lambda s, j, sbe, nsub, nval: (sbe[s], 0, 0)),
        ],
        out_specs=pl.BlockSpec((rblk, half), lambda s, j, sbe, nsub, nval: (live(s, nval), 0)),
        scratch_shapes=[
            pltpu.VMEM((rblk, d), BF16),
            pltpu.VMEM((rblk, d), F32),
            pltpu.VMEM((d, tj), BF16),
            pltpu.VMEM((d, tj), BF16),
            pltpu.VMEM((tj, d), BF16),
        ],
    )
    return pl.pallas_call(
        functools.partial(_experts_kernel, sub=sub),
        grid_spec=grid_spec,
        out_shape=jax.ShapeDtypeStruct((n_slots, half), U32),
        compiler_params=_params(("arbitrary", "arbitrary")),
        name="experts",
    )(sbe, nsub, nval, xs, w_gu, w_gu, b_gu, b_gu, w_d, b_d)


def _combine_kernel(dest_ref, x1_ref, rg_ref, gfin_ref, os_ref, y_ref, gbuf_ref, sem, *, final_norm):
    tm = x1_ref.shape[0]
    half = gbuf_ref.shape[2]

    def row_copy(r, k):
        d = dest_ref[0, 0, r * TOP_K + k]
        return pltpu.make_async_copy(os_ref.at[pl.ds(d, 1), :], gbuf_ref.at[k, pl.ds(r, 1), :], sem)

    def issue(r, _):
        for k in range(TOP_K):
            row_copy(r, k).start()
        return 0

    def drain(r, _):
        for k in range(TOP_K):
            row_copy(r, k).wait()
        return 0

    lax.fori_loop(0, tm, issue, 0)
    lax.fori_loop(0, tm, drain, 0)

    x1 = x1_ref[...]
    rg = rg_ref[...]
    lo = x1[:, :half]
    hi = x1[:, half:]
    for k in range(TOP_K):
        g = rg[:, k:k + 1]
        v = gbuf_ref[k]
        lo = lo + g * _unpack_lo(v)
        hi = hi + g * _unpack_hi(v)
    if final_norm:
        ms = (jnp.sum(lo * lo, axis=-1, keepdims=True) + jnp.sum(hi * hi, axis=-1, keepdims=True)) / (2 * half)
        scale = lax.rsqrt(ms + EPS)
        gf = gfin_ref[...]
        lo = lo * scale * gf[:, :half]
        hi = hi * scale * gf[:, half:]
    y_ref[:, :half] = lo
    y_ref[:, half:] = hi


def _combine(dest, x1, rg, gfin, os_, tm, final_norm):
    t, d = x1.shape
    half = d // 2
    dest3 = dest.reshape(t // tm, 1, tm * TOP_K)
    return pl.pallas_call(
        functools.partial(_combine_kernel, final_norm=final_norm),
        grid=(t // tm,),
        in_specs=[
            pl.BlockSpec((1, 1, tm * TOP_K), lambda i: (i, 0, 0), memory_space=pltpu.SMEM),
            pl.BlockSpec((tm, d), lambda i: (i, 0)),
            pl.BlockSpec((tm, LANES), lambda i: (i, 0)),
            pl.BlockSpec((1, d), lambda i: (0, 0)),
            pl.BlockSpec(memory_space=pl.ANY),
        ],
        out_specs=pl.BlockSpec((tm, d), lambda i: (i, 0)),
        out_shape=jax.ShapeDtypeStruct((t, d), F32),
        scratch_shapes=[pltpu.VMEM((TOP_K, tm, half), U32), pltpu.SemaphoreType.DMA],
        compiler_params=_params(("arbitrary",)),
        name="combine",
    )(dest3, x1, rg, gfin, os_)


def _tile(n, pref):
    while n % pref:
        pref //= 2
    return pref


def _layer(x, seqs, p, final_g):
    t, d = x.shape
    dg = p["gmlp_ln_g"].shape[-1]
    chunk = p["gmlp_ws"].shape[-1]
    d_lru = p["conv_w"].shape[-1]
    heads_l, hd_l = p["lru_wr"].shape[1], p["lru_wr"].shape[2]
    n_exp = p["router_w"].shape[-1]
    de = p["w_down"].shape[1]

    tm = _tile(t, 512)
    z = _inproj(x, p["mix_norm_g"][None], p["w_in"].astype(BF16), tm)

    bs_b = jnp.broadcast_to(p["gmlp_bs"][:, :, None], p["gmlp_ws"].shape).astype(F32)
    out_a = _gmlp(z, p["gmlp_ln_g"][None], p["gmlp_ln_b"][None], p["gmlp_ws"].astype(BF16), bs_b,
                  p["out_norm_a"][None], max(tm, chunk))

    wri = jnp.concatenate([p["lru_wr"], p["lru_wi"]], axis=-1).astype(BF16)
    bri = jnp.concatenate([p["lru_br"].reshape(2, heads_l, 1, hd_l),
                           p["lru_bi"].reshape(2, heads_l, 1, hd_l)], axis=-1)
    lam = p["lru_lam"].reshape(2, heads_l, 1, hd_l)
    out_b = None
    for row0, n_seq, seq in seqs:
        out_b = _rglru(z, out_b, row0, n_seq, seq, 2 * dg // hd_l, (2 * dg + d_lru) // hd_l,
                       p["conv_w"], p["conv_b"][None], wri, bri, lam, _tile(seq, 512))

    rw = jnp.zeros((d, LANES), BF16).at[:, :n_exp].set(p["router_w"].astype(BF16))
    rb = jnp.full((1, LANES), NEG_BIG, F32).at[0, :n_exp].set(p["router_b"])
    x1, hp, ri, rg, cnt = _outproj(x, out_a, out_b, p["out_norm_b"][None], p["w_out"].astype(BF16),
                                   p["ffn_norm_g"][None], rw, rb, tm)

    m = t * TOP_K
    rblk = _tile(m, ROW_BLOCK)
    sub = min(SUB_BLOCK, rblk)
    counts = cnt[0, :n_exp]
    padded = (counts + rblk - 1) // rblk * rblk
    pad_end = jnp.cumsum(padded)
    pad_start = pad_end - padded
    n_sb = m // rblk + n_exp
    n_slots = n_sb * rblk
    top_e = ri[:, :TOP_K]
    dest = (jnp.take(pad_start, top_e) + ri[:, TOP_K:2 * TOP_K]).reshape(m)
    sb_row = jnp.arange(n_sb, dtype=jnp.int32) * rblk
    sbe = jnp.minimum(jnp.searchsorted(pad_end, sb_row, side="right"), n_exp - 1).astype(jnp.int32)
    sb_rows = jnp.clip(jnp.take(counts, sbe) - (sb_row - jnp.take(pad_start, sbe)), 0, rblk)
    nsub = ((sb_rows + sub - 1) // sub).astype(jnp.int32)
    nval = (pad_end[-1:] // rblk).astype(jnp.int32)
    sbe = jnp.take(sbe, jnp.minimum(jnp.arange(n_sb), nval[0] - 1))
    zrow = (pad_start + jnp.maximum((counts + sub - 1) // sub - 1, 0) * sub).astype(jnp.int32)

    xs = _dispatch(zrow, counts, dest, hp, n_slots, sub, _tile(t, 256))
    tj = _tile(de, 256)
    os_ = _experts(sbe, nsub, nval, xs, p["w_gate_up"], p["b_gate_up"][:, None, :], p["w_down"],
                   p["b_down"][:, None, :], rblk, sub, tj)
    gfin = jnp.ones((1, d), F32) if final_g is None else final_g[None]
    return _combine(dest, x1, rg, gfin, os_, _tile(t, 256), final_g is not None)


_LAYER_KEYS = ("mix_norm_g", "w_in", "gmlp_ln_g", "gmlp_ln_b", "gmlp_ws", "gmlp_bs", "conv_w", "conv_b",
               "lru_wr", "lru_br", "lru_wi", "lru_bi", "lru_lam", "out_norm_a", "out_norm_b", "w_out",
               "ffn_norm_g", "router_w", "router_b", "w_gate_up", "b_gate_up", "w_down", "b_down")


def kernel(x_prompt, x_sample, mix_norm_g, w_in, gmlp_ln_g, gmlp_ln_b, gmlp_ws, gmlp_bs, conv_w, conv_b, lru_wr, lru_br, lru_wi, lru_bi, lru_lam, out_norm_a, out_norm_b, w_out, ffn_norm_g, router_w, router_b, w_gate_up, b_gate_up, w_down, b_down, final_norm_g):
    stacked = dict(zip(_LAYER_KEYS, (mix_norm_g, w_in, gmlp_ln_g, gmlp_ln_b, gmlp_ws, gmlp_bs, conv_w, conv_b,
                                     lru_wr, lru_br, lru_wi, lru_bi, lru_lam, out_norm_a, out_norm_b, w_out,
                                     ffn_norm_g, router_w, router_b, w_gate_up, b_gate_up, w_down, b_down)))
    depth = w_in.shape[0]
    bp, sp, d = x_prompt.shape
    bs, ss, _ = x_sample.shape
    x = jnp.concatenate([x_prompt.reshape(bp * sp, d), x_sample.reshape(bs * ss, d)], axis=0)
    seqs = [(0, bp, sp), (bp * sp, bs, ss)]
    for layer in range(depth):
        p = {k: v[layer] for k, v in stacked.items()}
        x = _layer(x, seqs, p, final_norm_g if layer == depth - 1 else None)
    return x[:bp * sp].reshape(bp, sp, d), x[bp * sp:].reshape(bs, ss, d)
```

```python
import functools

import jax
import jax.numpy as jnp
from jax import lax
from jax.experimental import pallas as pl
from jax.experimental.pallas import tpu as pltpu

TOP_K = 4
LRU_C = 8.0
SWIGLU_LIMIT = 7.0
SWIGLU_ALPHA = 1.702
EPS = 1e-6

LANES = 128
VMEM_LIMIT = 56 * 1024 * 1024
NEG_BIG = -1e30
ROW_BLOCK = 512
SUB_BLOCK = 128
WEIGHT_CHUNK_ROWS = 128

F32 = jnp.float32
BF16 = jnp.bfloat16
U32 = jnp.uint32


def _params(sem):
    return pltpu.CompilerParams(dimension_semantics=sem, vmem_limit_bytes=VMEM_LIMIT)


def _gelu(x):
    return 0.5 * x * (1.0 + jnp.tanh(0.7978845608028654 * (x + 0.044715 * (x * x * x))))


def _rms(x, g):
    return x * lax.rsqrt(jnp.mean(x * x, axis=-1, keepdims=True) + EPS) * g


def _pack_bf16_pair(lo, hi):
    lo_bits = lax.bitcast_convert_type(lo.astype(BF16).astype(F32), U32)
    hi_bits = lax.bitcast_convert_type(hi.astype(BF16).astype(F32), U32)
    return (lo_bits >> 16) | (hi_bits & jnp.uint32(0xFFFF0000))


def _unpack_lo(v):
    return lax.bitcast_convert_type(v << 16, F32)


def _unpack_hi(v):
    return lax.bitcast_convert_type(v & jnp.uint32(0xFFFF0000), F32)


def _inproj_kernel(x_ref, g_ref, w_ref, z_ref, *, n_chunk):
    xn = _rms(x_ref[...], g_ref[...]).astype(BF16)
    for n in range(w_ref.shape[1] // n_chunk):
        cols = slice(n * n_chunk, (n + 1) * n_chunk)
        z_ref[:, cols] = jnp.dot(xn, w_ref[:, cols], preferred_element_type=F32).astype(z_ref.dtype)


def _inproj(x, g, w, tm):
    t, d = x.shape
    d_in = w.shape[1]
    return pl.pallas_call(
        functools.partial(_inproj_kernel, n_chunk=min(1024, d_in)),
        grid=(t // tm,),
        in_specs=[
            pl.BlockSpec((tm, d), lambda i: (i, 0)),
            pl.BlockSpec((1, d), lambda i: (0, 0)),
            pl.BlockSpec((d, d_in), lambda i: (0, 0), pipeline_mode=pl.Buffered(1)),
        ],
        out_specs=pl.BlockSpec((tm, d_in), lambda i: (i, 0)),
        out_shape=jax.ShapeDtypeStruct((t, d_in), BF16),
        compiler_params=_params(("arbitrary",)),
        name="inproj",
    )(x, g, w)


def _gmlp_kernel(u_ref, v_ref, lng_ref, lnb_ref, ws_ref, bs_ref, ga_ref, o_ref, mix_ref, *, heads, chunk):
    v = _gelu(v_ref[...].astype(F32))
    mu = jnp.mean(v, axis=-1, keepdims=True)
    vc = v - mu
    var = jnp.mean(vc * vc, axis=-1, keepdims=True)
    vn = (vc * lax.rsqrt(var + EPS) * lng_ref[...] + lnb_ref[...]).astype(BF16)
    tm, dg = vn.shape
    hd = dg // heads
    for c in range(tm // chunk):
        rows = slice(c * chunk, (c + 1) * chunk)
        for h in range(heads):
            cols = slice(h * hd, (h + 1) * hd)
            mixed = jnp.dot(ws_ref[h], vn[rows, cols], preferred_element_type=F32) + bs_ref[h]
            mix_ref[rows, cols] = mixed
    out = _gelu(u_ref[...].astype(F32)) * mix_ref[...]
    o_ref[...] = _rms(out, ga_ref[...]).astype(o_ref.dtype)


def _gmlp(z, ln_g, ln_b, ws, bs_b, ga, tm):
    t = z.shape[0]
    heads, chunk, _ = ws.shape
    dg = ln_g.shape[1]
    vec = pl.BlockSpec((1, dg), lambda i: (0, 0))
    return pl.pallas_call(
        functools.partial(_gmlp_kernel, heads=heads, chunk=chunk),
        grid=(t // tm,),
        in_specs=[
            pl.BlockSpec((tm, dg), lambda i: (i, 0)),
            pl.BlockSpec((tm, dg), lambda i: (i, 1)),
            vec, vec,
            pl.BlockSpec(ws.shape, lambda i: (0, 0, 0)),
            pl.BlockSpec(bs_b.shape, lambda i: (0, 0, 0)),
            vec,
        ],
        out_specs=pl.BlockSpec((tm, dg), lambda i: (i, 0)),
        out_shape=jax.ShapeDtypeStruct((t, dg), BF16),
        scratch_shapes=[pltpu.VMEM((tm, dg), F32)],
        compiler_params=_params(("arbitrary",)),
        name="gmlp",
    )(z, z, ln_g, ln_b, ws, bs_b, ga)


def _shift_rows(x, s, fill, reverse):
    n = x.shape[0]
    row = lax.broadcasted_iota(jnp.int32, x.shape, 0) % 8
    if reverse:
        return jnp.where(row >= 8 - s, fill, pltpu.roll(x, n - s, 0))
    return jnp.where(row < s, fill, pltpu.roll(x, s, 0))


def _lru_scan_chunk(a, b, h, reverse):
    for s in (1, 2, 4):
        b = a * _shift_rows(b, s, 0.0, reverse) + b
        a = a * _shift_rows(a, s, 1.0, reverse)
    n_grp = a.shape[0] // 8
    out = [None] * n_grp
    order = range(n_grp - 1, -1, -1) if reverse else range(n_grp)
    edge = 0 if reverse else 7
    for gi in order:
        rows = slice(gi * 8, gi * 8 + 8)
        hg = b[rows] + a[rows] * h
        out[gi] = hg
        h = hg[edge:edge + 1]
    return jnp.concatenate(out, axis=0), h


def _rglru_kernel(xr_ref, gr_ref, cw_ref, cb_ref, wri_ref, bri_ref, lam_ref, o_ref, hf_ref, *, tc):
    seq, hd = xr_ref.shape
    n_ck = seq // tc
    halo = 16
    cw = cw_ref[...]
    cb = cb_ref[...]

    def conv_chunk(c):
        t0 = pl.multiple_of(c * tc, tc)
        main = xr_ref[pl.ds(t0, tc), :].astype(F32)
        p0 = pl.multiple_of(jnp.maximum(t0 - halo, 0), halo)
        n0 = pl.multiple_of(jnp.minimum(t0 + tc, seq - halo), halo)
        prev = jnp.where(c > 0, xr_ref[pl.ds(p0, halo), :].astype(F32), 0.0)
        nxt = jnp.where(c < n_ck - 1, xr_ref[pl.ds(n0, halo), :].astype(F32), 0.0)
        ext = jnp.concatenate([prev, main, nxt], axis=0)
        n = ext.shape[0]
        acc = cb + cw[2:3] * main
        acc = acc + cw[0:1] * pltpu.roll(ext, 2, 0)[halo:halo + tc]
        acc = acc + cw[1:2] * pltpu.roll(ext, 1, 0)[halo:halo + tc]
        acc = acc + cw[3:4] * pltpu.roll(ext, n - 1, 0)[halo:halo + tc]
        return t0, acc

    def direction(d, c, h):
        t0, xc = conv_chunk(c)
        g = jnp.dot(xc.astype(BF16), wri_ref[d], preferred_element_type=F32) + bri_ref[d]
        r = jax.nn.sigmoid(g[:, :hd])
        i = jax.nn.sigmoid(g[:, hd:])
        nl = -lam_ref[d]
        softplus = jnp.maximum(nl, 0.0) + jnp.log1p(jnp.exp(-jnp.abs(nl)))
        log_a = (-LRU_C * softplus) * r
        a = jnp.exp(log_a)
        b = jnp.sqrt(-jnp.tanh(log_a) * (a * a + 1.0)) * i * xc
        hs, h = _lru_scan_chunk(a, b, h, reverse=(d == 1))
        return t0, hs, h

    def fwd(c, h):
        t0, hs, h = direction(0, c, h)
        hf_ref[pl.ds(t0, tc), :] = hs
        return h

    def bwd(k, h):
        t0, hs, h = direction(1, n_ck - 1 - k, h)
        tot = hf_ref[pl.ds(t0, tc), :] + hs
        o_ref[pl.ds(t0, tc), :] = (_gelu(gr_ref[pl.ds(t0, tc), :].astype(F32)) * tot).astype(o_ref.dtype)
        return h

    h0 = jnp.zeros((1, hd), F32)
    lax.fori_loop(0, n_ck, fwd, h0)
    lax.fori_loop(0, n_ck, bwd, h0)


def _rglru(z, prev_out, row0, n_seq, seq, col_x, col_g, cw, cb, wri, bri, lam, tc):
    t = z.shape[0]
    _, heads, hd, _ = wri.shape
    sb0 = row0 // seq
    args = [z, z, cw, cb, wri, bri, lam]
    in_specs = [
        pl.BlockSpec((seq, hd), lambda b, h: (sb0 + b, col_x + h)),
        pl.BlockSpec((seq, hd), lambda b, h: (sb0 + b, col_g + h)),
        pl.BlockSpec((cw.shape[0], hd), lambda b, h: (0, h)),
        pl.BlockSpec((1, hd), lambda b, h: (0, h)),
        pl.BlockSpec((2, None, hd, 2 * hd), lambda b, h: (0, h, 0, 0)),
        pl.BlockSpec((2, None, 1, 2 * hd), lambda b, h: (0, h, 0, 0)),
        pl.BlockSpec((2, None, 1, hd), lambda b, h: (0, h, 0, 0)),
    ]
    aliases = {}
    kern = functools.partial(_rglru_kernel, tc=tc)
    if prev_out is not None:
        args.append(prev_out)
        in_specs.append(pl.BlockSpec(memory_space=pl.ANY))
        aliases = {len(args) - 1: 0}
        kern = lambda *refs: _rglru_kernel(*refs[:7], *refs[8:], tc=tc)
    return pl.pallas_call(
        kern,
        grid=(n_seq, heads),
        in_specs=in_specs,
        out_specs=pl.BlockSpec((seq, hd), lambda b, h: (sb0 + b, h)),
        out_shape=jax.ShapeDtypeStruct((t, heads * hd), BF16),
        scratch_shapes=[pltpu.VMEM((seq, hd), F32)],
        input_output_aliases=aliases,
        compiler_params=_params(("arbitrary", "arbitrary")),
        name="rglru",
    )(*args)


def _outproj_kernel(x_ref, a_ref, b_ref, gb_ref, wo_ref, gf_ref, rw_ref, rb_ref,
                    x1_ref, hp_ref, ri_ref, rg_ref, cnt_ref, carry_ref):
    step = pl.program_id(0)

    @pl.when(step == 0)
    def _():
        carry_ref[...] = jnp.zeros_like(carry_ref)

    da = a_ref.shape[1]
    obn = _rms(b_ref[...].astype(F32), gb_ref[...]).astype(BF16)
    y = jnp.dot(a_ref[...], wo_ref[:da], preferred_element_type=F32)
    y = y + jnp.dot(obn, wo_ref[da:], preferred_element_type=F32)
    x1 = x_ref[...] + y
    x1_ref[...] = x1
    hf = _rms(x1, gf_ref[...])
    half = hf.shape[1] // 2
    hp_ref[...] = _pack_bf16_pair(hf[:, :half], hf[:, half:])

    logits = jnp.dot(hf.astype(BF16), rw_ref[...], preferred_element_type=F32) + rb_ref[...]
    tm = logits.shape[0]
    lane = lax.broadcasted_iota(jnp.int32, logits.shape, 1)
    lane_f = lane.astype(F32)
    work = logits
    sels, vals = [], []
    for _ in range(TOP_K):
        m = jnp.max(work, axis=-1, keepdims=True)
        sel = jnp.min(jnp.where(work == m, lane_f, float(LANES)), axis=-1, keepdims=True)
        hit = lane_f == sel
        sels.append(hit)
        vals.append(m)
        work = jnp.where(hit, -jnp.inf, work)
    exps = [jnp.exp(v - vals[0]) for v in vals]
    denom = exps[0]
    for e in exps[1:]:
        denom = denom + e

    onehot = sels[0]
    for hit in sels[1:]:
        onehot = onehot | hit
    cnt = onehot.astype(F32)
    rr = lax.broadcasted_iota(jnp.int32, (tm, tm), 0)
    cc = lax.broadcasted_iota(jnp.int32, (tm, tm), 1)
    tril = (rr > cc).astype(BF16)
    before = jnp.dot(tril, cnt.astype(BF16), preferred_element_type=F32) + carry_ref[...]

    ri = jnp.zeros(logits.shape, F32)
    rg = jnp.zeros(logits.shape, F32)
    for k in range(TOP_K):
        e_k = jnp.sum(jnp.where(sels[k], lane_f, 0.0), axis=-1, keepdims=True)
        rank_k = jnp.sum(jnp.where(sels[k], before, 0.0), axis=-1, keepdims=True)
        ri = jnp.where(lane == k, e_k, ri)
        ri = jnp.where(lane == TOP_K + k, rank_k, ri)
        rg = jnp.where(lane == k, exps[k] / denom, rg)
    ri_ref[...] = ri.astype(jnp.int32)
    rg_ref[...] = rg
    carry_ref[...] = carry_ref[...] + jnp.sum(cnt, axis=0, keepdims=True)
    cnt_ref[...] = carry_ref[...].astype(jnp.int32)


def _outproj(x, out_a, out_b, gb, wo, gf, rw, rb, tm):
    t, d = x.shape
    da = out_a.shape[1]
    db = out_b.shape[1]
    row = lambda i: (i, 0)
    fixed = lambda i: (0, 0)
    return pl.pallas_call(
        _outproj_kernel,
        grid=(t // tm,),
        in_specs=[
            pl.BlockSpec((tm, d), row),
            pl.BlockSpec((tm, da), row),
            pl.BlockSpec((tm, db), row),
            pl.BlockSpec((1, db), fixed),
            pl.BlockSpec(wo.shape, fixed, pipeline_mode=pl.Buffered(1)),
            pl.BlockSpec((1, d), fixed),
            pl.BlockSpec(rw.shape, fixed),
            pl.BlockSpec((1, LANES), fixed),
        ],
        out_specs=[
            pl.BlockSpec((tm, d), row),
            pl.BlockSpec((tm, d // 2), row),
            pl.BlockSpec((tm, LANES), row),
            pl.BlockSpec((tm, LANES), row),
            pl.BlockSpec((1, LANES), fixed),
        ],
        out_shape=[
            jax.ShapeDtypeStruct((t, d), F32),
            jax.ShapeDtypeStruct((t, d // 2), U32),
            jax.ShapeDtypeStruct((t, LANES), jnp.int32),
            jax.ShapeDtypeStruct((t, LANES), F32),
            jax.ShapeDtypeStruct((1, LANES), jnp.int32),
        ],
        scratch_shapes=[pltpu.VMEM((1, LANES), F32)],
        compiler_params=_params(("arbitrary",)),
        name="outproj_router",
    )(x, out_a, out_b, gb, wo, gf, rw, rb)


def _dispatch_kernel(zrow_ref, zcnt_ref, dest_ref, hp_ref, xs_ref, zero_ref, sem, zsem, *, sub):
    step = pl.program_id(0)
    tm = hp_ref.shape[0]
    n_exp = zrow_ref.shape[0]

    def zero_copy(e):
        return pltpu.make_async_copy(zero_ref, xs_ref.at[pl.ds(pl.multiple_of(zrow_ref[e], sub), sub), :], zsem)

    @pl.when(step == 0)
    def _():
        zero_ref[...] = jnp.zeros_like(zero_ref)
        for e in range(n_exp):
            @pl.when(zcnt_ref[e] > 0)
            def _():
                zero_copy(e).start()
        for e in range(n_exp):
            @pl.when(zcnt_ref[e] > 0)
            def _():
                zero_copy(e).wait()

    def row_copy(r, k):
        d = dest_ref[0, 0, r * TOP_K + k]
        return pltpu.make_async_copy(hp_ref.at[pl.ds(r, 1), :], xs_ref.at[pl.ds(d, 1), :], sem)

    def issue(r, _):
        for k in range(TOP_K):
            row_copy(r, k).start()
        return 0

    def drain(r, _):
        for k in range(TOP_K):
            row_copy(r, k).wait()
        return 0

    lax.fori_loop(0, tm, issue, 0)
    lax.fori_loop(0, tm, drain, 0)


def _dispatch(zrow, zcnt, dest, hp, n_slots, sub, tm):
    t, half = hp.shape
    dest3 = dest.reshape(t // tm, 1, tm * TOP_K)
    grid_spec = pltpu.PrefetchScalarGridSpec(
        num_scalar_prefetch=2,
        grid=(t // tm,),
        in_specs=[
            pl.BlockSpec((1, 1, tm * TOP_K), lambda i, *_: (i, 0, 0), memory_space=pltpu.SMEM),
            pl.BlockSpec((tm, half), lambda i, *_: (i, 0)),
        ],
        out_specs=pl.BlockSpec(memory_space=pl.ANY),
        scratch_shapes=[pltpu.VMEM((sub, half), U32), pltpu.SemaphoreType.DMA, pltpu.SemaphoreType.DMA],
    )
    return pl.pallas_call(
        functools.partial(_dispatch_kernel, sub=sub),
        grid_spec=grid_spec,
        out_shape=jax.ShapeDtypeStruct((n_slots, half), U32),
        compiler_params=_params(("arbitrary",)),
        name="dispatch",
    )(zrow, zcnt, dest3, hp)


def _gmm_kernel(plan_ref, x_ref, bias_ref, w_hbm, o_ref, wbuf, stage, sem, *rest, mode, kc, sub, nw):
    blk = pl.program_id(0)
    n_blk = plan_ref.shape[1]
    expert = plan_ref[0, blk]
    half_idx = plan_ref[1, blk]
    nxt = plan_ref[3, blk]
    valid = blk < plan_ref[7, 0]
    n_sub = jnp.where(valid, plan_ref[6, blk], 0)
    n_ch = w_hbm.shape[1] // kc
    del n_blk

    def chunk_rows(k):
        return pl.ds(pl.multiple_of(k * kc, kc), kc)

    def chunk_copy(e, k, st):
        return pltpu.make_async_copy(w_hbm.at[e, chunk_rows(k), :], stage.at[st], sem.at[st])

    def land(e, k, dst):
        st = k % 2
        chunk_copy(e, k, st).wait()
        wbuf[dst, chunk_rows(k), :] = stage[st].astype(BF16)

        @pl.when(k + 2 < n_ch)
        def _():
            chunk_copy(e, k + 2, st).start()

    @pl.when(blk == 0)
    def _():
        chunk_copy(expert, 0, 0).start()
        chunk_copy(expert, 1, 1).start()
        lax.fori_loop(0, n_ch, lambda k, c: (land(expert, k, half_idx), c)[1], 0)

    @pl.when(valid & (plan_ref[2, blk] == 1) & (nxt >= 0))
    def _():
        chunk_copy(nxt, 0, 0).start()
        chunk_copy(nxt, 1, 1).start()

    if mode == "gate_up":
        xb_ref, = rest
        half = x_ref.shape[1]
        de = o_ref.shape[1]

        def run(rows):
            v = x_ref[rows, :]
            xb_ref[rows, :half] = _unpack_lo(v).astype(BF16)
            xb_ref[rows, half:] = _unpack_hi(v).astype(BF16)
            x = xb_ref[rows, :]
            for n in range(de // nw):
                cg = slice(n * nw, (n + 1) * nw)
                cu = slice(de + n * nw, de + (n + 1) * nw)
                gate = jnp.dot(x, wbuf[half_idx, :, cg], preferred_element_type=F32) + bias_ref[:, cg]
                up = jnp.dot(x, wbuf[half_idx, :, cu], preferred_element_type=F32) + bias_ref[:, cu]
                gate = jnp.minimum(gate, SWIGLU_LIMIT)
                up = jnp.clip(up, -SWIGLU_LIMIT, SWIGLU_LIMIT)
                act = (up + 1.0) * (gate * jax.nn.sigmoid(SWIGLU_ALPHA * gate))
                o_ref[rows, cg] = act.astype(o_ref.dtype)
    else:
        half = o_ref.shape[1]

        def run(rows):
            x = x_ref[rows, :]
            for n in range(half // nw):
                cl = slice(n * nw, (n + 1) * nw)
                ch = slice(half + n * nw, half + (n + 1) * nw)
                lo = jnp.dot(x, wbuf[half_idx, :, cl], preferred_element_type=F32) + bias_ref[:, cl]
                hi = jnp.dot(x, wbuf[half_idx, :, ch], preferred_element_type=F32) + bias_ref[:, ch]
                o_ref[rows, cl] = _pack_bf16_pair(lo, hi)

    full = x_ref.shape[0] // sub

    @pl.when(n_sub == full)
    def _():
        run(slice(None))

    @pl.when((n_sub > 0) & (n_sub < full))
    def _():
        def body(i, c):
            run(pl.ds(pl.multiple_of(i * sub, sub), sub))
            return c
        lax.fori_loop(0, n_sub, body, 0)

    lax.fori_loop(plan_ref[4, blk], plan_ref[5, blk], lambda k, c: (land(nxt, k, 1 - half_idx), c)[1], 0)


def _gmm(plan, x, bias, w, mode, bm, sub, kc):
    n_slots = x.shape[0]
    n_exp, kdim, ndim = w.shape
    n_blk = n_slots // bm
    if mode == "gate_up":
        out_cols, out_dtype = ndim // 2, BF16
        extra = [pltpu.VMEM((bm, kdim), BF16)]
    else:
        out_cols, out_dtype = ndim // 2, U32
        extra = []

    def live(b, plan):
        return (jnp.minimum(b, plan[7, 0] - 1), 0)

    grid_spec = pltpu.PrefetchScalarGridSpec(
        num_scalar_prefetch=1,
        grid=(n_blk,),
        in_specs=[
            pl.BlockSpec((bm, x.shape[1]), live),
            pl.BlockSpec((None, 1, ndim), lambda b, plan: (plan[0, b], 0, 0)),
            pl.BlockSpec(memory_space=pl.ANY),
        ],
        out_specs=pl.BlockSpec((bm, out_cols), live),
        scratch_shapes=[
            pltpu.VMEM((2, kdim, ndim), BF16),
            pltpu.VMEM((2, kc, ndim), F32),
            pltpu.SemaphoreType.DMA((2,)),
        ] + extra,
    )
    return pl.pallas_call(
        functools.partial(_gmm_kernel, mode=mode, kc=kc, sub=sub, nw=min(512, out_cols)),
        grid_spec=grid_spec,
        out_shape=jax.ShapeDtypeStruct((n_slots, out_cols), out_dtype),
        compiler_params=_params(("arbitrary",)),
        name="experts_" + mode,
    )(plan, x, bias, w)


def _expert_plan(counts, bm, sub, n_blk, n_ch):
    n_exp = counts.shape[0]
    nb_e = (counts + bm - 1) // bm
    blk_end = jnp.cumsum(nb_e)
    blk_start = blk_end - nb_e
    n_valid = blk_end[-1]
    blk = jnp.arange(n_blk, dtype=jnp.int32)
    last = jnp.minimum(blk, n_valid - 1)
    be = jnp.minimum(jnp.searchsorted(blk_end, last, side="right"), n_exp - 1).astype(jnp.int32)
    valid = blk < n_valid
    i_in = last - jnp.take(blk_start, be)
    nb_b = jnp.take(nb_e, be)
    nonempty = nb_e > 0
    ordinal = jnp.cumsum(nonempty.astype(jnp.int32)) - 1
    idx = jnp.where(nonempty, jnp.arange(n_exp, dtype=jnp.int32), n_exp)
    nxt_incl = lax.cummin(idx, axis=0, reverse=True)
    nxt_e = jnp.concatenate([nxt_incl[1:], jnp.full((1,), n_exp, jnp.int32)])
    nxt_e = jnp.where(nxt_e >= n_exp, -1, nxt_e)
    nxt_b = jnp.take(nxt_e, be)
    has_next = valid & (nxt_b >= 0)
    lo = jnp.where(has_next, i_in * n_ch // nb_b, 0)
    hi = jnp.where(has_next, (i_in + 1) * n_ch // nb_b, 0)
    rows = jnp.clip(jnp.take(counts, be) - i_in * bm, 0, bm)
    nsub = jnp.where(valid, (rows + sub - 1) // sub, 0)
    plan = jnp.stack([
        be, jnp.take(ordinal, be) % 2, (valid & (i_in == 0)).astype(jnp.int32), nxt_b, lo, hi, nsub,
        jnp.full((n_blk,), n_valid, jnp.int32),
    ]).astype(jnp.int32)
    return blk_start * bm, plan


def _combine_kernel(dest_ref, x1_ref, rg_ref, gfin_ref, os_ref, y_ref, gbuf_ref, sem, *, final_norm):
    tm = x1_ref.shape[0]
    half = gbuf_ref.shape[2]

    def row_copy(r, k):
        d = dest_ref[0, 0, r * TOP_K + k]
        return pltpu.make_async_copy(os_ref.at[pl.ds(d, 1), :], gbuf_ref.at[k, pl.ds(r, 1), :], sem)

    def issue(r, _):
        for k in range(TOP_K):
            row_copy(r, k).start()
        return 0

    def drain(r, _):
        for k in range(TOP_K):
            row_copy(r, k).wait()
        return 0

    lax.fori_loop(0, tm, issue, 0)
    lax.fori_loop(0, tm, drain, 0)

    x1 = x1_ref[...]
    rg = rg_ref[...]
    lo = x1[:, :half]
    hi = x1[:, half:]
    for k in range(TOP_K):
        g = rg[:, k:k + 1]
        v = gbuf_ref[k]
        lo = lo + g * _unpack_lo(v)
        hi = hi + g * _unpack_hi(v)
    if final_norm:
        ms = (jnp.sum(lo * lo, axis=-1, keepdims=True) + jnp.sum(hi * hi, axis=-1, keepdims=True)) / (2 * half)
        scale = lax.rsqrt(ms + EPS)
        gf = gfin_ref[...]
        lo = lo * scale * gf[:, :half]
        hi = hi * scale * gf[:, half:]
    y_ref[:, :half] = lo
    y_ref[:, half:] = hi


def _combine(dest, x1, rg, gfin, os_, tm, final_norm):
    t, d = x1.shape
    half = d // 2
    dest3 = dest.reshape(t // tm, 1, tm * TOP_K)
    return pl.pallas_call(
        functools.partial(_combine_kernel, final_norm=final_norm),
        grid=(t // tm,),
        in_specs=[
            pl.BlockSpec((1, 1, tm * TOP_K), lambda i: (i, 0, 0), memory_space=pltpu.SMEM),
            pl.BlockSpec((tm, d), lambda i: (i, 0)),
            pl.BlockSpec((tm, LANES), lambda i: (i, 0)),
            pl.BlockSpec((1, d), lambda i: (0, 0)),
            pl.BlockSpec(memory_space=pl.ANY),
        ],
        out_specs=pl.BlockSpec((tm, d), lambda i: (i, 0)),
        out_shape=jax.ShapeDtypeStruct((t, d), F32),
        scratch_shapes=[pltpu.VMEM((TOP_K, tm, half), U32), pltpu.SemaphoreType.DMA],
        compiler_params=_params(("arbitrary",)),
        name="combine",
    )(dest3, x1, rg, gfin, os_)


def _tile(n, pref):
    while n % pref:
        pref //= 2
    return pref


def _layer(x, seqs, p, final_g):
    t, d = x.shape
    dg = p["gmlp_ln_g"].shape[-1]
    chunk = p["gmlp_ws"].shape[-1]
    d_lru = p["conv_w"].shape[-1]
    heads_l, hd_l = p["lru_wr"].shape[1], p["lru_wr"].shape[2]
    n_exp = p["router_w"].shape[-1]
    de = p["w_down"].shape[1]

    tm = _tile(t, 512)
    z = _inproj(x, p["mix_norm_g"][None], p["w_in"].astype(BF16), tm)

    bs_b = jnp.broadcast_to(p["gmlp_bs"][:, :, None], p["gmlp_ws"].shape).astype(F32)
    out_a = _gmlp(z, p["gmlp_ln_g"][None], p["gmlp_ln_b"][None], p["gmlp_ws"].astype(BF16), bs_b,
                  p["out_norm_a"][None], max(tm, chunk))

    wri = jnp.concatenate([p["lru_wr"], p["lru_wi"]], axis=-1).astype(BF16)
    bri = jnp.concatenate([p["lru_br"].reshape(2, heads_l, 1, hd_l),
                           p["lru_bi"].reshape(2, heads_l, 1, hd_l)], axis=-1)
    lam = p["lru_lam"].reshape(2, heads_l, 1, hd_l)
    out_b = None
    for row0, n_seq, seq in seqs:
        out_b = _rglru(z, out_b, row0, n_seq, seq, 2 * dg // hd_l, (2 * dg + d_lru) // hd_l,
                       p["conv_w"], p["conv_b"][None], wri, bri, lam, _tile(seq, 512))

    rw = jnp.zeros((d, LANES), BF16).at[:, :n_exp].set(p["router_w"].astype(BF16))
    rb = jnp.full((1, LANES), NEG_BIG, F32).at[0, :n_exp].set(p["router_b"])
    x1, hp, ri, rg, cnt = _outproj(x, out_a, out_b, p["out_norm_b"][None], p["w_out"].astype(BF16),
                                   p["ffn_norm_g"][None], rw, rb, tm)

    m = t * TOP_K
    bm = _tile(m, ROW_BLOCK)
    sub = min(SUB_BLOCK, bm)
    kc = _tile(d, WEIGHT_CHUNK_ROWS)
    counts = cnt[0, :n_exp]
    n_blk = m // bm + n_exp
    n_slots = n_blk * bm
    pad_start, plan = _expert_plan(counts, bm, sub, n_blk, d // kc)
    dest = (jnp.take(pad_start, ri[:, :TOP_K]) + ri[:, TOP_K:2 * TOP_K]).reshape(m)
    zrow = (pad_start + jnp.maximum((counts + sub - 1) // sub - 1, 0) * sub).astype(jnp.int32)

    xs = _dispatch(zrow, counts, dest, hp, n_slots, sub, _tile(t, 256))
    act = _gmm(plan, xs, p["b_gate_up"][:, None, :], p["w_gate_up"], "gate_up", bm, sub, kc)
    os_ = _gmm(plan, act, p["b_down"][:, None, :], p["w_down"], "down", bm, sub, kc)
    gfin = jnp.ones((1, d), F32) if final_g is None else final_g[None]
    return _combine(dest, x1, rg, gfin, os_, _tile(t, 256), final_g is not None)


_LAYER_KEYS = ("mix_norm_g", "w_in", "gmlp_ln_g", "gmlp_ln_b", "gmlp_ws", "gmlp_bs", "conv_w", "conv_b",
               "lru_wr", "lru_br", "lru_wi", "lru_bi", "lru_lam", "out_norm_a", "out_norm_b", "w_out",
               "ffn_norm_g", "router_w", "router_b", "w_gate_up", "b_gate_up", "w_down", "b_down")


def kernel(x_prompt, x_sample, mix_norm_g, w_in, gmlp_ln_g, gmlp_ln_b, gmlp_ws, gmlp_bs, conv_w, conv_b, lru_wr, lru_br, lru_wi, lru_bi, lru_lam, out_norm_a, out_norm_b, w_out, ffn_norm_g, router_w, router_b, w_gate_up, b_gate_up, w_down, b_down, final_norm_g):
    stacked = dict(zip(_LAYER_KEYS, (mix_norm_g, w_in, gmlp_ln_g, gmlp_ln_b, gmlp_ws, gmlp_bs, conv_w, conv_b,
                                     lru_wr, lru_br, lru_wi, lru_bi, lru_lam, out_norm_a, out_norm_b, w_out,
                                     ffn_norm_g, router_w, router_b, w_gate_up, b_gate_up, w_down, b_down)))
    depth = w_in.shape[0]
    bp, sp, d = x_prompt.shape
    bs, ss, _ = x_sample.shape
    x = jnp.concatenate([x_prompt.reshape(bp * sp, d), x_sample.reshape(bs * ss, d)], axis=0)
    seqs = [(0, bp, sp), (bp * sp, bs, ss)]
    for layer in range(depth):
        p = {k: v[layer] for k, v in stacked.items()}
        x = _layer(x, seqs, p, final_norm_g if layer == depth - 1 else None)
    return x[:bp * sp].reshape(bp, sp, d), x[bp * sp:].reshape(bs, ss, d)
```

```python
import functools
import math

import jax
import jax.numpy as jnp
from jax import lax
from jax.experimental import pallas as pl
from jax.experimental.pallas import tpu as pltpu

TOP_K = 4
LRU_C = 8.0
SWIGLU_LIMIT = 7.0
SWIGLU_ALPHA = 1.702
EPS = 1e-6

LANES = 128
VMEM_LIMIT = 56 * 1024 * 1024
NEG_BIG = -1e30
ROW_BLOCK = 512
SUB_BLOCK = 128
WEIGHT_CHUNK_ROWS = 128
DMA_GROUP = 8

F32 = jnp.float32
BF16 = jnp.bfloat16
U32 = jnp.uint32


def _params(sem):
    return pltpu.CompilerParams(dimension_semantics=sem, vmem_limit_bytes=VMEM_LIMIT)


def _gelu(x):
    return 0.5 * x * (1.0 + jnp.tanh(0.7978845608028654 * (x + 0.044715 * (x * x * x))))


def _rms(x, g):
    return x * lax.rsqrt(jnp.mean(x * x, axis=-1, keepdims=True) + EPS) * g


def _pack_bf16_pair(lo, hi):
    lo_bits = lax.bitcast_convert_type(lo.astype(BF16).astype(F32), U32)
    hi_bits = lax.bitcast_convert_type(hi.astype(BF16).astype(F32), U32)
    return (lo_bits >> 16) | (hi_bits & jnp.uint32(0xFFFF0000))


def _unpack_lo(v):
    return lax.bitcast_convert_type(v << 16, F32)


def _unpack_hi(v):
    return lax.bitcast_convert_type(v & jnp.uint32(0xFFFF0000), F32)


def _pair_specs(tm, d, n_first):
    return [pl.BlockSpec((tm, d), lambda i: (jnp.minimum(i, n_first - 1), 0)),
            pl.BlockSpec((tm, d), lambda i: (jnp.maximum(i - n_first, 0), 0))]


def _pair_tile(xa_ref, xb_ref, n_first):
    return jnp.where(pl.program_id(0) < n_first, xa_ref[...], xb_ref[...])


def _inproj_kernel(xa_ref, xb_ref, g_ref, w_ref, z_ref, *, n_chunk, n_first):
    xn = _rms(_pair_tile(xa_ref, xb_ref, n_first), g_ref[...]).astype(BF16)
    for n in range(w_ref.shape[1] // n_chunk):
        cols = slice(n * n_chunk, (n + 1) * n_chunk)
        z_ref[:, cols] = jnp.dot(xn, w_ref[:, cols], preferred_element_type=F32).astype(z_ref.dtype)


def _inproj(xa, xb, g, w, tm):
    d = xa.shape[1]
    t = xa.shape[0] + xb.shape[0]
    d_in = w.shape[1]
    n_first = xa.shape[0] // tm
    return pl.pallas_call(
        functools.partial(_inproj_kernel, n_chunk=min(1024, d_in), n_first=n_first),
        grid=(t // tm,),
        in_specs=_pair_specs(tm, d, n_first) + [
            pl.BlockSpec((1, d), lambda i: (0, 0)),
            pl.BlockSpec((d, d_in), lambda i: (0, 0), pipeline_mode=pl.Buffered(1)),
        ],
        out_specs=pl.BlockSpec((tm, d_in), lambda i: (i, 0)),
        out_shape=jax.ShapeDtypeStruct((t, d_in), BF16),
        compiler_params=_params(("arbitrary",)),
        name="inproj",
    )(xa, xb, g, w)


def _gmlp_kernel(u_ref, v_ref, lng_ref, lnb_ref, ws_ref, bs_ref, ga_ref, o_ref, mix_ref, *, heads, chunk):
    v = _gelu(v_ref[...].astype(F32))
    mu = jnp.mean(v, axis=-1, keepdims=True)
    vc = v - mu
    var = jnp.mean(vc * vc, axis=-1, keepdims=True)
    vn = (vc * lax.rsqrt(var + EPS) * lng_ref[...] + lnb_ref[...]).astype(BF16)
    tm, dg = vn.shape
    hd = dg // heads
    for c in range(tm // chunk):
        rows = slice(c * chunk, (c + 1) * chunk)
        for h in range(heads):
            cols = slice(h * hd, (h + 1) * hd)
            mixed = jnp.dot(ws_ref[h], vn[rows, cols], preferred_element_type=F32) + bs_ref[h]
            mix_ref[rows, cols] = mixed
    out = _gelu(u_ref[...].astype(F32)) * mix_ref[...]
    o_ref[...] = _rms(out, ga_ref[...]).astype(o_ref.dtype)


def _gmlp(z, ln_g, ln_b, ws, bs_b, ga, tm):
    t = z.shape[0]
    heads, chunk, _ = ws.shape
    dg = ln_g.shape[1]
    vec = pl.BlockSpec((1, dg), lambda i: (0, 0))
    return pl.pallas_call(
        functools.partial(_gmlp_kernel, heads=heads, chunk=chunk),
        grid=(t // tm,),
        in_specs=[
            pl.BlockSpec((tm, dg), lambda i: (i, 0)),
            pl.BlockSpec((tm, dg), lambda i: (i, 1)),
            vec, vec,
            pl.BlockSpec(ws.shape, lambda i: (0, 0, 0)),
            pl.BlockSpec(bs_b.shape, lambda i: (0, 0, 0)),
            vec,
        ],
        out_specs=pl.BlockSpec((tm, dg), lambda i: (i, 0)),
        out_shape=jax.ShapeDtypeStruct((t, dg), BF16),
        scratch_shapes=[pltpu.VMEM((tm, dg), F32)],
        compiler_params=_params(("arbitrary",)),
        name="gmlp",
    )(z, z, ln_g, ln_b, ws, bs_b, ga)


def _shift_rows(x, s, fill, reverse):
    n = x.shape[0]
    row = lax.broadcasted_iota(jnp.int32, x.shape, 0) % 8
    if reverse:
        return jnp.where(row >= 8 - s, fill, pltpu.roll(x, n - s, 0))
    return jnp.where(row < s, fill, pltpu.roll(x, s, 0))


def _lru_scan_chunk(a, b, h, reverse):
    for s in (1, 2, 4):
        b = a * _shift_rows(b, s, 0.0, reverse) + b
        a = a * _shift_rows(a, s, 1.0, reverse)
    n_grp = a.shape[0] // 8
    out = [None] * n_grp
    order = range(n_grp - 1, -1, -1) if reverse else range(n_grp)
    edge = 0 if reverse else 7
    for gi in order:
        rows = slice(gi * 8, gi * 8 + 8)
        hg = b[rows] + a[rows] * h
        out[gi] = hg
        h = hg[edge:edge + 1]
    return jnp.concatenate(out, axis=0), h


def _rglru_kernel(seqlen_ref, xr_ref, gr_ref, cw_ref, cb_ref, wri_ref, bri_ref, lam_ref, o_ref, hf_ref, *, tc):
    rows, hd = xr_ref.shape
    n_ck = rows // tc
    halo = 16
    cw = cw_ref[...]
    cb = cb_ref[...]
    seq = seqlen_ref[pl.program_id(0)]

    def starts_seq(c):
        return lax.rem(c * tc, seq) == 0

    def ends_seq(c):
        return lax.rem((c + 1) * tc, seq) == 0

    def conv_chunk(c):
        t0 = pl.multiple_of(c * tc, tc)
        main = xr_ref[pl.ds(t0, tc), :].astype(F32)
        p0 = pl.multiple_of(jnp.maximum(t0 - halo, 0), halo)
        n0 = pl.multiple_of(jnp.minimum(t0 + tc, rows - halo), halo)
        prev = jnp.where(starts_seq(c), 0.0, xr_ref[pl.ds(p0, halo), :].astype(F32))
        nxt = jnp.where(ends_seq(c), 0.0, xr_ref[pl.ds(n0, halo), :].astype(F32))
        ext = jnp.concatenate([prev, main, nxt], axis=0)
        n = ext.shape[0]
        acc = cb + cw[2:3] * main
        acc = acc + cw[0:1] * pltpu.roll(ext, 2, 0)[halo:halo + tc]
        acc = acc + cw[1:2] * pltpu.roll(ext, 1, 0)[halo:halo + tc]
        acc = acc + cw[3:4] * pltpu.roll(ext, n - 1, 0)[halo:halo + tc]
        return t0, acc

    def direction(d, c, h):
        t0, xc = conv_chunk(c)
        g = jnp.dot(xc.astype(BF16), wri_ref[d], preferred_element_type=F32) + bri_ref[d]
        r = jax.nn.sigmoid(g[:, :hd])
        i = jax.nn.sigmoid(g[:, hd:])
        nl = -lam_ref[d]
        softplus = jnp.maximum(nl, 0.0) + jnp.log1p(jnp.exp(-jnp.abs(nl)))
        log_a = (-LRU_C * softplus) * r
        a = jnp.exp(log_a)
        b = jnp.sqrt(-jnp.tanh(log_a) * (a * a + 1.0)) * i * xc
        hs, h = _lru_scan_chunk(a, b, h, reverse=(d == 1))
        return t0, hs, h

    def fwd(c, h):
        t0, hs, h = direction(0, c, jnp.where(starts_seq(c), 0.0, h))
        hf_ref[pl.ds(t0, tc), :] = hs
        return h

    def bwd(k, h):
        c = n_ck - 1 - k
        t0, hs, h = direction(1, c, jnp.where(ends_seq(c), 0.0, h))
        tot = hf_ref[pl.ds(t0, tc), :] + hs
        o_ref[pl.ds(t0, tc), :] = (_gelu(gr_ref[pl.ds(t0, tc), :].astype(F32)) * tot).astype(o_ref.dtype)
        return h

    h0 = jnp.zeros((1, hd), F32)
    lax.fori_loop(0, n_ck, fwd, h0)
    lax.fori_loop(0, n_ck, bwd, h0)


def _rglru(z, unit_seq, unit, col_x, col_g, cw, cb, wri, bri, lam, tc):
    t = z.shape[0]
    _, heads, hd, _ = wri.shape
    grid_spec = pltpu.PrefetchScalarGridSpec(
        num_scalar_prefetch=1,
        grid=(t // unit, heads),
        in_specs=[
            pl.BlockSpec((unit, hd), lambda u, h, sl: (u, col_x + h)),
            pl.BlockSpec((unit, hd), lambda u, h, sl: (u, col_g + h)),
            pl.BlockSpec((cw.shape[0], hd), lambda u, h, sl: (0, h)),
            pl.BlockSpec((1, hd), lambda u, h, sl: (0, h)),
            pl.BlockSpec((2, None, hd, 2 * hd), lambda u, h, sl: (0, h, 0, 0)),
            pl.BlockSpec((2, None, 1, 2 * hd), lambda u, h, sl: (0, h, 0, 0)),
            pl.BlockSpec((2, None, 1, hd), lambda u, h, sl: (0, h, 0, 0)),
        ],
        out_specs=pl.BlockSpec((unit, hd), lambda u, h, sl: (u, h)),
        scratch_shapes=[pltpu.VMEM((unit, hd), F32)],
    )
    return pl.pallas_call(
        functools.partial(_rglru_kernel, tc=tc),
        grid_spec=grid_spec,
        out_shape=jax.ShapeDtypeStruct((t, heads * hd), BF16),
        compiler_params=_params(("arbitrary", "arbitrary")),
        name="rglru",
    )(unit_seq, z, z, cw, cb, wri, bri, lam)


def _outproj_kernel(xa_ref, xb_ref, a_ref, b_ref, gb_ref, wo_ref, gf_ref, rw_ref, rb_ref,
                    x1_ref, hp_ref, ri_ref, rg_ref, cnt_ref, carry_ref, *, n_first):
    step = pl.program_id(0)

    @pl.when(step == 0)
    def _():
        carry_ref[...] = jnp.zeros_like(carry_ref)

    da = a_ref.shape[1]
    obn = _rms(b_ref[...].astype(F32), gb_ref[...]).astype(BF16)
    y = jnp.dot(a_ref[...], wo_ref[:da], preferred_element_type=F32)
    y = y + jnp.dot(obn, wo_ref[da:], preferred_element_type=F32)
    x1 = _pair_tile(xa_ref, xb_ref, n_first) + y
    x1_ref[...] = x1
    hf = _rms(x1, gf_ref[...])
    half = hf.shape[1] // 2
    hp_ref[...] = _pack_bf16_pair(hf[:, :half], hf[:, half:])

    logits = jnp.dot(hf.astype(BF16), rw_ref[...], preferred_element_type=F32) + rb_ref[...]
    tm = logits.shape[0]
    lane = lax.broadcasted_iota(jnp.int32, logits.shape, 1)
    lane_f = lane.astype(F32)
    work = logits
    sels, vals = [], []
    for _ in range(TOP_K):
        m = jnp.max(work, axis=-1, keepdims=True)
        sel = jnp.min(jnp.where(work == m, lane_f, float(LANES)), axis=-1, keepdims=True)
        hit = lane_f == sel
        sels.append(hit)
        vals.append(m)
        work = jnp.where(hit, -jnp.inf, work)
    exps = [jnp.exp(v - vals[0]) for v in vals]
    denom = exps[0]
    for e in exps[1:]:
        denom = denom + e

    onehot = sels[0]
    for hit in sels[1:]:
        onehot = onehot | hit
    cnt = onehot.astype(F32)
    rr = lax.broadcasted_iota(jnp.int32, (tm, tm), 0)
    cc = lax.broadcasted_iota(jnp.int32, (tm, tm), 1)
    tril = (rr > cc).astype(BF16)
    before = jnp.dot(tril, cnt.astype(BF16), preferred_element_type=F32) + carry_ref[...]

    ri = jnp.zeros(logits.shape, F32)
    rg = jnp.zeros(logits.shape, F32)
    for k in range(TOP_K):
        e_k = jnp.sum(jnp.where(sels[k], lane_f, 0.0), axis=-1, keepdims=True)
        rank_k = jnp.sum(jnp.where(sels[k], before, 0.0), axis=-1, keepdims=True)
        ri = jnp.where(lane == k, e_k, ri)
        ri = jnp.where(lane == TOP_K + k, rank_k, ri)
        rg = jnp.where(lane == k, exps[k] / denom, rg)
    ri_ref[...] = ri.astype(jnp.int32)
    rg_ref[...] = rg
    carry_ref[...] = carry_ref[...] + jnp.sum(cnt, axis=0, keepdims=True)
    cnt_ref[...] = carry_ref[...].astype(jnp.int32)


def _outproj(xa, xb, out_a, out_b, gb, wo, gf, rw, rb, tm):
    d = xa.shape[1]
    t = xa.shape[0] + xb.shape[0]
    n_first = xa.shape[0] // tm
    da = out_a.shape[1]
    db = out_b.shape[1]
    row = lambda i: (i, 0)
    fixed = lambda i: (0, 0)
    return pl.pallas_call(
        functools.partial(_outproj_kernel, n_first=n_first),
        grid=(t // tm,),
        in_specs=_pair_specs(tm, d, n_first) + [
            pl.BlockSpec((tm, da), row),
            pl.BlockSpec((tm, db), row),
            pl.BlockSpec((1, db), fixed),
            pl.BlockSpec(wo.shape, fixed, pipeline_mode=pl.Buffered(1)),
            pl.BlockSpec((1, d), fixed),
            pl.BlockSpec(rw.shape, fixed),
            pl.BlockSpec((1, LANES), fixed),
        ],
        out_specs=[
            pl.BlockSpec((tm, d), row),
            pl.BlockSpec((tm, d // 2), row),
            pl.BlockSpec((tm, LANES), row),
            pl.BlockSpec((tm, LANES), row),
            pl.BlockSpec((1, LANES), fixed),
        ],
        out_shape=[
            jax.ShapeDtypeStruct((t, d), F32),
            jax.ShapeDtypeStruct((t, d // 2), U32),
            jax.ShapeDtypeStruct((t, LANES), jnp.int32),
            jax.ShapeDtypeStruct((t, LANES), F32),
            jax.ShapeDtypeStruct((1, LANES), jnp.int32),
        ],
        scratch_shapes=[pltpu.VMEM((1, LANES), F32)],
        compiler_params=_params(("arbitrary",)),
        name="outproj_router",
    )(xa, xb, out_a, out_b, gb, wo, gf, rw, rb)


def _dispatch_kernel(zneed_ref, dest_ref, hp_ref, xs_ref, zero_ref, sem, zsem, *, sub, group):
    step = pl.program_id(0)
    tm = hp_ref.shape[0]
    n_zc = zneed_ref.shape[0]

    def zero_copy(j):
        return pltpu.make_async_copy(zero_ref, xs_ref.at[pl.ds(pl.multiple_of(j * sub, sub), sub), :], zsem)

    @pl.when(step == 0)
    def _():
        zero_ref[...] = jnp.zeros_like(zero_ref)

        def start(j, c):
            @pl.when(zneed_ref[j] != 0)
            def _():
                zero_copy(j).start()
            return c

        def wait(j, c):
            @pl.when(zneed_ref[j] != 0)
            def _():
                zero_copy(j).wait()
            return c

        lax.fori_loop(0, n_zc, start, 0)
        lax.fori_loop(0, n_zc, wait, 0)

    def issue(g, c):
        base = g * group
        for r in range(group):
            for k in range(TOP_K):
                d = dest_ref[0, 0, (base + r) * TOP_K + k]
                pltpu.make_async_copy(hp_ref.at[pl.ds(base + r, 1), :], xs_ref.at[pl.ds(d, 1), :],
                                      sem).start(priority=k % 2)
        return c

    lax.fori_loop(0, tm // group, issue, 0)
    for _ in range(TOP_K):
        pltpu.make_async_copy(hp_ref, xs_ref.at[pl.ds(0, tm), :], sem).wait()


def _dispatch(zneed, dest, hp, n_slots, sub, tm):
    t, half = hp.shape
    dest3 = dest.reshape(t // tm, 1, tm * TOP_K)
    grid_spec = pltpu.PrefetchScalarGridSpec(
        num_scalar_prefetch=1,
        grid=(t // tm,),
        in_specs=[
            pl.BlockSpec((1, 1, tm * TOP_K), lambda i, *_: (i, 0, 0), memory_space=pltpu.SMEM),
            pl.BlockSpec((tm, half), lambda i, *_: (i, 0)),
        ],
        out_specs=pl.BlockSpec(memory_space=pl.ANY),
        scratch_shapes=[pltpu.VMEM((sub, half), U32), pltpu.SemaphoreType.DMA, pltpu.SemaphoreType.DMA],
    )
    return pl.pallas_call(
        functools.partial(_dispatch_kernel, sub=sub, group=min(DMA_GROUP, tm)),
        grid_spec=grid_spec,
        out_shape=jax.ShapeDtypeStruct((n_slots, half), U32),
        compiler_params=_params(("arbitrary",)),
        name="dispatch",
    )(zneed, dest3, hp)


def _gmm_kernel(plan_ref, x_ref, bias_ref, w_hbm, o_ref, wbuf, stage, sem, *rest, mode, kc, sub, nw):
    blk = pl.program_id(0)
    n_blk = plan_ref.shape[1]
    expert = plan_ref[0, blk]
    half_idx = plan_ref[1, blk]
    nxt = plan_ref[3, blk]
    valid = blk < plan_ref[7, 0]
    n_sub = jnp.where(valid, plan_ref[6, blk], 0)
    n_ch = w_hbm.shape[1] // kc
    del n_blk

    def chunk_rows(k):
        return pl.ds(pl.multiple_of(k * kc, kc), kc)

    def chunk_copy(e, k, st):
        return pltpu.make_async_copy(w_hbm.at[e, chunk_rows(k), :], stage.at[st], sem.at[st])

    def land(e, k, dst):
        st = k % 2
        chunk_copy(e, k, st).wait()
        wbuf[dst, chunk_rows(k), :] = stage[st].astype(BF16)

        @pl.when(k + 2 < n_ch)
        def _():
            chunk_copy(e, k + 2, st).start()

    @pl.when(blk == 0)
    def _():
        chunk_copy(expert, 0, 0).start()
        chunk_copy(expert, 1, 1).start()
        lax.fori_loop(0, n_ch, lambda k, c: (land(expert, k, half_idx), c)[1], 0)

    @pl.when(valid & (plan_ref[2, blk] == 1) & (nxt >= 0))
    def _():
        chunk_copy(nxt, 0, 0).start()
        chunk_copy(nxt, 1, 1).start()

    if mode == "gate_up":
        xb_ref, = rest
        half = x_ref.shape[1]
        de = o_ref.shape[1]

        def run(rows):
            v = x_ref[rows, :]
            xb_ref[rows, :half] = _unpack_lo(v).astype(BF16)
            xb_ref[rows, half:] = _unpack_hi(v).astype(BF16)
            x = xb_ref[rows, :]
            for n in range(de // nw):
                cg = slice(n * nw, (n + 1) * nw)
                cu = slice(de + n * nw, de + (n + 1) * nw)
                gate = jnp.dot(x, wbuf[half_idx, :, cg], preferred_element_type=F32) + bias_ref[:, cg]
                up = jnp.dot(x, wbuf[half_idx, :, cu], preferred_element_type=F32) + bias_ref[:, cu]
                gate = jnp.minimum(gate, SWIGLU_LIMIT)
                up = jnp.clip(up, -SWIGLU_LIMIT, SWIGLU_LIMIT)
                act = (up + 1.0) * (gate * jax.nn.sigmoid(SWIGLU_ALPHA * gate))
                o_ref[rows, cg] = act.astype(o_ref.dtype)
    else:
        half = o_ref.shape[1]

        def run(rows):
            x = x_ref[rows, :]
            for n in range(half // nw):
                cl = slice(n * nw, (n + 1) * nw)
                ch = slice(half + n * nw, half + (n + 1) * nw)
                lo = jnp.dot(x, wbuf[half_idx, :, cl], preferred_element_type=F32) + bias_ref[:, cl]
                hi = jnp.dot(x, wbuf[half_idx, :, ch], preferred_element_type=F32) + bias_ref[:, ch]
                o_ref[rows, cl] = _pack_bf16_pair(lo, hi)

    full = x_ref.shape[0] // sub

    @pl.when(n_sub < full)
    def _():
        o_ref[...] = jnp.zeros_like(o_ref)

    @pl.when(n_sub == full)
    def _():
        run(slice(None))

    @pl.when((n_sub > 0) & (n_sub < full))
    def _():
        def body(i, c):
            run(pl.ds(pl.multiple_of(i * sub, sub), sub))
            return c
        lax.fori_loop(0, n_sub, body, 0)

    lax.fori_loop(plan_ref[4, blk], plan_ref[5, blk], lambda k, c: (land(nxt, k, 1 - half_idx), c)[1], 0)


def _gmm(plan, x, bias, w, mode, bm, sub, kc):
    n_slots = x.shape[0]
    n_exp, kdim, ndim = w.shape
    n_blk = n_slots // bm
    if mode == "gate_up":
        out_cols, out_dtype = ndim // 2, BF16
        extra = [pltpu.VMEM((bm, kdim), BF16)]
    else:
        out_cols, out_dtype = ndim // 2, U32
        extra = []

    def live(b, plan):
        return (jnp.minimum(b, plan[7, 0] - 1), 0)

    grid_spec = pltpu.PrefetchScalarGridSpec(
        num_scalar_prefetch=1,
        grid=(n_blk,),
        in_specs=[
            pl.BlockSpec((bm, x.shape[1]), live),
            pl.BlockSpec((None, 1, ndim), lambda b, plan: (plan[0, b], 0, 0)),
            pl.BlockSpec(memory_space=pl.ANY),
        ],
        out_specs=pl.BlockSpec((bm, out_cols), lambda b, plan: (b, 0)),
        scratch_shapes=[
            pltpu.VMEM((2, kdim, ndim), BF16),
            pltpu.VMEM((2, kc, ndim), F32),
            pltpu.SemaphoreType.DMA((2,)),
        ] + extra,
    )
    return pl.pallas_call(
        functools.partial(_gmm_kernel, mode=mode, kc=kc, sub=sub, nw=min(512, out_cols)),
        grid_spec=grid_spec,
        out_shape=jax.ShapeDtypeStruct((n_slots, out_cols), out_dtype),
        compiler_params=_params(("arbitrary",)),
        name="experts_" + mode,
    )(plan, x, bias, w)


def _expert_plan(counts, bm, sub, n_blk, n_ch):
    n_exp = counts.shape[0]
    nb_e = (counts + bm - 1) // bm
    blk_end = jnp.cumsum(nb_e)
    blk_start = blk_end - nb_e
    n_valid = blk_end[-1]
    blk = jnp.arange(n_blk, dtype=jnp.int32)
    last = jnp.minimum(blk, n_valid - 1)
    be = jnp.minimum(jnp.sum(blk_end[None, :] <= last[:, None], axis=1), n_exp - 1).astype(jnp.int32)
    valid = blk < n_valid
    i_in = last - jnp.take(blk_start, be)
    nb_b = jnp.take(nb_e, be)
    nonempty = nb_e > 0
    ordinal = jnp.cumsum(nonempty.astype(jnp.int32)) - 1
    idx = jnp.where(nonempty, jnp.arange(n_exp, dtype=jnp.int32), n_exp)
    nxt_incl = lax.cummin(idx, axis=0, reverse=True)
    nxt_e = jnp.concatenate([nxt_incl[1:], jnp.full((1,), n_exp, jnp.int32)])
    nxt_e = jnp.where(nxt_e >= n_exp, -1, nxt_e)
    nxt_b = jnp.take(nxt_e, be)
    has_next = valid & (nxt_b >= 0)
    lo = jnp.where(has_next, i_in * n_ch // nb_b, 0)
    hi = jnp.where(has_next, (i_in + 1) * n_ch // nb_b, 0)
    rows = jnp.where(valid, jnp.clip(jnp.take(counts, be) - i_in * bm, 0, bm), 0)
    nsub = (rows + sub - 1) // sub
    plan = jnp.stack([
        be, jnp.take(ordinal, be) % 2, (valid & (i_in == 0)).astype(jnp.int32), nxt_b, lo, hi, nsub,
        jnp.full((n_blk,), n_valid, jnp.int32),
    ]).astype(jnp.int32)
    per = bm // sub
    sub_rows = rows[:, None] - jnp.arange(per, dtype=jnp.int32)[None, :] * sub
    zneed = (sub_rows < sub).astype(jnp.int32).reshape(n_blk * per)
    return blk_start * bm, plan, zneed


def _combine_kernel(dcur_ref, dnxt_ref, x1_ref, rg_ref, gfin_ref, os_ref, ya_ref, yb_ref, gbuf_ref, sem,
                    *, final_norm, n_first, group):
    step = pl.program_id(0)
    tm = x1_ref.shape[0]
    half = gbuf_ref.shape[3]
    cur = step % 2

    def gather(dref, buf):
        def body(g, c):
            base = g * group
            for r in range(group):
                for k in range(TOP_K):
                    d = dref[0, 0, (base + r) * TOP_K + k]
                    pltpu.make_async_copy(os_ref.at[pl.ds(d, 1), :], gbuf_ref.at[buf, k, pl.ds(base + r, 1), :],
                                          sem.at[buf]).start(priority=k % 2)
            return c
        lax.fori_loop(0, tm // group, body, 0)

    @pl.when(step == 0)
    def _():
        gather(dcur_ref, 0)

    @pl.when(step + 1 < pl.num_programs(0))
    def _():
        gather(dnxt_ref, 1 - cur)

    for k in range(TOP_K):
        pltpu.make_async_copy(os_ref.at[pl.ds(0, tm), :], gbuf_ref.at[cur, k], sem.at[cur]).wait()

    x1 = x1_ref[...]
    rg = rg_ref[...]
    lo = x1[:, :half]
    hi = x1[:, half:]
    for k in range(TOP_K):
        g = rg[:, k:k + 1]
        v = gbuf_ref[cur, k]
        lo = lo + g * _unpack_lo(v)
        hi = hi + g * _unpack_hi(v)
    if final_norm:
        ms = (jnp.sum(lo * lo, axis=-1, keepdims=True) + jnp.sum(hi * hi, axis=-1, keepdims=True)) / (2 * half)
        scale = lax.rsqrt(ms + EPS)
        gf = gfin_ref[...]
        lo = lo * scale * gf[:, :half]
        hi = hi * scale * gf[:, half:]

    @pl.when(step < n_first)
    def _():
        ya_ref[:, :half] = lo
        ya_ref[:, half:] = hi

    @pl.when(step >= n_first)
    def _():
        yb_ref[:, :half] = lo
        yb_ref[:, half:] = hi


def _combine(dest, x1, rg, gfin, os_, t_first, tm, final_norm):
    t, d = x1.shape
    half = d // 2
    n_tiles = t // tm
    n_first = t_first // tm
    dest3 = dest.reshape(n_tiles, 1, tm * TOP_K)
    return pl.pallas_call(
        functools.partial(_combine_kernel, final_norm=final_norm, n_first=n_first, group=min(DMA_GROUP, tm)),
        grid=(n_tiles,),
        in_specs=[
            pl.BlockSpec((1, 1, tm * TOP_K), lambda i: (i, 0, 0), memory_space=pltpu.SMEM),
            pl.BlockSpec((1, 1, tm * TOP_K), lambda i: (jnp.minimum(i + 1, n_tiles - 1), 0, 0),
                         memory_space=pltpu.SMEM),
            pl.BlockSpec((tm, d), lambda i: (i, 0)),
            pl.BlockSpec((tm, LANES), lambda i: (i, 0)),
            pl.BlockSpec((1, d), lambda i: (0, 0)),
            pl.BlockSpec(memory_space=pl.ANY),
        ],
        out_specs=_pair_specs(tm, d, n_first),
        out_shape=[jax.ShapeDtypeStruct((t_first, d), F32), jax.ShapeDtypeStruct((t - t_first, d), F32)],
        scratch_shapes=[pltpu.VMEM((2, TOP_K, tm, half), U32), pltpu.SemaphoreType.DMA((2,))],
        compiler_params=_params(("arbitrary",)),
        name="combine",
    )(dest3, dest3, x1, rg, gfin, os_)


def _tile(n, pref):
    while n % pref:
        pref //= 2
    return pref


def _layer(xa, xb, seq_a, seq_b, p, final_g):
    ta, d = xa.shape
    t = ta + xb.shape[0]
    dg = p["gmlp_ln_g"].shape[-1]
    chunk = p["gmlp_ws"].shape[-1]
    d_lru = p["conv_w"].shape[-1]
    heads_l, hd_l = p["lru_wr"].shape[1], p["lru_wr"].shape[2]
    n_exp = p["router_w"].shape[-1]
    assert p["w_down"].shape[1] == d, "expert width must equal the model width (shared weight-chunk plan)"

    unit = max(seq_a, seq_b)
    assert unit % seq_a == 0 and unit % seq_b == 0 and ta % unit == 0 and (t - ta) % unit == 0
    tm = _tile(math.gcd(ta, t - ta), 512)
    z = _inproj(xa, xb, p["mix_norm_g"][None], p["w_in"].astype(BF16), tm)

    bs_b = jnp.broadcast_to(p["gmlp_bs"][:, :, None], p["gmlp_ws"].shape).astype(F32)
    out_a = _gmlp(z, p["gmlp_ln_g"][None], p["gmlp_ln_b"][None], p["gmlp_ws"].astype(BF16), bs_b,
                  p["out_norm_a"][None], max(tm, chunk))

    wri = jnp.concatenate([p["lru_wr"], p["lru_wi"]], axis=-1).astype(BF16)
    bri = jnp.concatenate([p["lru_br"].reshape(2, heads_l, 1, hd_l),
                           p["lru_bi"].reshape(2, heads_l, 1, hd_l)], axis=-1)
    lam = p["lru_lam"].reshape(2, heads_l, 1, hd_l)
    unit_seq = jnp.asarray([seq_a] * (ta // unit) + [seq_b] * ((t - ta) // unit), jnp.int32)
    out_b = _rglru(z, unit_seq, unit, 2 * dg // hd_l, (2 * dg + d_lru) // hd_l,
                   p["conv_w"], p["conv_b"][None], wri, bri, lam, _tile(math.gcd(seq_a, seq_b), 512))

    rw = jnp.zeros((d, LANES), BF16).at[:, :n_exp].set(p["router_w"].astype(BF16))
    rb = jnp.full((1, LANES), NEG_BIG, F32).at[0, :n_exp].set(p["router_b"])
    x1, hp, ri, rg, cnt = _outproj(xa, xb, out_a, out_b, p["out_norm_b"][None], p["w_out"].astype(BF16),
                                   p["ffn_norm_g"][None], rw, rb, tm)

    m = t * TOP_K
    bm = _tile(m, ROW_BLOCK)
    sub = min(SUB_BLOCK, bm)
    kc = _tile(d, WEIGHT_CHUNK_ROWS)
    counts = cnt[0, :n_exp]
    n_blk = m // bm + n_exp
    n_slots = n_blk * bm
    pad_start, plan, zneed = _expert_plan(counts, bm, sub, n_blk, d // kc)
    dest = (jnp.take(pad_start, ri[:, :TOP_K]) + ri[:, TOP_K:2 * TOP_K]).reshape(m)

    xs = _dispatch(zneed, dest, hp, n_slots, sub, _tile(t, 512))
    act = _gmm(plan, xs, p["b_gate_up"][:, None, :], p["w_gate_up"], "gate_up", bm, sub, kc)
    os_ = _gmm(plan, act, p["b_down"][:, None, :], p["w_down"], "down", bm, sub, kc)
    gfin = jnp.ones((1, d), F32) if final_g is None else final_g[None]
    return _combine(dest, x1, rg, gfin, os_, ta, _tile(math.gcd(ta, t - ta), 256), final_g is not None)


_LAYER_KEYS = ("mix_norm_g", "w_in", "gmlp_ln_g", "gmlp_ln_b", "gmlp_ws", "gmlp_bs", "conv_w", "conv_b",
               "lru_wr", "lru_br", "lru_wi", "lru_bi", "lru_lam", "out_norm_a", "out_norm_b", "w_out",
               "ffn_norm_g", "router_w", "router_b", "w_gate_up", "b_gate_up", "w_down", "b_down")


def kernel(x_prompt, x_sample, mix_norm_g, w_in, gmlp_ln_g, gmlp_ln_b, gmlp_ws, gmlp_bs, conv_w, conv_b, lru_wr, lru_br, lru_wi, lru_bi, lru_lam, out_norm_a, out_norm_b, w_out, ffn_norm_g, router_w, router_b, w_gate_up, b_gate_up, w_down, b_down, final_norm_g):
    stacked = dict(zip(_LAYER_KEYS, (mix_norm_g, w_in, gmlp_ln_g, gmlp_ln_b, gmlp_ws, gmlp_bs, conv_w, conv_b,
                                     lru_wr, lru_br, lru_wi, lru_bi, lru_lam, out_norm_a, out_norm_b, w_out,
                                     ffn_norm_g, router_w, router_b, w_gate_up, b_gate_up, w_down, b_down)))
    depth = w_in.shape[0]
    bp, sp, d = x_prompt.shape
    bs, ss, _ = x_sample.shape
    xa = x_prompt.reshape(bp * sp, d)
    xb = x_sample.reshape(bs * ss, d)
    for layer in range(depth):
        p = {k: v[layer] for k, v in stacked.items()}
        xa, xb = _layer(xa, xb, sp, ss, p, final_norm_g if layer == depth - 1 else None)
    return xa.reshape(bp, sp, d), xb.reshape(bs, ss, d)
```

```python
import functools
import math

import jax
import jax.numpy as jnp
from jax import lax
from jax.experimental import pallas as pl
from jax.experimental.pallas import tpu as pltpu

TOP_K = 4
LRU_C = 8.0
SWIGLU_LIMIT = 7.0
SWIGLU_ALPHA = 1.702
EPS = 1e-6

LANES = 128
VMEM_LIMIT = 56 * 1024 * 1024
NEG_BIG = -1e30
ROW_BLOCK = 512
SUB_BLOCK = 128
WEIGHT_CHUNK_ROWS = 128
DMA_GROUP = 8

F32 = jnp.float32
BF16 = jnp.bfloat16
U32 = jnp.uint32


def _params(sem):
    return pltpu.CompilerParams(dimension_semantics=sem, vmem_limit_bytes=VMEM_LIMIT)


def _gelu(x):
    return 0.5 * x * (1.0 + jnp.tanh(0.7978845608028654 * (x + 0.044715 * (x * x * x))))


def _rms(x, g):
    return x * lax.rsqrt(jnp.mean(x * x, axis=-1, keepdims=True) + EPS) * g


def _pack_bf16_pair(lo, hi):
    lo_bits = lax.bitcast_convert_type(lo.astype(BF16).astype(F32), U32)
    hi_bits = lax.bitcast_convert_type(hi.astype(BF16).astype(F32), U32)
    return (lo_bits >> 16) | (hi_bits & jnp.uint32(0xFFFF0000))


def _unpack_lo(v):
    return lax.bitcast_convert_type(v << 16, F32)


def _unpack_hi(v):
    return lax.bitcast_convert_type(v & jnp.uint32(0xFFFF0000), F32)


def _pair_specs(tm, d, n_first):
    return [pl.BlockSpec((tm, d), lambda i: (jnp.minimum(i, n_first - 1), 0)),
            pl.BlockSpec((tm, d), lambda i: (jnp.maximum(i - n_first, 0), 0))]


def _pair_tile(xa_ref, xb_ref, n_first):
    return jnp.where(pl.program_id(0) < n_first, xa_ref[...], xb_ref[...])


def _inproj_kernel(xa_ref, xb_ref, g_ref, w_ref, z_ref, *, n_chunk, n_first):
    xn = _rms(_pair_tile(xa_ref, xb_ref, n_first), g_ref[...]).astype(BF16)
    for n in range(w_ref.shape[1] // n_chunk):
        cols = slice(n * n_chunk, (n + 1) * n_chunk)
        z_ref[:, cols] = jnp.dot(xn, w_ref[:, cols], preferred_element_type=F32).astype(z_ref.dtype)


def _inproj(xa, xb, g, w, tm):
    d = xa.shape[1]
    t = xa.shape[0] + xb.shape[0]
    d_in = w.shape[1]
    n_first = xa.shape[0] // tm
    return pl.pallas_call(
        functools.partial(_inproj_kernel, n_chunk=min(1024, d_in), n_first=n_first),
        grid=(t // tm,),
        in_specs=_pair_specs(tm, d, n_first) + [
            pl.BlockSpec((1, d), lambda i: (0, 0)),
            pl.BlockSpec((d, d_in), lambda i: (0, 0), pipeline_mode=pl.Buffered(1)),
        ],
        out_specs=pl.BlockSpec((tm, d_in), lambda i: (i, 0)),
        out_shape=jax.ShapeDtypeStruct((t, d_in), BF16),
        compiler_params=_params(("arbitrary",)),
        name="inproj",
    )(xa, xb, g, w)


def _gmlp_kernel(u_ref, v_ref, lng_ref, lnb_ref, ws_ref, bs_ref, ga_ref, o_ref, mix_ref, *, heads, chunk):
    v = _gelu(v_ref[...].astype(F32))
    mu = jnp.mean(v, axis=-1, keepdims=True)
    vc = v - mu
    var = jnp.mean(vc * vc, axis=-1, keepdims=True)
    vn = (vc * lax.rsqrt(var + EPS) * lng_ref[...] + lnb_ref[...]).astype(BF16)
    tm, dg = vn.shape
    hd = dg // heads
    for c in range(tm // chunk):
        rows = slice(c * chunk, (c + 1) * chunk)
        for h in range(heads):
            cols = slice(h * hd, (h + 1) * hd)
            mixed = jnp.dot(ws_ref[h], vn[rows, cols], preferred_element_type=F32) + bs_ref[h]
            mix_ref[rows, cols] = mixed
    out = _gelu(u_ref[...].astype(F32)) * mix_ref[...]
    o_ref[...] = _rms(out, ga_ref[...]).astype(o_ref.dtype)


def _gmlp(z, ln_g, ln_b, ws, bs_b, ga, tm):
    t = z.shape[0]
    heads, chunk, _ = ws.shape
    dg = ln_g.shape[1]
    vec = pl.BlockSpec((1, dg), lambda i: (0, 0))
    return pl.pallas_call(
        functools.partial(_gmlp_kernel, heads=heads, chunk=chunk),
        grid=(t // tm,),
        in_specs=[
            pl.BlockSpec((tm, dg), lambda i: (i, 0)),
            pl.BlockSpec((tm, dg), lambda i: (i, 1)),
            vec, vec,
            pl.BlockSpec(ws.shape, lambda i: (0, 0, 0)),
            pl.BlockSpec(bs_b.shape, lambda i: (0, 0, 0)),
            vec,
        ],
        out_specs=pl.BlockSpec((tm, dg), lambda i: (i, 0)),
        out_shape=jax.ShapeDtypeStruct((t, dg), BF16),
        scratch_shapes=[pltpu.VMEM((tm, dg), F32)],
        compiler_params=_params(("arbitrary",)),
        name="gmlp",
    )(z, z, ln_g, ln_b, ws, bs_b, ga)


def _rot8(x, s):
    n, w = x.shape
    return pltpu.roll(x.reshape(n // 8, 8, w), s, axis=1).reshape(n, w)


def _row_in_group(shape):
    return lax.broadcasted_iota(jnp.int32, shape, 0) & 7


def _shift_rows(x, s, fill, reverse, row):
    if reverse:
        return jnp.where(row >= 8 - s, fill, _rot8(x, 8 - s))
    return jnp.where(row < s, fill, _rot8(x, s))


def _lru_scan_chunk(a, b, h, reverse):
    row = _row_in_group(a.shape)
    for s in (1, 2, 4):
        b = a * _shift_rows(b, s, 0.0, reverse, row) + b
        a = a * _shift_rows(a, s, 1.0, reverse, row)
    n_grp = a.shape[0] // 8
    out = [None] * n_grp
    order = range(n_grp - 1, -1, -1) if reverse else range(n_grp)
    edge = 0 if reverse else 7
    for gi in order:
        rows = slice(gi * 8, gi * 8 + 8)
        hg = b[rows] + a[rows] * h
        out[gi] = hg
        h = hg[edge:edge + 1]
    return jnp.concatenate(out, axis=0), h


def _rglru_kernel(seqlen_ref, xr_ref, gr_ref, cw_ref, cb_ref, wri_ref, bri_ref, lam_ref, o_ref, hf_ref, xc_ref,
                  *, tc):
    rows, hd = xr_ref.shape
    n_ck = rows // tc
    halo = 16
    cw = cw_ref[...]
    cb = cb_ref[...]
    seq = seqlen_ref[pl.program_id(0)]

    def starts_seq(c):
        return lax.rem(c * tc, seq) == 0

    def ends_seq(c):
        return lax.rem((c + 1) * tc, seq) == 0

    def conv_chunk(c):
        t0 = pl.multiple_of(c * tc, tc)
        main = xr_ref[pl.ds(t0, tc), :].astype(F32)
        p0 = pl.multiple_of(jnp.maximum(t0 - halo, 0), halo)
        n0 = pl.multiple_of(jnp.minimum(t0 + tc, rows - halo), halo)
        prev = jnp.where(starts_seq(c), 0.0, xr_ref[pl.ds(p0, halo), :].astype(F32))
        nxt = jnp.where(ends_seq(c), 0.0, xr_ref[pl.ds(n0, halo), :].astype(F32))
        ext = jnp.concatenate([prev, main, nxt], axis=0)
        row = _row_in_group(main.shape)

        def earlier(s):
            r = _rot8(ext, s)
            return jnp.where(row < s, r[halo - 8:halo - 8 + tc], r[halo:halo + tc])

        r7 = _rot8(ext, 7)
        later = jnp.where(row >= 7, r7[halo + 8:halo + 8 + tc], r7[halo:halo + tc])
        acc = cb + cw[2:3] * main
        acc = acc + cw[0:1] * earlier(2)
        acc = acc + cw[1:2] * earlier(1)
        acc = acc + cw[3:4] * later
        return acc

    def direction(d, xc, h):
        g = jnp.tanh(jnp.dot(xc.astype(BF16), wri_ref[d], preferred_element_type=F32) + bri_ref[d])
        i = 0.5 * g[:, hd:] + 0.5
        nl = -lam_ref[d]
        softplus = jnp.maximum(nl, 0.0) + jnp.log1p(jnp.exp(-jnp.abs(nl)))
        half_c = (-0.5 * LRU_C) * softplus
        log_a = half_c * g[:, :hd] + half_c
        a = jnp.exp(log_a)
        b = jnp.sqrt(-jnp.tanh(log_a) * (a * a + 1.0)) * i * xc
        return _lru_scan_chunk(a, b, h, reverse=(d == 1))

    def fwd(c, h):
        t0 = pl.multiple_of(c * tc, tc)
        xc = conv_chunk(c)
        xc_ref[pl.ds(t0, tc), :] = xc
        hs, h = direction(0, xc, jnp.where(starts_seq(c), 0.0, h))
        hf_ref[pl.ds(t0, tc), :] = hs
        return h

    def bwd(k, h):
        c = n_ck - 1 - k
        t0 = pl.multiple_of(c * tc, tc)
        hs, h = direction(1, xc_ref[pl.ds(t0, tc), :], jnp.where(ends_seq(c), 0.0, h))
        tot = hf_ref[pl.ds(t0, tc), :] + hs
        o_ref[pl.ds(t0, tc), :] = (_gelu(gr_ref[pl.ds(t0, tc), :].astype(F32)) * tot).astype(o_ref.dtype)
        return h

    h0 = jnp.zeros((1, hd), F32)
    lax.fori_loop(0, n_ck, fwd, h0)
    lax.fori_loop(0, n_ck, bwd, h0)


def _rglru(z, unit_seq, unit, col_x, col_g, cw, cb, wri, bri, lam, tc):
    t = z.shape[0]
    _, heads, hd, _ = wri.shape
    grid_spec = pltpu.PrefetchScalarGridSpec(
        num_scalar_prefetch=1,
        grid=(t // unit, heads),
        in_specs=[
            pl.BlockSpec((unit, hd), lambda u, h, sl: (u, col_x + h)),
            pl.BlockSpec((unit, hd), lambda u, h, sl: (u, col_g + h)),
            pl.BlockSpec((cw.shape[0], hd), lambda u, h, sl: (0, h)),
            pl.BlockSpec((1, hd), lambda u, h, sl: (0, h)),
            pl.BlockSpec((2, None, hd, 2 * hd), lambda u, h, sl: (0, h, 0, 0)),
            pl.BlockSpec((2, None, 1, 2 * hd), lambda u, h, sl: (0, h, 0, 0)),
            pl.BlockSpec((2, None, 1, hd), lambda u, h, sl: (0, h, 0, 0)),
        ],
        out_specs=pl.BlockSpec((unit, hd), lambda u, h, sl: (u, h)),
        scratch_shapes=[pltpu.VMEM((unit, hd), F32), pltpu.VMEM((unit, hd), F32)],
    )
    return pl.pallas_call(
        functools.partial(_rglru_kernel, tc=tc),
        grid_spec=grid_spec,
        out_shape=jax.ShapeDtypeStruct((t, heads * hd), BF16),
        compiler_params=_params(("arbitrary", "arbitrary")),
        name="rglru",
    )(unit_seq, z, z, cw, cb, wri, bri, lam)


def _outproj_kernel(xa_ref, xb_ref, a_ref, b_ref, gb_ref, wo_ref, gf_ref, rw_ref, rb_ref,
                    x1_ref, hp_ref, ri_ref, rg_ref, cnt_ref, carry_ref, *, n_first):
    step = pl.program_id(0)

    @pl.when(step == 0)
    def _():
        carry_ref[...] = jnp.zeros_like(carry_ref)

    da = a_ref.shape[1]
    obn = _rms(b_ref[...].astype(F32), gb_ref[...]).astype(BF16)
    y = jnp.dot(a_ref[...], wo_ref[:da], preferred_element_type=F32)
    y = y + jnp.dot(obn, wo_ref[da:], preferred_element_type=F32)
    x1 = _pair_tile(xa_ref, xb_ref, n_first) + y
    x1_ref[...] = x1
    hf = _rms(x1, gf_ref[...])
    half = hf.shape[1] // 2
    hp_ref[...] = _pack_bf16_pair(hf[:, :half], hf[:, half:])

    logits = jnp.dot(hf.astype(BF16), rw_ref[...], preferred_element_type=F32) + rb_ref[...]
    tm = logits.shape[0]
    lane = lax.broadcasted_iota(jnp.int32, logits.shape, 1)
    lane_f = lane.astype(F32)
    work = logits
    sels, vals = [], []
    for _ in range(TOP_K):
        m = jnp.max(work, axis=-1, keepdims=True)
        sel = jnp.min(jnp.where(work == m, lane_f, float(LANES)), axis=-1, keepdims=True)
        hit = lane_f == sel
        sels.append(hit)
        vals.append(m)
        work = jnp.where(hit, -jnp.inf, work)
    exps = [jnp.exp(v - vals[0]) for v in vals]
    denom = exps[0]
    for e in exps[1:]:
        denom = denom + e

    onehot = sels[0]
    for hit in sels[1:]:
        onehot = onehot | hit
    cnt = onehot.astype(F32)
    rr = lax.broadcasted_iota(jnp.int32, (tm, tm), 0)
    cc = lax.broadcasted_iota(jnp.int32, (tm, tm), 1)
    tril = (rr > cc).astype(BF16)
    before = jnp.dot(tril, cnt.astype(BF16), preferred_element_type=F32) + carry_ref[...]

    ri = jnp.zeros(logits.shape, F32)
    rg = jnp.zeros(logits.shape, F32)
    for k in range(TOP_K):
        e_k = jnp.sum(jnp.where(sels[k], lane_f, 0.0), axis=-1, keepdims=True)
        rank_k = jnp.sum(jnp.where(sels[k], before, 0.0), axis=-1, keepdims=True)
        ri = jnp.where(lane == k, e_k, ri)
        ri = jnp.where(lane == TOP_K + k, rank_k, ri)
        rg = jnp.where(lane == k, exps[k] / denom, rg)
    ri_ref[...] = ri.astype(jnp.int32)
    rg_ref[...] = rg
    carry_ref[...] = carry_ref[...] + jnp.sum(cnt, axis=0, keepdims=True)
    cnt_ref[...] = carry_ref[...].astype(jnp.int32)


def _outproj(xa, xb, out_a, out_b, gb, wo, gf, rw, rb, tm):
    d = xa.shape[1]
    t = xa.shape[0] + xb.shape[0]
    n_first = xa.shape[0] // tm
    da = out_a.shape[1]
    db = out_b.shape[1]
    row = lambda i: (i, 0)
    fixed = lambda i: (0, 0)
    return pl.pallas_call(
        functools.partial(_outproj_kernel, n_first=n_first),
        grid=(t // tm,),
        in_specs=_pair_specs(tm, d, n_first) + [
            pl.BlockSpec((tm, da), row),
            pl.BlockSpec((tm, db), row),
            pl.BlockSpec((1, db), fixed),
            pl.BlockSpec(wo.shape, fixed, pipeline_mode=pl.Buffered(1)),
            pl.BlockSpec((1, d), fixed),
            pl.BlockSpec(rw.shape, fixed),
            pl.BlockSpec((1, LANES), fixed),
        ],
        out_specs=[
            pl.BlockSpec((tm, d), row),
            pl.BlockSpec((tm, d // 2), row),
            pl.BlockSpec((tm, LANES), row),
            pl.BlockSpec((tm, LANES), row),
            pl.BlockSpec((1, LANES), fixed),
        ],
        out_shape=[
            jax.ShapeDtypeStruct((t, d), F32),
            jax.ShapeDtypeStruct((t, d // 2), U32),
            jax.ShapeDtypeStruct((t, LANES), jnp.int32),
            jax.ShapeDtypeStruct((t, LANES), F32),
            jax.ShapeDtypeStruct((1, LANES), jnp.int32),
        ],
        scratch_shapes=[pltpu.VMEM((1, LANES), F32)],
        compiler_params=_params(("arbitrary",)),
        name="outproj_router",
    )(xa, xb, out_a, out_b, gb, wo, gf, rw, rb)


def _dispatch_kernel(zneed_ref, dest_ref, hp_ref, xs_ref, zero_ref, sem, zsem, *, sub, group):
    step = pl.program_id(0)
    tm = hp_ref.shape[0]
    n_zc = zneed_ref.shape[0]

    def zero_copy(j):
        return pltpu.make_async_copy(zero_ref, xs_ref.at[pl.ds(pl.multiple_of(j * sub, sub), sub), :], zsem)

    @pl.when(step == 0)
    def _():
        zero_ref[...] = jnp.zeros_like(zero_ref)

        def start(j, c):
            @pl.when(zneed_ref[j] != 0)
            def _():
                zero_copy(j).start()
            return c

        def wait(j, c):
            @pl.when(zneed_ref[j] != 0)
            def _():
                zero_copy(j).wait()
            return c

        lax.fori_loop(0, n_zc, start, 0)
        lax.fori_loop(0, n_zc, wait, 0)

    def issue(g, c):
        base = g * group
        for r in range(group):
            for k in range(TOP_K):
                d = dest_ref[0, 0, (base + r) * TOP_K + k]
                pltpu.make_async_copy(hp_ref.at[pl.ds(base + r, 1), :], xs_ref.at[pl.ds(d, 1), :],
                                      sem).start(priority=k % 2)
        return c

    lax.fori_loop(0, tm // group, issue, 0)
    for _ in range(TOP_K):
        pltpu.make_async_copy(hp_ref, xs_ref.at[pl.ds(0, tm), :], sem).wait()


def _dispatch(zneed, dest, hp, n_slots, sub, tm):
    t, half = hp.shape
    dest3 = dest.reshape(t // tm, 1, tm * TOP_K)
    grid_spec = pltpu.PrefetchScalarGridSpec(
        num_scalar_prefetch=1,
        grid=(t // tm,),
        in_specs=[
            pl.BlockSpec((1, 1, tm * TOP_K), lambda i, *_: (i, 0, 0), memory_space=pltpu.SMEM),
            pl.BlockSpec((tm, half), lambda i, *_: (i, 0)),
        ],
        out_specs=pl.BlockSpec(memory_space=pl.ANY),
        scratch_shapes=[pltpu.VMEM((sub, half), U32), pltpu.SemaphoreType.DMA, pltpu.SemaphoreType.DMA],
    )
    return pl.pallas_call(
        functools.partial(_dispatch_kernel, sub=sub, group=min(DMA_GROUP, tm)),
        grid_spec=grid_spec,
        out_shape=jax.ShapeDtypeStruct((n_slots, half), U32),
        compiler_params=_params(("arbitrary",)),
        name="dispatch",
    )(zneed, dest3, hp)


def _gmm_kernel(plan_ref, x_ref, bias_ref, w_hbm, o_ref, wbuf, stage, sem, *rest, mode, kc, sub, nw):
    blk = pl.program_id(0)
    n_blk = plan_ref.shape[1]
    expert = plan_ref[0, blk]
    half_idx = plan_ref[1, blk]
    nxt = plan_ref[3, blk]
    valid = blk < plan_ref[7, 0]
    n_sub = jnp.where(valid, plan_ref[6, blk], 0)
    n_ch = w_hbm.shape[1] // kc
    del n_blk

    def chunk_rows(k):
        return pl.ds(pl.multiple_of(k * kc, kc), kc)

    def chunk_copy(e, k, st):
        return pltpu.make_async_copy(w_hbm.at[e, chunk_rows(k), :], stage.at[st], sem.at[st])

    def land(e, k, dst):
        st = k % 2
        chunk_copy(e, k, st).wait()
        wbuf[dst, chunk_rows(k), :] = stage[st].astype(BF16)

        @pl.when(k + 2 < n_ch)
        def _():
            chunk_copy(e, k + 2, st).start()

    @pl.when(blk == 0)
    def _():
        chunk_copy(expert, 0, 0).start()
        chunk_copy(expert, 1, 1).start()
        lax.fori_loop(0, n_ch, lambda k, c: (land(expert, k, half_idx), c)[1], 0)

    @pl.when(valid & (plan_ref[2, blk] == 1) & (nxt >= 0))
    def _():
        chunk_copy(nxt, 0, 0).start()
        chunk_copy(nxt, 1, 1).start()

    if mode == "gate_up":
        xb_ref, = rest
        half = x_ref.shape[1]
        de = o_ref.shape[1]

        def run(rows):
            v = x_ref[rows, :]
            xb_ref[rows, :half] = _unpack_lo(v).astype(BF16)
            xb_ref[rows, half:] = _unpack_hi(v).astype(BF16)
            x = xb_ref[rows, :]
            for n in range(de // nw):
                cg = slice(n * nw, (n + 1) * nw)
                cu = slice(de + n * nw, de + (n + 1) * nw)
                gate = jnp.dot(x, wbuf[half_idx, :, cg], preferred_element_type=F32) + bias_ref[:, cg]
                up = jnp.dot(x, wbuf[half_idx, :, cu], preferred_element_type=F32) + bias_ref[:, cu]
                gate = jnp.minimum(gate, SWIGLU_LIMIT)
                up = jnp.clip(up, -SWIGLU_LIMIT, SWIGLU_LIMIT)
                act = (up + 1.0) * (gate * jax.nn.sigmoid(SWIGLU_ALPHA * gate))
                o_ref[rows, cg] = act.astype(o_ref.dtype)
    else:
        half = o_ref.shape[1]

        def run(rows):
            x = x_ref[rows, :]
            for n in range(half // nw):
                cl = slice(n * nw, (n + 1) * nw)
                ch = slice(half + n * nw, half + (n + 1) * nw)
                lo = jnp.dot(x, wbuf[half_idx, :, cl], preferred_element_type=F32) + bias_ref[:, cl]
                hi = jnp.dot(x, wbuf[half_idx, :, ch], preferred_element_type=F32) + bias_ref[:, ch]
                o_ref[rows, cl] = _pack_bf16_pair(lo, hi)

    full = x_ref.shape[0] // sub

    @pl.when(n_sub < full)
    def _():
        o_ref[...] = jnp.zeros_like(o_ref)

    @pl.when(n_sub == full)
    def _():
        run(slice(None))

    @pl.when((n_sub > 0) & (n_sub < full))
    def _():
        def body(i, c):
            run(pl.ds(pl.multiple_of(i * sub, sub), sub))
            return c
        lax.fori_loop(0, n_sub, body, 0)

    lax.fori_loop(plan_ref[4, blk], plan_ref[5, blk], lambda k, c: (land(nxt, k, 1 - half_idx), c)[1], 0)


def _gmm(plan, x, bias, w, mode, bm, sub, kc):
    n_slots = x.shape[0]
    n_exp, kdim, ndim = w.shape
    n_blk = n_slots // bm
    if mode == "gate_up":
        out_cols, out_dtype = ndim // 2, BF16
        extra = [pltpu.VMEM((bm, kdim), BF16)]
    else:
        out_cols, out_dtype = ndim // 2, U32
        extra = []

    def live(b, plan):
        return (jnp.minimum(b, plan[7, 0] - 1), 0)

    grid_spec = pltpu.PrefetchScalarGridSpec(
        num_scalar_prefetch=1,
        grid=(n_blk,),
        in_specs=[
            pl.BlockSpec((bm, x.shape[1]), live),
            pl.BlockSpec((None, 1, ndim), lambda b, plan: (plan[0, b], 0, 0)),
            pl.BlockSpec(memory_space=pl.ANY),
        ],
        out_specs=pl.BlockSpec((bm, out_cols), lambda b, plan: (b, 0)),
        scratch_shapes=[
            pltpu.VMEM((2, kdim, ndim), BF16),
            pltpu.VMEM((2, kc, ndim), F32),
            pltpu.SemaphoreType.DMA((2,)),
        ] + extra,
    )
    return pl.pallas_call(
        functools.partial(_gmm_kernel, mode=mode, kc=kc, sub=sub, nw=min(512, out_cols)),
        grid_spec=grid_spec,
        out_shape=jax.ShapeDtypeStruct((n_slots, out_cols), out_dtype),
        compiler_params=_params(("arbitrary",)),
        name="experts_" + mode,
    )(plan, x, bias, w)


def _expert_plan(counts, bm, sub, n_blk, n_ch):
    n_exp = counts.shape[0]
    nb_e = (counts + bm - 1) // bm
    blk_end = jnp.cumsum(nb_e)
    blk_start = blk_end - nb_e
    n_valid = blk_end[-1]
    blk = jnp.arange(n_blk, dtype=jnp.int32)
    last = jnp.minimum(blk, n_valid - 1)
    be = jnp.minimum(jnp.sum(blk_end[None, :] <= last[:, None], axis=1), n_exp - 1).astype(jnp.int32)
    valid = blk < n_valid
    i_in = last - jnp.take(blk_start, be)
    nb_b = jnp.take(nb_e, be)
    nonempty = nb_e > 0
    ordinal = jnp.cumsum(nonempty.astype(jnp.int32)) - 1
    idx = jnp.where(nonempty, jnp.arange(n_exp, dtype=jnp.int32), n_exp)
    nxt_incl = lax.cummin(idx, axis=0, reverse=True)
    nxt_e = jnp.concatenate([nxt_incl[1:], jnp.full((1,), n_exp, jnp.int32)])
    nxt_e = jnp.where(nxt_e >= n_exp, -1, nxt_e)
    nxt_b = jnp.take(nxt_e, be)
    has_next = valid & (nxt_b >= 0)
    lo = jnp.where(has_next, i_in * n_ch // nb_b, 0)
    hi = jnp.where(has_next, (i_in + 1) * n_ch // nb_b, 0)
    rows = jnp.where(valid, jnp.clip(jnp.take(counts, be) - i_in * bm, 0, bm), 0)
    nsub = (rows + sub - 1) // sub
    plan = jnp.stack([
        be, jnp.take(ordinal, be) % 2, (valid & (i_in == 0)).astype(jnp.int32), nxt_b, lo, hi, nsub,
        jnp.full((n_blk,), n_valid, jnp.int32),
    ]).astype(jnp.int32)
    per = bm // sub
    sub_rows = rows[:, None] - jnp.arange(per, dtype=jnp.int32)[None, :] * sub
    zneed = (sub_rows < sub).astype(jnp.int32).reshape(n_blk * per)
    return blk_start * bm, plan, zneed


def _combine_kernel(dcur_ref, dnxt_ref, x1_ref, rg_ref, gfin_ref, os_ref, ya_ref, yb_ref, gbuf_ref, sem,
                    *, final_norm, n_first, group):
    step = pl.program_id(0)
    tm = x1_ref.shape[0]
    half = gbuf_ref.shape[3]
    cur = step % 2

    def gather(dref, buf):
        def body(g, c):
            base = g * group
            for r in range(group):
                for k in range(TOP_K):
                    d = dref[0, 0, (base + r) * TOP_K + k]
                    pltpu.make_async_copy(os_ref.at[pl.ds(d, 1), :], gbuf_ref.at[buf, k, pl.ds(base + r, 1), :],
                                          sem.at[buf]).start(priority=k % 2)
            return c
        lax.fori_loop(0, tm // group, body, 0)

    @pl.when(step == 0)
    def _():
        gather(dcur_ref, 0)

    @pl.when(step + 1 < pl.num_programs(0))
    def _():
        gather(dnxt_ref, 1 - cur)

    for k in range(TOP_K):
        pltpu.make_async_copy(os_ref.at[pl.ds(0, tm), :], gbuf_ref.at[cur, k], sem.at[cur]).wait()

    x1 = x1_ref[...]
    rg = rg_ref[...]
    lo = x1[:, :half]
    hi = x1[:, half:]
    for k in range(TOP_K):
        g = rg[:, k:k + 1]
        v = gbuf_ref[cur, k]
        lo = lo + g * _unpack_lo(v)
        hi = hi + g * _unpack_hi(v)
    if final_norm:
        ms = (jnp.sum(lo * lo, axis=-1, keepdims=True) + jnp.sum(hi * hi, axis=-1, keepdims=True)) / (2 * half)
        scale = lax.rsqrt(ms + EPS)
        gf = gfin_ref[...]
        lo = lo * scale * gf[:, :half]
        hi = hi * scale * gf[:, half:]

    @pl.when(step < n_first)
    def _():
        ya_ref[:, :half] = lo
        ya_ref[:, half:] = hi

    @pl.when(step >= n_first)
    def _():
        yb_ref[:, :half] = lo
        yb_ref[:, half:] = hi


def _combine(dest, x1, rg, gfin, os_, t_first, tm, final_norm):
    t, d = x1.shape
    half = d // 2
    n_tiles = t // tm
    n_first = t_first // tm
    dest3 = dest.reshape(n_tiles, 1, tm * TOP_K)
    return pl.pallas_call(
        functools.partial(_combine_kernel, final_norm=final_norm, n_first=n_first, group=min(DMA_GROUP, tm)),
        grid=(n_tiles,),
        in_specs=[
            pl.BlockSpec((1, 1, tm * TOP_K), lambda i: (i, 0, 0), memory_space=pltpu.SMEM),
            pl.BlockSpec((1, 1, tm * TOP_K), lambda i: (jnp.minimum(i + 1, n_tiles - 1), 0, 0),
                         memory_space=pltpu.SMEM),
            pl.BlockSpec((tm, d), lambda i: (i, 0)),
            pl.BlockSpec((tm, LANES), lambda i: (i, 0)),
            pl.BlockSpec((1, d), lambda i: (0, 0)),
            pl.BlockSpec(memory_space=pl.ANY),
        ],
        out_specs=_pair_specs(tm, d, n_first),
        out_shape=[jax.ShapeDtypeStruct((t_first, d), F32), jax.ShapeDtypeStruct((t - t_first, d), F32)],
        scratch_shapes=[pltpu.VMEM((2, TOP_K, tm, half), U32), pltpu.SemaphoreType.DMA((2,))],
        compiler_params=_params(("arbitrary",)),
        name="combine",
    )(dest3, dest3, x1, rg, gfin, os_)


def _tile(n, pref):
    while n % pref:
        pref //= 2
    return pref


def _layer(xa, xb, seq_a, seq_b, p, final_g):
    ta, d = xa.shape
    t = ta + xb.shape[0]
    dg = p["gmlp_ln_g"].shape[-1]
    chunk = p["gmlp_ws"].shape[-1]
    d_lru = p["conv_w"].shape[-1]
    heads_l, hd_l = p["lru_wr"].shape[1], p["lru_wr"].shape[2]
    n_exp = p["router_w"].shape[-1]
    assert p["w_down"].shape[1] == d, "expert width must equal the model width (shared weight-chunk plan)"

    unit = max(seq_a, seq_b)
    assert unit % seq_a == 0 and unit % seq_b == 0 and ta % unit == 0 and (t - ta) % unit == 0
    tm = _tile(math.gcd(ta, t - ta), 512)
    z = _inproj(xa, xb, p["mix_norm_g"][None], p["w_in"].astype(BF16), tm)

    bs_b = jnp.broadcast_to(p["gmlp_bs"][:, :, None], p["gmlp_ws"].shape).astype(F32)
    out_a = _gmlp(z, p["gmlp_ln_g"][None], p["gmlp_ln_b"][None], p["gmlp_ws"].astype(BF16), bs_b,
                  p["out_norm_a"][None], max(tm, chunk))

    wri = (0.5 * jnp.concatenate([p["lru_wr"], p["lru_wi"]], axis=-1)).astype(BF16)
    bri = 0.5 * jnp.concatenate([p["lru_br"].reshape(2, heads_l, 1, hd_l),
                                 p["lru_bi"].reshape(2, heads_l, 1, hd_l)], axis=-1)
    lam = p["lru_lam"].reshape(2, heads_l, 1, hd_l)
    unit_seq = jnp.asarray([seq_a] * (ta // unit) + [seq_b] * ((t - ta) // unit), jnp.int32)
    out_b = _rglru(z, unit_seq, unit, 2 * dg // hd_l, (2 * dg + d_lru) // hd_l,
                   p["conv_w"], p["conv_b"][None], wri, bri, lam, _tile(math.gcd(seq_a, seq_b), 512))

    rw = jnp.zeros((d, LANES), BF16).at[:, :n_exp].set(p["router_w"].astype(BF16))
    rb = jnp.full((1, LANES), NEG_BIG, F32).at[0, :n_exp].set(p["router_b"])
    x1, hp, ri, rg, cnt = _outproj(xa, xb, out_a, out_b, p["out_norm_b"][None], p["w_out"].astype(BF16),
                                   p["ffn_norm_g"][None], rw, rb, tm)

    m = t * TOP_K
    bm = _tile(m, ROW_BLOCK)
    sub = min(SUB_BLOCK, bm)
    kc = _tile(d, WEIGHT_CHUNK_ROWS)
    counts = cnt[0, :n_exp]
    n_blk = m // bm + n_exp
    n_slots = n_blk * bm
    pad_start, plan, zneed = _expert_plan(counts, bm, sub, n_blk, d // kc)
    dest = (jnp.take(pad_start, ri[:, :TOP_K]) + ri[:, TOP_K:2 * TOP_K]).reshape(m)

    xs = _dispatch(zneed, dest, hp, n_slots, sub, _tile(t, 512))
    act = _gmm(plan, xs, p["b_gate_up"][:, None, :], p["w_gate_up"], "gate_up", bm, sub, kc)
    os_ = _gmm(plan, act, p["b_down"][:, None, :], p["w_down"], "down", bm, sub, kc)
    gfin = jnp.ones((1, d), F32) if final_g is None else final_g[None]
    return _combine(dest, x1, rg, gfin, os_, ta, _tile(math.gcd(ta, t - ta), 256), final_g is not None)


_LAYER_KEYS = ("mix_norm_g", "w_in", "gmlp_ln_g", "gmlp_ln_b", "gmlp_ws", "gmlp_bs", "conv_w", "conv_b",
               "lru_wr", "lru_br", "lru_wi", "lru_bi", "lru_lam", "out_norm_a", "out_norm_b", "w_out",
               "ffn_norm_g", "router_w", "router_b", "w_gate_up", "b_gate_up", "w_down", "b_down")


def kernel(x_prompt, x_sample, mix_norm_g, w_in, gmlp_ln_g, gmlp_ln_b, gmlp_ws, gmlp_bs, conv_w, conv_b, lru_wr, lru_br, lru_wi, lru_bi, lru_lam, out_norm_a, out_norm_b, w_out, ffn_norm_g, router_w, router_b, w_gate_up, b_gate_up, w_down, b_down, final_norm_g):
    stacked = dict(zip(_LAYER_KEYS, (mix_norm_g, w_in, gmlp_ln_g, gmlp_ln_b, gmlp_ws, gmlp_bs, conv_w, conv_b,
                                     lru_wr, lru_br, lru_wi, lru_bi, lru_lam, out_norm_a, out_norm_b, w_out,
                                     ffn_norm_g, router_w, router_b, w_gate_up, b_gate_up, w_down, b_down)))
    depth = w_in.shape[0]
    bp, sp, d = x_prompt.shape
    bs, ss, _ = x_sample.shape
    xa = x_prompt.reshape(bp * sp, d)
    xb = x_sample.reshape(bs * ss, d)
    for layer in range(depth):
        p = {k: v[layer] for k, v in stacked.items()}
        xa, xb = _layer(xa, xb, sp, ss, p, final_norm_g if layer == depth - 1 else None)
    return xa.reshape(bp, sp, d), xb.reshape(bs, ss, d)
```

```python
import functools
import math

import jax
import jax.numpy as jnp
from jax import lax
from jax.experimental import pallas as pl
from jax.experimental.pallas import tpu as pltpu

TOP_K = 4
LRU_C = 8.0
SWIGLU_LIMIT = 7.0
SWIGLU_ALPHA = 1.702
EPS = 1e-6

LANES = 128
VMEM_LIMIT = 56 * 1024 * 1024
NEG_BIG = -1e30
ROW_BLOCK = 512
SUB_BLOCK = 128
WEIGHT_CHUNK_ROWS = 128
DMA_GROUP = 8

F32 = jnp.float32
BF16 = jnp.bfloat16
U32 = jnp.uint32


def _params(sem):
    return pltpu.CompilerParams(dimension_semantics=sem, vmem_limit_bytes=VMEM_LIMIT)


def _gelu(x):
    return 0.5 * x * (1.0 + jnp.tanh(0.7978845608028654 * (x + 0.044715 * (x * x * x))))


def _rms(x, g):
    return x * lax.rsqrt(jnp.mean(x * x, axis=-1, keepdims=True) + EPS) * g


def _pack_bf16_pair(lo, hi):
    lo_bits = lax.bitcast_convert_type(lo.astype(BF16).astype(F32), U32)
    hi_bits = lax.bitcast_convert_type(hi.astype(BF16).astype(F32), U32)
    return (lo_bits >> 16) | (hi_bits & jnp.uint32(0xFFFF0000))


def _unpack_lo(v):
    return lax.bitcast_convert_type(v << 16, F32)


def _unpack_hi(v):
    return lax.bitcast_convert_type(v & jnp.uint32(0xFFFF0000), F32)


def _pair_specs(tm, d, n_first):
    return [pl.BlockSpec((tm, d), lambda i: (jnp.minimum(i, n_first - 1), 0)),
            pl.BlockSpec((tm, d), lambda i: (jnp.maximum(i - n_first, 0), 0))]


def _pair_tile(xa_ref, xb_ref, n_first):
    return jnp.where(pl.program_id(0) < n_first, xa_ref[...], xb_ref[...])


def _inproj_kernel(xa_ref, xb_ref, g_ref, w_ref, z_ref, *, n_chunk, n_first):
    xn = _rms(_pair_tile(xa_ref, xb_ref, n_first), g_ref[...]).astype(BF16)
    for n in range(w_ref.shape[1] // n_chunk):
        cols = slice(n * n_chunk, (n + 1) * n_chunk)
        z_ref[:, cols] = jnp.dot(xn, w_ref[:, cols], preferred_element_type=F32).astype(z_ref.dtype)


def _inproj(xa, xb, g, w, tm):
    d = xa.shape[1]
    t = xa.shape[0] + xb.shape[0]
    d_in = w.shape[1]
    n_first = xa.shape[0] // tm
    return pl.pallas_call(
        functools.partial(_inproj_kernel, n_chunk=min(1024, d_in), n_first=n_first),
        grid=(t // tm,),
        in_specs=_pair_specs(tm, d, n_first) + [
            pl.BlockSpec((1, d), lambda i: (0, 0)),
            pl.BlockSpec((d, d_in), lambda i: (0, 0), pipeline_mode=pl.Buffered(1)),
        ],
        out_specs=pl.BlockSpec((tm, d_in), lambda i: (i, 0)),
        out_shape=jax.ShapeDtypeStruct((t, d_in), BF16),
        compiler_params=_params(("arbitrary",)),
        name="inproj",
    )(xa, xb, g, w)


def _gmlp_kernel(u_ref, v_ref, lng_ref, lnb_ref, ws_ref, bs_ref, ga_ref, o_ref, mix_ref, *, heads, chunk):
    v = _gelu(v_ref[...].astype(F32))
    mu = jnp.mean(v, axis=-1, keepdims=True)
    vc = v - mu
    var = jnp.mean(vc * vc, axis=-1, keepdims=True)
    vn = (vc * lax.rsqrt(var + EPS) * lng_ref[...] + lnb_ref[...]).astype(BF16)
    tm, dg = vn.shape
    hd = dg // heads
    for c in range(tm // chunk):
        rows = slice(c * chunk, (c + 1) * chunk)
        for h in range(heads):
            cols = slice(h * hd, (h + 1) * hd)
            mixed = jnp.dot(ws_ref[h], vn[rows, cols], preferred_element_type=F32) + bs_ref[h]
            mix_ref[rows, cols] = mixed
    out = _gelu(u_ref[...].astype(F32)) * mix_ref[...]
    o_ref[...] = _rms(out, ga_ref[...]).astype(o_ref.dtype)


def _gmlp(z, ln_g, ln_b, ws, bs_b, ga, tm):
    t = z.shape[0]
    heads, chunk, _ = ws.shape
    dg = ln_g.shape[1]
    vec = pl.BlockSpec((1, dg), lambda i: (0, 0))
    return pl.pallas_call(
        functools.partial(_gmlp_kernel, heads=heads, chunk=chunk),
        grid=(t // tm,),
        in_specs=[
            pl.BlockSpec((tm, dg), lambda i: (i, 0)),
            pl.BlockSpec((tm, dg), lambda i: (i, 1)),
            vec, vec,
            pl.BlockSpec(ws.shape, lambda i: (0, 0, 0)),
            pl.BlockSpec(bs_b.shape, lambda i: (0, 0, 0)),
            vec,
        ],
        out_specs=pl.BlockSpec((tm, dg), lambda i: (i, 0)),
        out_shape=jax.ShapeDtypeStruct((t, dg), BF16),
        scratch_shapes=[pltpu.VMEM((tm, dg), F32)],
        compiler_params=_params(("arbitrary",)),
        name="gmlp",
    )(z, z, ln_g, ln_b, ws, bs_b, ga)


def _rot8(x, s):
    n, w = x.shape
    return pltpu.roll(x.reshape(n // 8, 8, w), s, axis=1).reshape(n, w)


def _row_in_group(shape):
    return lax.broadcasted_iota(jnp.int32, shape, 0) & 7


def _shift_rows(x, s, fill, reverse, row):
    if reverse:
        return jnp.where(row >= 8 - s, fill, _rot8(x, 8 - s))
    return jnp.where(row < s, fill, _rot8(x, s))


def _lru_scan_chunk(a, b, h, reverse):
    row = _row_in_group(a.shape)
    for s in (1, 2, 4):
        b = a * _shift_rows(b, s, 0.0, reverse, row) + b
        a = a * _shift_rows(a, s, 1.0, reverse, row)
    n_grp = a.shape[0] // 8
    out = [None] * n_grp
    order = range(n_grp - 1, -1, -1) if reverse else range(n_grp)
    edge = 0 if reverse else 7
    for gi in order:
        rows = slice(gi * 8, gi * 8 + 8)
        hg = b[rows] + a[rows] * h
        out[gi] = hg
        h = hg[edge:edge + 1]
    return jnp.concatenate(out, axis=0), h


def _rglru_kernel(seqlen_ref, xr_ref, gr_ref, cw_ref, cb_ref, wri_ref, bri_ref, lam_ref, o_ref, hf_ref, xc_ref,
                  *, tc):
    rows, hd = xr_ref.shape
    n_ck = rows // tc
    halo = 16
    cw = cw_ref[...]
    cb = cb_ref[...]
    seq = seqlen_ref[pl.program_id(0)]

    def starts_seq(c):
        return lax.rem(c * tc, seq) == 0

    def ends_seq(c):
        return lax.rem((c + 1) * tc, seq) == 0

    def conv_chunk(c):
        t0 = pl.multiple_of(c * tc, tc)
        main = xr_ref[pl.ds(t0, tc), :].astype(F32)
        p0 = pl.multiple_of(jnp.maximum(t0 - halo, 0), halo)
        n0 = pl.multiple_of(jnp.minimum(t0 + tc, rows - halo), halo)
        prev = jnp.where(starts_seq(c), 0.0, xr_ref[pl.ds(p0, halo), :].astype(F32))
        nxt = jnp.where(ends_seq(c), 0.0, xr_ref[pl.ds(n0, halo), :].astype(F32))
        ext = jnp.concatenate([prev, main, nxt], axis=0)
        row = _row_in_group(main.shape)

        def earlier(s):
            r = _rot8(ext, s)
            return jnp.where(row < s, r[halo - 8:halo - 8 + tc], r[halo:halo + tc])

        r7 = _rot8(ext, 7)
        later = jnp.where(row >= 7, r7[halo + 8:halo + 8 + tc], r7[halo:halo + tc])
        acc = cb + cw[2:3] * main
        acc = acc + cw[0:1] * earlier(2)
        acc = acc + cw[1:2] * earlier(1)
        acc = acc + cw[3:4] * later
        return acc

    def direction(d, xc, h):
        g = jnp.tanh(jnp.dot(xc.astype(BF16), wri_ref[d], preferred_element_type=F32) + bri_ref[d])
        i = 0.5 * g[:, hd:] + 0.5
        nl = -lam_ref[d]
        softplus = jnp.maximum(nl, 0.0) + jnp.log1p(jnp.exp(-jnp.abs(nl)))
        half_c = (-0.5 * LRU_C) * softplus
        log_a = half_c * g[:, :hd] + half_c
        a = jnp.exp(log_a)
        b = jnp.sqrt(-jnp.tanh(log_a) * (a * a + 1.0)) * i * xc
        return _lru_scan_chunk(a, b, h, reverse=(d == 1))

    def fwd(c, h):
        t0 = pl.multiple_of(c * tc, tc)
        xc = conv_chunk(c)
        xc_ref[pl.ds(t0, tc), :] = xc
        hs, h = direction(0, xc, jnp.where(starts_seq(c), 0.0, h))
        hf_ref[pl.ds(t0, tc), :] = hs
        return h

    def bwd(k, h):
        c = n_ck - 1 - k
        t0 = pl.multiple_of(c * tc, tc)
        hs, h = direction(1, xc_ref[pl.ds(t0, tc), :], jnp.where(ends_seq(c), 0.0, h))
        tot = hf_ref[pl.ds(t0, tc), :] + hs
        o_ref[pl.ds(t0, tc), :] = (_gelu(gr_ref[pl.ds(t0, tc), :].astype(F32)) * tot).astype(o_ref.dtype)
        return h

    h0 = jnp.zeros((1, hd), F32)
    lax.fori_loop(0, n_ck, fwd, h0)
    lax.fori_loop(0, n_ck, bwd, h0)


def _rglru(z, unit_seq, unit, col_x, col_g, cw, cb, wri, bri, lam, tc):
    t = z.shape[0]
    _, heads, hd, _ = wri.shape
    grid_spec = pltpu.PrefetchScalarGridSpec(
        num_scalar_prefetch=1,
        grid=(t // unit, heads),
        in_specs=[
            pl.BlockSpec((unit, hd), lambda u, h, sl: (u, col_x + h)),
            pl.BlockSpec((unit, hd), lambda u, h, sl: (u, col_g + h)),
            pl.BlockSpec((cw.shape[0], hd), lambda u, h, sl: (0, h)),
            pl.BlockSpec((1, hd), lambda u, h, sl: (0, h)),
            pl.BlockSpec((2, None, hd, 2 * hd), lambda u, h, sl: (0, h, 0, 0)),
            pl.BlockSpec((2, None, 1, 2 * hd), lambda u, h, sl: (0, h, 0, 0)),
            pl.BlockSpec((2, None, 1, hd), lambda u, h, sl: (0, h, 0, 0)),
        ],
        out_specs=pl.BlockSpec((unit, hd), lambda u, h, sl: (u, h)),
        scratch_shapes=[pltpu.VMEM((unit, hd), F32), pltpu.VMEM((unit, hd), F32)],
    )
    return pl.pallas_call(
        functools.partial(_rglru_kernel, tc=tc),
        grid_spec=grid_spec,
        out_shape=jax.ShapeDtypeStruct((t, heads * hd), BF16),
        compiler_params=_params(("arbitrary", "arbitrary")),
        name="rglru",
    )(unit_seq, z, z, cw, cb, wri, bri, lam)


def _outproj_kernel(xa_ref, xb_ref, a_ref, b_ref, gb_ref, wo_ref, gf_ref, rw_ref, rb_ref,
                    x1_ref, hp_ref, rt_ref, rg_ref, cnt_ref, carry_ref, *, n_first):
    step = pl.program_id(0)

    @pl.when(step == 0)
    def _():
        carry_ref[...] = jnp.zeros_like(carry_ref)

    da = a_ref.shape[1]
    obn = _rms(b_ref[...].astype(F32), gb_ref[...]).astype(BF16)
    y = jnp.dot(a_ref[...], wo_ref[:da], preferred_element_type=F32)
    y = y + jnp.dot(obn, wo_ref[da:], preferred_element_type=F32)
    x1 = _pair_tile(xa_ref, xb_ref, n_first) + y
    x1_ref[...] = x1
    hf = _rms(x1, gf_ref[...])
    half = hf.shape[1] // 2
    hp_ref[...] = _pack_bf16_pair(hf[:, :half], hf[:, half:])

    logits = jnp.dot(hf.astype(BF16), rw_ref[...], preferred_element_type=F32) + rb_ref[...]
    tm = logits.shape[0]
    lane = lax.broadcasted_iota(jnp.int32, logits.shape, 1)
    lane_f = lane.astype(F32)
    work = logits
    sels, vals = [], []
    for _ in range(TOP_K):
        m = jnp.max(work, axis=-1, keepdims=True)
        sel = jnp.min(jnp.where(work == m, lane_f, float(LANES)), axis=-1, keepdims=True)
        hit = lane_f == sel
        sels.append(hit)
        vals.append(m)
        work = jnp.where(hit, -jnp.inf, work)
    exps = [jnp.exp(v - vals[0]) for v in vals]
    denom = exps[0]
    for e in exps[1:]:
        denom = denom + e

    onehot = sels[0]
    for hit in sels[1:]:
        onehot = onehot | hit
    cnt = onehot.astype(F32)
    rr = lax.broadcasted_iota(jnp.int32, (tm, tm), 0)
    cc = lax.broadcasted_iota(jnp.int32, (tm, tm), 1)
    tril = (rr > cc).astype(BF16)
    before = jnp.dot(tril, cnt.astype(BF16), preferred_element_type=F32) + carry_ref[...]

    ri = jnp.zeros(logits.shape, F32)
    rg = jnp.zeros(logits.shape, F32)
    for k in range(TOP_K):
        e_k = jnp.sum(jnp.where(sels[k], lane_f, 0.0), axis=-1, keepdims=True)
        rank_k = jnp.sum(jnp.where(sels[k], before, 0.0), axis=-1, keepdims=True)
        ri = jnp.where(lane == k, e_k, ri)
        ri = jnp.where(lane == TOP_K + k, rank_k, ri)
        rg = jnp.where(lane == k, exps[k] / denom, rg)
    rt_ref[...] = ri.T[:2 * TOP_K, :].astype(jnp.int32)
    rg_ref[...] = rg
    carry_ref[...] = carry_ref[...] + jnp.sum(cnt, axis=0, keepdims=True)
    cnt_ref[...] = carry_ref[...].astype(jnp.int32)


def _outproj(xa, xb, out_a, out_b, gb, wo, gf, rw, rb, tm):
    d = xa.shape[1]
    t = xa.shape[0] + xb.shape[0]
    n_first = xa.shape[0] // tm
    da = out_a.shape[1]
    db = out_b.shape[1]
    row = lambda i: (i, 0)
    fixed = lambda i: (0, 0)
    return pl.pallas_call(
        functools.partial(_outproj_kernel, n_first=n_first),
        grid=(t // tm,),
        in_specs=_pair_specs(tm, d, n_first) + [
            pl.BlockSpec((tm, da), row),
            pl.BlockSpec((tm, db), row),
            pl.BlockSpec((1, db), fixed),
            pl.BlockSpec(wo.shape, fixed, pipeline_mode=pl.Buffered(1)),
            pl.BlockSpec((1, d), fixed),
            pl.BlockSpec(rw.shape, fixed),
            pl.BlockSpec((1, LANES), fixed),
        ],
        out_specs=[
            pl.BlockSpec((tm, d), row),
            pl.BlockSpec((tm, d // 2), row),
            pl.BlockSpec((2 * TOP_K, tm), lambda i: (0, i)),
            pl.BlockSpec((tm, LANES), row),
            pl.BlockSpec((1, LANES), fixed),
        ],
        out_shape=[
            jax.ShapeDtypeStruct((t, d), F32),
            jax.ShapeDtypeStruct((t, d // 2), U32),
            jax.ShapeDtypeStruct((2 * TOP_K, t), jnp.int32),
            jax.ShapeDtypeStruct((t, LANES), F32),
            jax.ShapeDtypeStruct((1, LANES), jnp.int32),
        ],
        scratch_shapes=[pltpu.VMEM((1, LANES), F32)],
        compiler_params=_params(("arbitrary",)),
        name="outproj_router",
    )(xa, xb, out_a, out_b, gb, wo, gf, rw, rb)


def _tile_slots(dest_kt, tm):
    k, t = dest_kt.shape
    return dest_kt.reshape(k, t // tm, tm).transpose(1, 0, 2).reshape(t // tm, 1, k * tm)


def _dispatch_kernel(zneed_ref, dest_ref, hp_ref, xs_ref, zero_ref, sem, zsem, *, sub, group):
    step = pl.program_id(0)
    tm = hp_ref.shape[0]
    n_zc = zneed_ref.shape[0]

    def zero_copy(j):
        return pltpu.make_async_copy(zero_ref, xs_ref.at[pl.ds(pl.multiple_of(j * sub, sub), sub), :], zsem)

    @pl.when(step == 0)
    def _():
        zero_ref[...] = jnp.zeros_like(zero_ref)

        def start(j, c):
            @pl.when(zneed_ref[j] != 0)
            def _():
                zero_copy(j).start()
            return c

        def wait(j, c):
            @pl.when(zneed_ref[j] != 0)
            def _():
                zero_copy(j).wait()
            return c

        lax.fori_loop(0, n_zc, start, 0)
        lax.fori_loop(0, n_zc, wait, 0)

    def issue(g, c):
        base = g * group
        for r in range(group):
            for k in range(TOP_K):
                d = dest_ref[0, 0, k * tm + base + r]
                pltpu.make_async_copy(hp_ref.at[pl.ds(base + r, 1), :], xs_ref.at[pl.ds(d, 1), :],
                                      sem).start(priority=k % 2)
        return c

    lax.fori_loop(0, tm // group, issue, 0)
    for _ in range(TOP_K):
        pltpu.make_async_copy(hp_ref, xs_ref.at[pl.ds(0, tm), :], sem).wait()


def _dispatch(zneed, dest_kt, hp, n_slots, sub, tm):
    t, half = hp.shape
    grid_spec = pltpu.PrefetchScalarGridSpec(
        num_scalar_prefetch=1,
        grid=(t // tm,),
        in_specs=[
            pl.BlockSpec((1, 1, tm * TOP_K), lambda i, *_: (i, 0, 0), memory_space=pltpu.SMEM),
            pl.BlockSpec((tm, half), lambda i, *_: (i, 0)),
        ],
        out_specs=pl.BlockSpec(memory_space=pl.ANY),
        scratch_shapes=[pltpu.VMEM((sub, half), U32), pltpu.SemaphoreType.DMA, pltpu.SemaphoreType.DMA],
    )
    return pl.pallas_call(
        functools.partial(_dispatch_kernel, sub=sub, group=min(DMA_GROUP, tm)),
        grid_spec=grid_spec,
        out_shape=jax.ShapeDtypeStruct((n_slots, half), U32),
        compiler_params=_params(("arbitrary",)),
        name="dispatch",
    )(zneed, _tile_slots(dest_kt, tm), hp)


def _gmm_kernel(plan_ref, x_ref, bias_ref, w_hbm, o_ref, wbuf, stage, sem, *rest, mode, kc, sub, nw):
    blk = pl.program_id(0)
    expert = plan_ref[0, blk]
    half_idx = plan_ref[1, blk]
    nxt = plan_ref[3, blk]
    valid = blk < plan_ref[7, 0]
    n_sub = jnp.where(valid, plan_ref[6, blk], 0)
    n_ch = w_hbm.shape[1] // kc

    def chunk_rows(k):
        return pl.ds(pl.multiple_of(k * kc, kc), kc)

    def chunk_copy(e, k, st):
        return pltpu.make_async_copy(w_hbm.at[e, chunk_rows(k), :], stage.at[st], sem.at[st])

    def land(e, k, dst):
        st = k % 2
        chunk_copy(e, k, st).wait()
        wbuf[dst, chunk_rows(k), :] = stage[st].astype(BF16)

        @pl.when(k + 2 < n_ch)
        def _():
            chunk_copy(e, k + 2, st).start()

    @pl.when(blk == 0)
    def _():
        chunk_copy(expert, 0, 0).start()
        chunk_copy(expert, 1, 1).start()
        lax.fori_loop(0, n_ch, lambda k, c: (land(expert, k, half_idx), c)[1], 0)

    @pl.when(valid & (plan_ref[2, blk] == 1) & (nxt >= 0))
    def _():
        chunk_copy(nxt, 0, 0).start()
        chunk_copy(nxt, 1, 1).start()

    if mode == "gate_up":
        xb_ref, = rest
        half = x_ref.shape[1]
        de = o_ref.shape[1]

        def run(rows):
            v = x_ref[rows, :]
            xb_ref[rows, :half] = _unpack_lo(v).astype(BF16)
            xb_ref[rows, half:] = _unpack_hi(v).astype(BF16)
            x = xb_ref[rows, :]
            for n in range(de // nw):
                cg = slice(n * nw, (n + 1) * nw)
                cu = slice(de + n * nw, de + (n + 1) * nw)
                gate = jnp.dot(x, wbuf[half_idx, :, cg], preferred_element_type=F32) + bias_ref[:, cg]
                up = jnp.dot(x, wbuf[half_idx, :, cu], preferred_element_type=F32) + bias_ref[:, cu]
                gate = jnp.minimum(gate, SWIGLU_LIMIT)
                up = jnp.clip(up, -SWIGLU_LIMIT, SWIGLU_LIMIT)
                act = (up + 1.0) * (gate * jax.nn.sigmoid(SWIGLU_ALPHA * gate))
                o_ref[rows, cg] = act.astype(o_ref.dtype)
    else:
        half = o_ref.shape[1]

        def run(rows):
            x = x_ref[rows, :]
            for n in range(half // nw):
                cl = slice(n * nw, (n + 1) * nw)
                ch = slice(half + n * nw, half + (n + 1) * nw)
                lo = jnp.dot(x, wbuf[half_idx, :, cl], preferred_element_type=F32) + bias_ref[:, cl]
                hi = jnp.dot(x, wbuf[half_idx, :, ch], preferred_element_type=F32) + bias_ref[:, ch]
                o_ref[rows, cl] = _pack_bf16_pair(lo, hi)

    full = x_ref.shape[0] // sub

    @pl.when(n_sub < full)
    def _():
        o_ref[...] = jnp.zeros_like(o_ref)

    @pl.when(n_sub == full)
    def _():
        run(slice(None))

    @pl.when((n_sub > 0) & (n_sub < full))
    def _():
        def body(i, c):
            run(pl.ds(pl.multiple_of(i * sub, sub), sub))
            return c
        lax.fori_loop(0, n_sub, body, 0)

    lax.fori_loop(plan_ref[4, blk], plan_ref[5, blk], lambda k, c: (land(nxt, k, 1 - half_idx), c)[1], 0)


def _gmm(plan, x, bias, w, mode, bm, sub, kc):
    n_slots = x.shape[0]
    n_exp, kdim, ndim = w.shape
    n_blk = n_slots // bm
    if mode == "gate_up":
        out_cols, out_dtype = ndim // 2, BF16
        extra = [pltpu.VMEM((bm, kdim), BF16)]
    else:
        out_cols, out_dtype = ndim // 2, U32
        extra = []

    def live(b, plan):
        return (jnp.minimum(b, plan[7, 0] - 1), 0)

    grid_spec = pltpu.PrefetchScalarGridSpec(
        num_scalar_prefetch=1,
        grid=(n_blk,),
        in_specs=[
            pl.BlockSpec((bm, x.shape[1]), live),
            pl.BlockSpec((None, 1, ndim), lambda b, plan: (plan[0, b], 0, 0)),
            pl.BlockSpec(memory_space=pl.ANY),
        ],
        out_specs=pl.BlockSpec((bm, out_cols), lambda b, plan: (b, 0)),
        scratch_shapes=[
            pltpu.VMEM((2, kdim, ndim), BF16),
            pltpu.VMEM((2, kc, ndim), F32),
            pltpu.SemaphoreType.DMA((2,)),
        ] + extra,
    )
    return pl.pallas_call(
        functools.partial(_gmm_kernel, mode=mode, kc=kc, sub=sub, nw=min(512, out_cols)),
        grid_spec=grid_spec,
        out_shape=jax.ShapeDtypeStruct((n_slots, out_cols), out_dtype),
        compiler_params=_params(("arbitrary",)),
        name="experts_" + mode,
    )(plan, x, bias, w)


def _expert_plan(counts, bm, sub, n_blk, n_ch):
    n_exp = counts.shape[0]
    nb_e = (counts + bm - 1) // bm
    blk_end = jnp.cumsum(nb_e)
    blk_start = blk_end - nb_e
    n_valid = blk_end[-1]
    blk = jnp.arange(n_blk, dtype=jnp.int32)
    last = jnp.minimum(blk, n_valid - 1)
    be = jnp.minimum(jnp.sum(blk_end[None, :] <= last[:, None], axis=1), n_exp - 1).astype(jnp.int32)
    valid = blk < n_valid
    i_in = last - jnp.take(blk_start, be)
    nb_b = jnp.take(nb_e, be)
    nonempty = nb_e > 0
    ordinal = jnp.cumsum(nonempty.astype(jnp.int32)) - 1
    idx = jnp.where(nonempty, jnp.arange(n_exp, dtype=jnp.int32), n_exp)
    nxt_incl = lax.cummin(idx, axis=0, reverse=True)
    nxt_e = jnp.concatenate([nxt_incl[1:], jnp.full((1,), n_exp, jnp.int32)])
    nxt_e = jnp.where(nxt_e >= n_exp, -1, nxt_e)
    nxt_b = jnp.take(nxt_e, be)
    has_next = valid & (nxt_b >= 0)
    lo = jnp.where(has_next, i_in * n_ch // nb_b, 0)
    hi = jnp.where(has_next, (i_in + 1) * n_ch // nb_b, 0)
    rows = jnp.where(valid, jnp.clip(jnp.take(counts, be) - i_in * bm, 0, bm), 0)
    nsub = (rows + sub - 1) // sub
    plan = jnp.stack([
        be, jnp.take(ordinal, be) % 2, (valid & (i_in == 0)).astype(jnp.int32), nxt_b, lo, hi, nsub,
        jnp.full((n_blk,), n_valid, jnp.int32),
    ]).astype(jnp.int32)
    per = bm // sub
    sub_rows = rows[:, None] - jnp.arange(per, dtype=jnp.int32)[None, :] * sub
    zneed = (sub_rows < sub).astype(jnp.int32).reshape(n_blk * per)
    return blk_start * bm, plan, zneed


def _combine_kernel(dcur_ref, dnxt_ref, x1_ref, rg_ref, gfin_ref, os_ref, ya_ref, yb_ref, gbuf_ref, sem,
                    *, final_norm, n_first, cw):
    step = pl.program_id(0)
    last = pl.num_programs(0) - 1
    tm = x1_ref.shape[0]
    half = gbuf_ref.shape[3]
    cur = step % 2
    n_cc = half // cw

    def token_copies(dref, r, buf):
        for k in range(TOP_K):
            d = dref[0, 0, k * tm + r]
            pltpu.make_async_copy(os_ref.at[pl.ds(d, 1), :], gbuf_ref.at[buf, k, pl.ds(r, 1), :],
                                  sem.at[buf]).start(priority=k % 2)

    def wait_tile(buf):
        for k in range(TOP_K):
            pltpu.make_async_copy(os_ref.at[pl.ds(0, tm), :], gbuf_ref.at[buf, k], sem.at[buf]).wait()

    @pl.when(step == 0)
    def _():
        def body(g, c):
            for r in range(DMA_GROUP):
                token_copies(dcur_ref, g * DMA_GROUP + r, 0)
            return c
        lax.fori_loop(0, tm // DMA_GROUP, body, 0)

    wait_tile(cur)

    def finish(y_ref):
        rg = rg_ref[...]
        ss = jnp.zeros((tm, 1), F32)
        share = tm // n_cc
        for c in range(n_cc):
            cl = slice(c * cw, (c + 1) * cw)
            ch = slice(half + c * cw, half + (c + 1) * cw)
            lo = x1_ref[:, cl]
            hi = x1_ref[:, ch]
            for k in range(TOP_K):
                g = rg[:, k:k + 1]
                v = gbuf_ref[cur, k, :, cl]
                lo = lo + g * _unpack_lo(v)
                hi = hi + g * _unpack_hi(v)
            if final_norm:
                ss = ss + jnp.sum(lo * lo, axis=-1, keepdims=True) + jnp.sum(hi * hi, axis=-1, keepdims=True)
            y_ref[:, cl] = lo
            y_ref[:, ch] = hi
            for r in range(c * share, (c + 1) * share):
                token_copies(dnxt_ref, r, 1 - cur)
        if final_norm:
            y_ref[...] = y_ref[...] * lax.rsqrt(ss / (2 * half) + EPS) * gfin_ref[...]

    @pl.when(step < n_first)
    def _():
        finish(ya_ref)

    @pl.when(step >= n_first)
    def _():
        finish(yb_ref)

    @pl.when(step == last)
    def _():
        wait_tile(1 - cur)


def _combine(dest_kt, x1, rg, gfin, os_, t_first, tm, final_norm):
    t, d = x1.shape
    half = d // 2
    n_tiles = t // tm
    n_first = t_first // tm
    dest3 = _tile_slots(dest_kt, tm)
    return pl.pallas_call(
        functools.partial(_combine_kernel, final_norm=final_norm, n_first=n_first, cw=min(256, half)),
        grid=(n_tiles,),
        in_specs=[
            pl.BlockSpec((1, 1, tm * TOP_K), lambda i: (i, 0, 0), memory_space=pltpu.SMEM),
            pl.BlockSpec((1, 1, tm * TOP_K), lambda i: (jnp.minimum(i + 1, n_tiles - 1), 0, 0),
                         memory_space=pltpu.SMEM),
            pl.BlockSpec((tm, d), lambda i: (i, 0)),
            pl.BlockSpec((tm, LANES), lambda i: (i, 0)),
            pl.BlockSpec((1, d), lambda i: (0, 0)),
            pl.BlockSpec(memory_space=pl.ANY),
        ],
        out_specs=_pair_specs(tm, d, n_first),
        out_shape=[jax.ShapeDtypeStruct((t_first, d), F32), jax.ShapeDtypeStruct((t - t_first, d), F32)],
        scratch_shapes=[pltpu.VMEM((2, TOP_K, tm, half), U32), pltpu.SemaphoreType.DMA((2,))],
        compiler_params=_params(("arbitrary",)),
        name="combine",
    )(dest3, dest3, x1, rg, gfin, os_)


def _tile(n, pref):
    while n % pref:
        pref //= 2
    return pref


def _layer(xa, xb, seq_a, seq_b, p, final_g):
    ta, d = xa.shape
    t = ta + xb.shape[0]
    dg = p["gmlp_ln_g"].shape[-1]
    chunk = p["gmlp_ws"].shape[-1]
    d_lru = p["conv_w"].shape[-1]
    heads_l, hd_l = p["lru_wr"].shape[1], p["lru_wr"].shape[2]
    n_exp = p["router_w"].shape[-1]
    assert p["w_down"].shape[1] == d, "expert width must equal the model width (shared weight-chunk plan)"

    unit = max(seq_a, seq_b)
    assert unit % seq_a == 0 and unit % seq_b == 0 and ta % unit == 0 and (t - ta) % unit == 0
    tm = _tile(math.gcd(ta, t - ta), 512)
    z = _inproj(xa, xb, p["mix_norm_g"][None], p["w_in"].astype(BF16), tm)

    bs_b = jnp.broadcast_to(p["gmlp_bs"][:, :, None], p["gmlp_ws"].shape).astype(F32)
    out_a = _gmlp(z, p["gmlp_ln_g"][None], p["gmlp_ln_b"][None], p["gmlp_ws"].astype(BF16), bs_b,
                  p["out_norm_a"][None], max(tm, chunk))

    wri = (0.5 * jnp.concatenate([p["lru_wr"], p["lru_wi"]], axis=-1)).astype(BF16)
    bri = 0.5 * jnp.concatenate([p["lru_br"].reshape(2, heads_l, 1, hd_l),
                                 p["lru_bi"].reshape(2, heads_l, 1, hd_l)], axis=-1)
    lam = p["lru_lam"].reshape(2, heads_l, 1, hd_l)
    unit_seq = jnp.asarray([seq_a] * (ta // unit) + [seq_b] * ((t - ta) // unit), jnp.int32)
    out_b = _rglru(z, unit_seq, unit, 2 * dg // hd_l, (2 * dg + d_lru) // hd_l,
                   p["conv_w"], p["conv_b"][None], wri, bri, lam, _tile(math.gcd(seq_a, seq_b), 512))

    rw = jnp.zeros((d, LANES), BF16).at[:, :n_exp].set(p["router_w"].astype(BF16))
    rb = jnp.full((1, LANES), NEG_BIG, F32).at[0, :n_exp].set(p["router_b"])
    x1, hp, rt, rg, cnt = _outproj(xa, xb, out_a, out_b, p["out_norm_b"][None], p["w_out"].astype(BF16),
                                   p["ffn_norm_g"][None], rw, rb, tm)

    m = t * TOP_K
    bm = _tile(m, ROW_BLOCK)
    sub = min(SUB_BLOCK, bm)
    kc = _tile(d, WEIGHT_CHUNK_ROWS)
    counts = cnt[0, :n_exp]
    n_blk = m // bm + n_exp
    n_slots = n_blk * bm
    pad_start, plan, zneed = _expert_plan(counts, bm, sub, n_blk, d // kc)
    top_e = rt[:TOP_K]
    base = jnp.sum(jnp.where(top_e[..., None] == jnp.arange(n_exp, dtype=jnp.int32), pad_start, 0), axis=-1)
    dest_kt = base + rt[TOP_K:]

    xs = _dispatch(zneed, dest_kt, hp, n_slots, sub, _tile(t, 512))
    act = _gmm(plan, xs, p["b_gate_up"][:, None, :], p["w_gate_up"], "gate_up", bm, sub, kc)
    os_ = _gmm(plan, act, p["b_down"][:, None, :], p["w_down"], "down", bm, sub, kc)
    gfin = jnp.ones((1, d), F32) if final_g is None else final_g[None]
    return _combine(dest_kt, x1, rg, gfin, os_, ta, _tile(math.gcd(ta, t - ta), 256), final_g is not None)


_LAYER_KEYS = ("mix_norm_g", "w_in", "gmlp_ln_g", "gmlp_ln_b", "gmlp_ws", "gmlp_bs", "conv_w", "conv_b",
               "lru_wr", "lru_br", "lru_wi", "lru_bi", "lru_lam", "out_norm_a", "out_norm_b", "w_out",
               "ffn_norm_g", "router_w", "router_b", "w_gate_up", "b_gate_up", "w_down", "b_down")


def kernel(x_prompt, x_sample, mix_norm_g, w_in, gmlp_ln_g, gmlp_ln_b, gmlp_ws, gmlp_bs, conv_w, conv_b, lru_wr, lru_br, lru_wi, lru_bi, lru_lam, out_norm_a, out_norm_b, w_out, ffn_norm_g, router_w, router_b, w_gate_up, b_gate_up, w_down, b_down, final_norm_g):
    stacked = dict(zip(_LAYER_KEYS, (mix_norm_g, w_in, gmlp_ln_g, gmlp_ln_b, gmlp_ws, gmlp_bs, conv_w, conv_b,
                                     lru_wr, lru_br, lru_wi, lru_bi, lru_lam, out_norm_a, out_norm_b, w_out,
                                     ffn_norm_g, router_w, router_b, w_gate_up, b_gate_up, w_down, b_down)))
    depth = w_in.shape[0]
    bp, sp, d = x_prompt.shape
    bs, ss, _ = x_sample.shape
    xa = x_prompt.reshape(bp * sp, d)
    xb = x_sample.reshape(bs * ss, d)
    for layer in range(depth):
        p = {k: v[layer] for k, v in stacked.items()}
        xa, xb = _layer(xa, xb, sp, ss, p, final_norm_g if layer == depth - 1 else None)
    return xa.reshape(bp, sp, d), xb.reshape(bs, ss, d)
```

```python
import functools
import math

import jax
import jax.numpy as jnp
from jax import lax
from jax.experimental import pallas as pl
from jax.experimental.pallas import tpu as pltpu

TOP_K = 4
LRU_C = 8.0
SWIGLU_LIMIT = 7.0
SWIGLU_ALPHA = 1.702
EPS = 1e-6

LANES = 128
VMEM_LIMIT = 56 * 1024 * 1024
NEG_BIG = -1e30
ROW_BLOCK = 512
SUB_BLOCK = 128
WEIGHT_CHUNK_ROWS = 128
DMA_GROUP = 8

F32 = jnp.float32
BF16 = jnp.bfloat16
U32 = jnp.uint32


def _params(sem):
    return pltpu.CompilerParams(dimension_semantics=sem, vmem_limit_bytes=VMEM_LIMIT)


def _gelu(x):
    return 0.5 * x * (1.0 + jnp.tanh(0.7978845608028654 * (x + 0.044715 * (x * x * x))))


def _rms(x, g):
    return x * lax.rsqrt(jnp.mean(x * x, axis=-1, keepdims=True) + EPS) * g


def _pack_bf16_pair(lo, hi):
    lo_bits = lax.bitcast_convert_type(lo.astype(BF16).astype(F32), U32)
    hi_bits = lax.bitcast_convert_type(hi.astype(BF16).astype(F32), U32)
    return (lo_bits >> 16) | (hi_bits & jnp.uint32(0xFFFF0000))


def _unpack_lo(v):
    return lax.bitcast_convert_type(v << 16, F32)


def _unpack_hi(v):
    return lax.bitcast_convert_type(v & jnp.uint32(0xFFFF0000), F32)


def _pair_specs(tm, d, n_first):
    return [pl.BlockSpec((tm, d), lambda i: (jnp.minimum(i, n_first - 1), 0)),
            pl.BlockSpec((tm, d), lambda i: (jnp.maximum(i - n_first, 0), 0))]


def _pair_tile(xa_ref, xb_ref, n_first):
    return jnp.where(pl.program_id(0) < n_first, xa_ref[...], xb_ref[...])


def _inproj_kernel(xa_ref, xb_ref, g_ref, w_ref, z_ref, *, n_chunk, n_first):
    xn = _rms(_pair_tile(xa_ref, xb_ref, n_first), g_ref[...]).astype(BF16)
    for n in range(w_ref.shape[1] // n_chunk):
        cols = slice(n * n_chunk, (n + 1) * n_chunk)
        z_ref[:, cols] = jnp.dot(xn, w_ref[:, cols], preferred_element_type=F32).astype(z_ref.dtype)


def _inproj(xa, xb, g, w, tm):
    d = xa.shape[1]
    t = xa.shape[0] + xb.shape[0]
    d_in = w.shape[1]
    n_first = xa.shape[0] // tm
    return pl.pallas_call(
        functools.partial(_inproj_kernel, n_chunk=min(1024, d_in), n_first=n_first),
        grid=(t // tm,),
        in_specs=_pair_specs(tm, d, n_first) + [
            pl.BlockSpec((1, d), lambda i: (0, 0)),
            pl.BlockSpec((d, d_in), lambda i: (0, 0), pipeline_mode=pl.Buffered(1)),
        ],
        out_specs=pl.BlockSpec((tm, d_in), lambda i: (i, 0)),
        out_shape=jax.ShapeDtypeStruct((t, d_in), BF16),
        compiler_params=_params(("arbitrary",)),
        name="inproj",
    )(xa, xb, g, w)


def _gmlp_kernel(u_ref, v_ref, lng_ref, lnb_ref, ws_ref, bs_ref, ga_ref, o_ref, mix_ref, *, heads, chunk):
    v = _gelu(v_ref[...].astype(F32))
    mu = jnp.mean(v, axis=-1, keepdims=True)
    vc = v - mu
    var = jnp.mean(vc * vc, axis=-1, keepdims=True)
    vn = (vc * lax.rsqrt(var + EPS) * lng_ref[...] + lnb_ref[...]).astype(BF16)
    tm, dg = vn.shape
    hd = dg // heads
    for c in range(tm // chunk):
        rows = slice(c * chunk, (c + 1) * chunk)
        for h in range(heads):
            cols = slice(h * hd, (h + 1) * hd)
            mixed = jnp.dot(ws_ref[h], vn[rows, cols], preferred_element_type=F32) + bs_ref[h]
            mix_ref[rows, cols] = mixed
    out = _gelu(u_ref[...].astype(F32)) * mix_ref[...]
    o_ref[...] = _rms(out, ga_ref[...]).astype(o_ref.dtype)


def _gmlp(z, ln_g, ln_b, ws, bs_b, ga, tm):
    t = z.shape[0]
    heads, chunk, _ = ws.shape
    dg = ln_g.shape[1]
    vec = pl.BlockSpec((1, dg), lambda i: (0, 0))
    return pl.pallas_call(
        functools.partial(_gmlp_kernel, heads=heads, chunk=chunk),
        grid=(t // tm,),
        in_specs=[
            pl.BlockSpec((tm, dg), lambda i: (i, 0)),
            pl.BlockSpec((tm, dg), lambda i: (i, 1)),
            vec, vec,
            pl.BlockSpec(ws.shape, lambda i: (0, 0, 0)),
            pl.BlockSpec(bs_b.shape, lambda i: (0, 0, 0)),
            vec,
        ],
        out_specs=pl.BlockSpec((tm, dg), lambda i: (i, 0)),
        out_shape=jax.ShapeDtypeStruct((t, dg), BF16),
        scratch_shapes=[pltpu.VMEM((tm, dg), F32)],
        compiler_params=_params(("arbitrary",)),
        name="gmlp",
    )(z, z, ln_g, ln_b, ws, bs_b, ga)


def _rot8(x, s):
    n, w = x.shape
    return pltpu.roll(x.reshape(n // 8, 8, w), s, axis=1).reshape(n, w)


def _row_in_group(shape):
    return lax.broadcasted_iota(jnp.int32, shape, 0) & 7


def _shift_rows(x, s, fill, reverse, row):
    if reverse:
        return jnp.where(row >= 8 - s, fill, _rot8(x, 8 - s))
    return jnp.where(row < s, fill, _rot8(x, s))


def _lru_scan_chunk(a, b, h, reverse):
    row = _row_in_group(a.shape)
    for s in (1, 2, 4):
        b = a * _shift_rows(b, s, 0.0, reverse, row) + b
        a = a * _shift_rows(a, s, 1.0, reverse, row)
    n_grp = a.shape[0] // 8
    out = [None] * n_grp
    order = range(n_grp - 1, -1, -1) if reverse else range(n_grp)
    edge = 0 if reverse else 7
    for gi in order:
        rows = slice(gi * 8, gi * 8 + 8)
        hg = b[rows] + a[rows] * h
        out[gi] = hg
        h = hg[edge:edge + 1]
    return jnp.concatenate(out, axis=0), h


def _rglru_kernel(seqlen_ref, xr_ref, gr_ref, cw_ref, cb_ref, wri_ref, bri_ref, lam_ref, o_ref, hf_ref, xc_ref,
                  *, tc):
    rows, hd = xr_ref.shape
    n_ck = rows // tc
    halo = 16
    cw = cw_ref[...]
    cb = cb_ref[...]
    seq = seqlen_ref[pl.program_id(0)]

    def starts_seq(c):
        return lax.rem(c * tc, seq) == 0

    def ends_seq(c):
        return lax.rem((c + 1) * tc, seq) == 0

    def conv_chunk(c):
        t0 = pl.multiple_of(c * tc, tc)
        main = xr_ref[pl.ds(t0, tc), :].astype(F32)
        p0 = pl.multiple_of(jnp.maximum(t0 - halo, 0), halo)
        n0 = pl.multiple_of(jnp.minimum(t0 + tc, rows - halo), halo)
        prev = jnp.where(starts_seq(c), 0.0, xr_ref[pl.ds(p0, halo), :].astype(F32))
        nxt = jnp.where(ends_seq(c), 0.0, xr_ref[pl.ds(n0, halo), :].astype(F32))
        ext = jnp.concatenate([prev, main, nxt], axis=0)
        row = _row_in_group(main.shape)

        def earlier(s):
            r = _rot8(ext, s)
            return jnp.where(row < s, r[halo - 8:halo - 8 + tc], r[halo:halo + tc])

        r7 = _rot8(ext, 7)
        later = jnp.where(row >= 7, r7[halo + 8:halo + 8 + tc], r7[halo:halo + tc])
        acc = cb + cw[2:3] * main
        acc = acc + cw[0:1] * earlier(2)
        acc = acc + cw[1:2] * earlier(1)
        acc = acc + cw[3:4] * later
        return acc

    def direction(d, xc, h):
        g = jnp.tanh(jnp.dot(xc.astype(BF16), wri_ref[d], preferred_element_type=F32) + bri_ref[d])
        i = 0.5 * g[:, hd:] + 0.5
        nl = -lam_ref[d]
        softplus = jnp.maximum(nl, 0.0) + jnp.log1p(jnp.exp(-jnp.abs(nl)))
        half_c = (-0.5 * LRU_C) * softplus
        log_a = half_c * g[:, :hd] + half_c
        a = jnp.exp(log_a)
        b = jnp.sqrt(-jnp.tanh(log_a) * (a * a + 1.0)) * i * xc
        return _lru_scan_chunk(a, b, h, reverse=(d == 1))

    def fwd(c, h):
        t0 = pl.multiple_of(c * tc, tc)
        xc = conv_chunk(c)
        xc_ref[pl.ds(t0, tc), :] = xc
        hs, h = direction(0, xc, jnp.where(starts_seq(c), 0.0, h))
        hf_ref[pl.ds(t0, tc), :] = hs
        return h

    def bwd(k, h):
        c = n_ck - 1 - k
        t0 = pl.multiple_of(c * tc, tc)
        hs, h = direction(1, xc_ref[pl.ds(t0, tc), :], jnp.where(ends_seq(c), 0.0, h))
        tot = hf_ref[pl.ds(t0, tc), :] + hs
        o_ref[pl.ds(t0, tc), :] = (_gelu(gr_ref[pl.ds(t0, tc), :].astype(F32)) * tot).astype(o_ref.dtype)
        return h

    h0 = jnp.zeros((1, hd), F32)
    lax.fori_loop(0, n_ck, fwd, h0)
    lax.fori_loop(0, n_ck, bwd, h0)


def _rglru(z, unit_seq, unit, col_x, col_g, cw, cb, wri, bri, lam, tc):
    t = z.shape[0]
    _, heads, hd, _ = wri.shape
    grid_spec = pltpu.PrefetchScalarGridSpec(
        num_scalar_prefetch=1,
        grid=(t // unit, heads),
        in_specs=[
            pl.BlockSpec((unit, hd), lambda u, h, sl: (u, col_x + h)),
            pl.BlockSpec((unit, hd), lambda u, h, sl: (u, col_g + h)),
            pl.BlockSpec((cw.shape[0], hd), lambda u, h, sl: (0, h)),
            pl.BlockSpec((1, hd), lambda u, h, sl: (0, h)),
            pl.BlockSpec((2, None, hd, 2 * hd), lambda u, h, sl: (0, h, 0, 0)),
            pl.BlockSpec((2, None, 1, 2 * hd), lambda u, h, sl: (0, h, 0, 0)),
            pl.BlockSpec((2, None, 1, hd), lambda u, h, sl: (0, h, 0, 0)),
        ],
        out_specs=pl.BlockSpec((unit, hd), lambda u, h, sl: (u, h)),
        scratch_shapes=[pltpu.VMEM((unit, hd), F32), pltpu.VMEM((unit, hd), F32)],
    )
    return pl.pallas_call(
        functools.partial(_rglru_kernel, tc=tc),
        grid_spec=grid_spec,
        out_shape=jax.ShapeDtypeStruct((t, heads * hd), BF16),
        compiler_params=_params(("arbitrary", "arbitrary")),
        name="rglru",
    )(unit_seq, z, z, cw, cb, wri, bri, lam)


def _outproj_kernel(xa_ref, xb_ref, a_ref, b_ref, gb_ref, wo_ref, gf_ref, rw_ref, rb_ref,
                    x1_ref, hp_ref, rt_ref, rg_ref, cnt_ref, carry_ref, *, n_first):
    step = pl.program_id(0)

    @pl.when(step == 0)
    def _():
        carry_ref[...] = jnp.zeros_like(carry_ref)

    da = a_ref.shape[1]
    obn = _rms(b_ref[...].astype(F32), gb_ref[...]).astype(BF16)
    y = jnp.dot(a_ref[...], wo_ref[:da], preferred_element_type=F32)
    y = y + jnp.dot(obn, wo_ref[da:], preferred_element_type=F32)
    x1 = _pair_tile(xa_ref, xb_ref, n_first) + y
    x1_ref[...] = x1
    hf = _rms(x1, gf_ref[...])
    half = hf.shape[1] // 2
    hp_ref[...] = _pack_bf16_pair(hf[:, :half], hf[:, half:])

    logits = jnp.dot(hf.astype(BF16), rw_ref[...], preferred_element_type=F32) + rb_ref[...]
    tm = logits.shape[0]
    lane = lax.broadcasted_iota(jnp.int32, logits.shape, 1)
    lane_f = lane.astype(F32)
    work = logits
    sels, vals = [], []
    for _ in range(TOP_K):
        m = jnp.max(work, axis=-1, keepdims=True)
        sel = jnp.min(jnp.where(work == m, lane_f, float(LANES)), axis=-1, keepdims=True)
        hit = lane_f == sel
        sels.append(hit)
        vals.append(m)
        work = jnp.where(hit, -jnp.inf, work)
    exps = [jnp.exp(v - vals[0]) for v in vals]
    denom = exps[0]
    for e in exps[1:]:
        denom = denom + e

    onehot = sels[0]
    for hit in sels[1:]:
        onehot = onehot | hit
    cnt = onehot.astype(F32)
    rr = lax.broadcasted_iota(jnp.int32, (tm, tm), 0)
    cc = lax.broadcasted_iota(jnp.int32, (tm, tm), 1)
    tril = (rr > cc).astype(BF16)
    before = jnp.dot(tril, cnt.astype(BF16), preferred_element_type=F32) + carry_ref[...]

    ri = jnp.zeros(logits.shape, F32)
    rg = jnp.zeros(logits.shape, F32)
    for k in range(TOP_K):
        e_k = jnp.sum(jnp.where(sels[k], lane_f, 0.0), axis=-1, keepdims=True)
        rank_k = jnp.sum(jnp.where(sels[k], before, 0.0), axis=-1, keepdims=True)
        ri = jnp.where(lane == k, e_k, ri)
        ri = jnp.where(lane == TOP_K + k, rank_k, ri)
        rg = jnp.where(lane == k, exps[k] / denom, rg)
    rt_ref[...] = ri.T[:2 * TOP_K, :].astype(jnp.int32)
    rg_ref[...] = rg
    carry_ref[...] = carry_ref[...] + jnp.sum(cnt, axis=0, keepdims=True)
    cnt_ref[...] = carry_ref[...].astype(jnp.int32)


def _outproj(xa, xb, out_a, out_b, gb, wo, gf, rw, rb, tm):
    d = xa.shape[1]
    t = xa.shape[0] + xb.shape[0]
    n_first = xa.shape[0] // tm
    da = out_a.shape[1]
    db = out_b.shape[1]
    row = lambda i: (i, 0)
    fixed = lambda i: (0, 0)
    return pl.pallas_call(
        functools.partial(_outproj_kernel, n_first=n_first),
        grid=(t // tm,),
        in_specs=_pair_specs(tm, d, n_first) + [
            pl.BlockSpec((tm, da), row),
            pl.BlockSpec((tm, db), row),
            pl.BlockSpec((1, db), fixed),
            pl.BlockSpec(wo.shape, fixed, pipeline_mode=pl.Buffered(1)),
            pl.BlockSpec((1, d), fixed),
            pl.BlockSpec(rw.shape, fixed),
            pl.BlockSpec((1, LANES), fixed),
        ],
        out_specs=[
            pl.BlockSpec((tm, d), row),
            pl.BlockSpec((tm, d // 2), row),
            pl.BlockSpec((2 * TOP_K, tm), lambda i: (0, i)),
            pl.BlockSpec((tm, LANES), row),
            pl.BlockSpec((1, LANES), fixed),
        ],
        out_shape=[
            jax.ShapeDtypeStruct((t, d), F32),
            jax.ShapeDtypeStruct((t, d // 2), U32),
            jax.ShapeDtypeStruct((2 * TOP_K, t), jnp.int32),
            jax.ShapeDtypeStruct((t, LANES), F32),
            jax.ShapeDtypeStruct((1, LANES), jnp.int32),
        ],
        scratch_shapes=[pltpu.VMEM((1, LANES), F32)],
        compiler_params=_params(("arbitrary",)),
        name="outproj_router",
    )(xa, xb, out_a, out_b, gb, wo, gf, rw, rb)


def _tile_slots(dest_kt, tm):
    k, t = dest_kt.shape
    return dest_kt.reshape(k, t // tm, tm).transpose(1, 0, 2).reshape(t // tm, 1, k * tm)


def _dispatch_kernel(zneed_ref, dest_ref, hp_ref, xs_ref, zero_ref, sem, zsem, *, sub, group):
    step = pl.program_id(0)
    tm = hp_ref.shape[0]
    n_zc = zneed_ref.shape[0]

    def zero_copy(j):
        return pltpu.make_async_copy(zero_ref, xs_ref.at[pl.ds(pl.multiple_of(j * sub, sub), sub), :], zsem)

    @pl.when(step == 0)
    def _():
        zero_ref[...] = jnp.zeros_like(zero_ref)

        def start(j, c):
            @pl.when(zneed_ref[j] != 0)
            def _():
                zero_copy(j).start()
            return c

        def wait(j, c):
            @pl.when(zneed_ref[j] != 0)
            def _():
                zero_copy(j).wait()
            return c

        lax.fori_loop(0, n_zc, start, 0)
        lax.fori_loop(0, n_zc, wait, 0)

    def issue(g, c):
        base = g * group
        for r in range(group):
            for k in range(TOP_K):
                d = dest_ref[0, 0, k * tm + base + r]
                pltpu.make_async_copy(hp_ref.at[pl.ds(base + r, 1), :], xs_ref.at[pl.ds(d, 1), :],
                                      sem).start(priority=k % 2)
        return c

    lax.fori_loop(0, tm // group, issue, 0)
    for _ in range(TOP_K):
        pltpu.make_async_copy(hp_ref, xs_ref.at[pl.ds(0, tm), :], sem).wait()


def _dispatch(zneed, dest_kt, hp, n_slots, sub, tm):
    t, half = hp.shape
    grid_spec = pltpu.PrefetchScalarGridSpec(
        num_scalar_prefetch=1,
        grid=(t // tm,),
        in_specs=[
            pl.BlockSpec((1, 1, tm * TOP_K), lambda i, *_: (i, 0, 0), memory_space=pltpu.SMEM),
            pl.BlockSpec((tm, half), lambda i, *_: (i, 0)),
        ],
        out_specs=pl.BlockSpec(memory_space=pl.ANY),
        scratch_shapes=[pltpu.VMEM((sub, half), U32), pltpu.SemaphoreType.DMA, pltpu.SemaphoreType.DMA],
    )
    return pl.pallas_call(
        functools.partial(_dispatch_kernel, sub=sub, group=min(DMA_GROUP, tm)),
        grid_spec=grid_spec,
        out_shape=jax.ShapeDtypeStruct((n_slots, half), U32),
        compiler_params=_params(("arbitrary",)),
        name="dispatch",
    )(zneed, _tile_slots(dest_kt, tm), hp)


def _gmm_kernel(plan_ref, x_ref, bias_ref, w_hbm, o_ref, wbuf, stage, sem, *rest, mode, kc, sub, nw):
    blk = pl.program_id(0)
    expert = plan_ref[0, blk]
    half_idx = plan_ref[1, blk]
    nxt = plan_ref[3, blk]
    valid = blk < plan_ref[7, 0]
    n_sub = jnp.where(valid, plan_ref[6, blk], 0)
    n_ch = w_hbm.shape[1] // kc

    def chunk_rows(k):
        return pl.ds(pl.multiple_of(k * kc, kc), kc)

    def chunk_copy(e, k, st):
        return pltpu.make_async_copy(w_hbm.at[e, chunk_rows(k), :], stage.at[st], sem.at[st])

    def land(e, k, dst):
        st = k % 2
        chunk_copy(e, k, st).wait()
        wbuf[dst, chunk_rows(k), :] = stage[st].astype(BF16)

        @pl.when(k + 2 < n_ch)
        def _():
            chunk_copy(e, k + 2, st).start()

    @pl.when(blk == 0)
    def _():
        chunk_copy(expert, 0, 0).start()
        chunk_copy(expert, 1, 1).start()
        lax.fori_loop(0, n_ch, lambda k, c: (land(expert, k, half_idx), c)[1], 0)

    @pl.when(valid & (plan_ref[2, blk] == 1) & (nxt >= 0))
    def _():
        chunk_copy(nxt, 0, 0).start()
        chunk_copy(nxt, 1, 1).start()

    if mode == "gate_up":
        xb_ref, = rest
        half = x_ref.shape[1]
        de = o_ref.shape[1]

        def run(rows):
            v = x_ref[rows, :]
            xb_ref[rows, :half] = _unpack_lo(v).astype(BF16)
            xb_ref[rows, half:] = _unpack_hi(v).astype(BF16)
            x = xb_ref[rows, :]
            for n in range(de // nw):
                cg = slice(n * nw, (n + 1) * nw)
                cu = slice(de + n * nw, de + (n + 1) * nw)
                gate = jnp.dot(x, wbuf[half_idx, :, cg], preferred_element_type=F32) + bias_ref[:, cg]
                up = jnp.dot(x, wbuf[half_idx, :, cu], preferred_element_type=F32) + bias_ref[:, cu]
                gate = jnp.minimum(gate, SWIGLU_LIMIT)
                up = jnp.clip(up, -SWIGLU_LIMIT, SWIGLU_LIMIT)
                act = (up + 1.0) * (gate * jax.nn.sigmoid(SWIGLU_ALPHA * gate))
                o_ref[rows, cg] = act.astype(o_ref.dtype)
    else:
        half = o_ref.shape[1]

        def run(rows):
            x = x_ref[rows, :]
            for n in range(half // nw):
                cl = slice(n * nw, (n + 1) * nw)
                ch = slice(half + n * nw, half + (n + 1) * nw)
                lo = jnp.dot(x, wbuf[half_idx, :, cl], preferred_element_type=F32) + bias_ref[:, cl]
                hi = jnp.dot(x, wbuf[half_idx, :, ch], preferred_element_type=F32) + bias_ref[:, ch]
                o_ref[rows, cl] = _pack_bf16_pair(lo, hi)

    full = x_ref.shape[0] // sub

    @pl.when(n_sub < full)
    def _():
        o_ref[...] = jnp.zeros_like(o_ref)

    @pl.when(n_sub == full)
    def _():
        run(slice(None))

    @pl.when((n_sub > 0) & (n_sub < full))
    def _():
        def body(i, c):
            run(pl.ds(pl.multiple_of(i * sub, sub), sub))
            return c
        lax.fori_loop(0, n_sub, body, 0)

    lax.fori_loop(plan_ref[4, blk], plan_ref[5, blk], lambda k, c: (land(nxt, k, 1 - half_idx), c)[1], 0)


def _gmm(plan, x, bias, w, mode, bm, sub, kc):
    n_slots = x.shape[0]
    n_exp, kdim, ndim = w.shape
    n_blk = n_slots // bm
    if mode == "gate_up":
        out_cols, out_dtype = ndim // 2, BF16
        extra = [pltpu.VMEM((bm, kdim), BF16)]
    else:
        out_cols, out_dtype = ndim // 2, U32
        extra = []

    def live(b, plan):
        return (jnp.minimum(b, plan[7, 0] - 1), 0)

    grid_spec = pltpu.PrefetchScalarGridSpec(
        num_scalar_prefetch=1,
        grid=(n_blk,),
        in_specs=[
            pl.BlockSpec((bm, x.shape[1]), live),
            pl.BlockSpec((None, 1, ndim), lambda b, plan: (plan[0, b], 0, 0)),
            pl.BlockSpec(memory_space=pl.ANY),
        ],
        out_specs=pl.BlockSpec((bm, out_cols), lambda b, plan: (b, 0)),
        scratch_shapes=[
            pltpu.VMEM((2, kdim, ndim), BF16),
            pltpu.VMEM((2, kc, ndim), F32),
            pltpu.SemaphoreType.DMA((2,)),
        ] + extra,
    )
    return pl.pallas_call(
        functools.partial(_gmm_kernel, mode=mode, kc=kc, sub=sub, nw=min(512, out_cols)),
        grid_spec=grid_spec,
        out_shape=jax.ShapeDtypeStruct((n_slots, out_cols), out_dtype),
        compiler_params=_params(("arbitrary",)),
        name="experts_" + mode,
    )(plan, x, bias, w)


def _expert_plan(counts, bm, sub, n_blk, n_ch):
    n_exp = counts.shape[0]
    ar = jnp.arange(n_exp, dtype=jnp.int32)
    nb_e = (counts + bm - 1) // bm
    blk_end = jnp.cumsum(nb_e)
    blk_start = blk_end - nb_e
    n_valid = blk_end[-1]
    blk = jnp.arange(n_blk, dtype=jnp.int32)
    last = jnp.minimum(blk, n_valid - 1)
    be = jnp.minimum(jnp.sum(blk_end[None, :] <= last[:, None], axis=1), n_exp - 1).astype(jnp.int32)
    mine = be[:, None] == ar[None, :]

    def of_block(per_expert):
        return jnp.sum(jnp.where(mine, per_expert[None, :], 0), axis=1)

    valid = blk < n_valid
    i_in = last - of_block(blk_start)
    nb_b = of_block(nb_e)
    nonempty = nb_e > 0
    ordinal = jnp.cumsum(nonempty.astype(jnp.int32)) - 1
    later = (ar[None, :] > ar[:, None]) & nonempty[None, :]
    nxt_e = jnp.min(jnp.where(later, ar[None, :], n_exp), axis=1)
    nxt_e = jnp.where(nxt_e >= n_exp, -1, nxt_e)
    nxt_b = of_block(nxt_e)
    has_next = valid & (nxt_b >= 0)
    lo = jnp.where(has_next, i_in * n_ch // nb_b, 0)
    hi = jnp.where(has_next, (i_in + 1) * n_ch // nb_b, 0)
    rows = jnp.where(valid, jnp.clip(of_block(counts) - i_in * bm, 0, bm), 0)
    nsub = (rows + sub - 1) // sub
    plan = jnp.stack([
        be, of_block(ordinal) % 2, (valid & (i_in == 0)).astype(jnp.int32), nxt_b, lo, hi, nsub,
        jnp.full((n_blk,), n_valid, jnp.int32),
    ]).astype(jnp.int32)
    per = bm // sub
    sub_rows = rows[:, None] - jnp.arange(per, dtype=jnp.int32)[None, :] * sub
    zneed = (sub_rows < sub).astype(jnp.int32).reshape(n_blk * per)
    return blk_start * bm, plan, zneed


def _combine_kernel(dcur_ref, dnxt_ref, x1_ref, rg_ref, gfin_ref, os_ref, ya_ref, yb_ref, gbuf_ref, sem,
                    *, final_norm, n_first, cw):
    step = pl.program_id(0)
    last = pl.num_programs(0) - 1
    tm = x1_ref.shape[0]
    half = gbuf_ref.shape[3]
    cur = step % 2
    n_cc = half // cw

    def token_copies(dref, r, buf):
        for k in range(TOP_K):
            d = dref[0, 0, k * tm + r]
            pltpu.make_async_copy(os_ref.at[pl.ds(d, 1), :], gbuf_ref.at[buf, k, pl.ds(r, 1), :],
                                  sem.at[buf]).start(priority=k % 2)

    def wait_tile(buf):
        for k in range(TOP_K):
            pltpu.make_async_copy(os_ref.at[pl.ds(0, tm), :], gbuf_ref.at[buf, k], sem.at[buf]).wait()

    @pl.when(step == 0)
    def _():
        def body(g, c):
            for r in range(DMA_GROUP):
                token_copies(dcur_ref, g * DMA_GROUP + r, 0)
            return c
        lax.fori_loop(0, tm // DMA_GROUP, body, 0)

    wait_tile(cur)

    def finish(y_ref):
        rg = rg_ref[...]
        ss = jnp.zeros((tm, 1), F32)
        share = tm // n_cc
        for c in range(n_cc):
            cl = slice(c * cw, (c + 1) * cw)
            ch = slice(half + c * cw, half + (c + 1) * cw)
            lo = x1_ref[:, cl]
            hi = x1_ref[:, ch]
            for k in range(TOP_K):
                g = rg[:, k:k + 1]
                v = gbuf_ref[cur, k, :, cl]
                lo = lo + g * _unpack_lo(v)
                hi = hi + g * _unpack_hi(v)
            if final_norm:
                ss = ss + jnp.sum(lo * lo, axis=-1, keepdims=True) + jnp.sum(hi * hi, axis=-1, keepdims=True)
            y_ref[:, cl] = lo
            y_ref[:, ch] = hi
            for r in range(c * share, (c + 1) * share):
                token_copies(dnxt_ref, r, 1 - cur)
        if final_norm:
            y_ref[...] = y_ref[...] * lax.rsqrt(ss / (2 * half) + EPS) * gfin_ref[...]

    @pl.when(step < n_first)
    def _():
        finish(ya_ref)

    @pl.when(step >= n_first)
    def _():
        finish(yb_ref)

    @pl.when(step == last)
    def _():
        wait_tile(1 - cur)


def _combine(dest_kt, x1, rg, gfin, os_, t_first, tm, final_norm):
    t, d = x1.shape
    half = d // 2
    n_tiles = t // tm
    n_first = t_first // tm
    dest3 = _tile_slots(dest_kt, tm)
    return pl.pallas_call(
        functools.partial(_combine_kernel, final_norm=final_norm, n_first=n_first, cw=min(256, half)),
        grid=(n_tiles,),
        in_specs=[
            pl.BlockSpec((1, 1, tm * TOP_K), lambda i: (i, 0, 0), memory_space=pltpu.SMEM),
            pl.BlockSpec((1, 1, tm * TOP_K), lambda i: (jnp.minimum(i + 1, n_tiles - 1), 0, 0),
                         memory_space=pltpu.SMEM),
            pl.BlockSpec((tm, d), lambda i: (i, 0)),
            pl.BlockSpec((tm, LANES), lambda i: (i, 0)),
            pl.BlockSpec((1, d), lambda i: (0, 0)),
            pl.BlockSpec(memory_space=pl.ANY),
        ],
        out_specs=_pair_specs(tm, d, n_first),
        out_shape=[jax.ShapeDtypeStruct((t_first, d), F32), jax.ShapeDtypeStruct((t - t_first, d), F32)],
        scratch_shapes=[pltpu.VMEM((2, TOP_K, tm, half), U32), pltpu.SemaphoreType.DMA((2,))],
        compiler_params=_params(("arbitrary",)),
        name="combine",
    )(dest3, dest3, x1, rg, gfin, os_)


def _tile(n, pref):
    while n % pref:
        pref //= 2
    return pref


def _layer(xa, xb, seq_a, seq_b, p, final_g):
    ta, d = xa.shape
    t = ta + xb.shape[0]
    dg = p["gmlp_ln_g"].shape[-1]
    chunk = p["gmlp_ws"].shape[-1]
    d_lru = p["conv_w"].shape[-1]
    heads_l, hd_l = p["lru_wr"].shape[1], p["lru_wr"].shape[2]
    n_exp = p["router_w"].shape[-1]
    assert p["w_down"].shape[1] == d, "expert width must equal the model width (shared weight-chunk plan)"

    unit = max(seq_a, seq_b)
    assert unit % seq_a == 0 and unit % seq_b == 0 and ta % unit == 0 and (t - ta) % unit == 0
    tm = _tile(math.gcd(ta, t - ta), 512)
    z = _inproj(xa, xb, p["mix_norm_g"][None], p["w_in"].astype(BF16), tm)

    bs_b = jnp.broadcast_to(p["gmlp_bs"][:, :, None], p["gmlp_ws"].shape).astype(F32)
    out_a = _gmlp(z, p["gmlp_ln_g"][None], p["gmlp_ln_b"][None], p["gmlp_ws"].astype(BF16), bs_b,
                  p["out_norm_a"][None], max(tm, chunk))

    wri = (0.5 * jnp.concatenate([p["lru_wr"], p["lru_wi"]], axis=-1)).astype(BF16)
    bri = 0.5 * jnp.concatenate([p["lru_br"].reshape(2, heads_l, 1, hd_l),
                                 p["lru_bi"].reshape(2, heads_l, 1, hd_l)], axis=-1)
    lam = p["lru_lam"].reshape(2, heads_l, 1, hd_l)
    unit_seq = jnp.asarray([seq_a] * (ta // unit) + [seq_b] * ((t - ta) // unit), jnp.int32)
    out_b = _rglru(z, unit_seq, unit, 2 * dg // hd_l, (2 * dg + d_lru) // hd_l,
                   p["conv_w"], p["conv_b"][None], wri, bri, lam, _tile(math.gcd(seq_a, seq_b), 512))

    rw = jnp.zeros((d, LANES), BF16).at[:, :n_exp].set(p["router_w"].astype(BF16))
    rb = jnp.full((1, LANES), NEG_BIG, F32).at[0, :n_exp].set(p["router_b"])
    x1, hp, rt, rg, cnt = _outproj(xa, xb, out_a, out_b, p["out_norm_b"][None], p["w_out"].astype(BF16),
                                   p["ffn_norm_g"][None], rw, rb, tm)

    m = t * TOP_K
    bm = _tile(m, ROW_BLOCK)
    sub = min(SUB_BLOCK, bm)
    kc = _tile(d, WEIGHT_CHUNK_ROWS)
    counts = cnt[0, :n_exp]
    n_blk = m // bm + n_exp
    n_slots = n_blk * bm
    pad_start, plan, zneed = _expert_plan(counts, bm, sub, n_blk, d // kc)
    top_e = rt[:TOP_K]
    base = jnp.sum(jnp.where(top_e[..., None] == jnp.arange(n_exp, dtype=jnp.int32), pad_start, 0), axis=-1)
    dest_kt = base + rt[TOP_K:]

    xs = _dispatch(zneed, dest_kt, hp, n_slots, sub, _tile(t, 1024))
    act = _gmm(plan, xs, p["b_gate_up"][:, None, :], p["w_gate_up"], "gate_up", bm, sub, kc)
    os_ = _gmm(plan, act, p["b_down"][:, None, :], p["w_down"], "down", bm, sub, kc)
    gfin = jnp.ones((1, d), F32) if final_g is None else final_g[None]
    return _combine(dest_kt, x1, rg, gfin, os_, ta, _tile(math.gcd(ta, t - ta), 256), final_g is not None)


_LAYER_KEYS = ("mix_norm_g", "w_in", "gmlp_ln_g", "gmlp_ln_b", "gmlp_ws", "gmlp_bs", "conv_w", "conv_b",
               "lru_wr", "lru_br", "lru_wi", "lru_bi", "lru_lam", "out_norm_a", "out_norm_b", "w_out",
               "ffn_norm_g", "router_w", "router_b", "w_gate_up", "b_gate_up", "w_down", "b_down")


def kernel(x_prompt, x_sample, mix_norm_g, w_in, gmlp_ln_g, gmlp_ln_b, gmlp_ws, gmlp_bs, conv_w, conv_b, lru_wr, lru_br, lru_wi, lru_bi, lru_lam, out_norm_a, out_norm_b, w_out, ffn_norm_g, router_w, router_b, w_gate_up, b_gate_up, w_down, b_down, final_norm_g):
    stacked = dict(zip(_LAYER_KEYS, (mix_norm_g, w_in, gmlp_ln_g, gmlp_ln_b, gmlp_ws, gmlp_bs, conv_w, conv_b,
                                     lru_wr, lru_br, lru_wi, lru_bi, lru_lam, out_norm_a, out_norm_b, w_out,
                                     ffn_norm_g, router_w, router_b, w_gate_up, b_gate_up, w_down, b_down)))
    depth = w_in.shape[0]
    bp, sp, d = x_prompt.shape
    bs, ss, _ = x_sample.shape
    xa = x_prompt.reshape(bp * sp, d)
    xb = x_sample.reshape(bs * ss, d)
    for layer in range(depth):
        p = {k: v[layer] for k, v in stacked.items()}
        xa, xb = _layer(xa, xb, sp, ss, p, final_norm_g if layer == depth - 1 else None)
    return xa.reshape(bp, sp, d), xb.reshape(bs, ss, d)
```

```python
import functools
import math

import jax
import jax.numpy as jnp
from jax import lax
from jax.experimental import pallas as pl
from jax.experimental.pallas import tpu as pltpu

TOP_K = 4
LRU_C = 8.0
SWIGLU_LIMIT = 7.0
SWIGLU_ALPHA = 1.702
EPS = 1e-6

LANES = 128
VMEM_LIMIT = 56 * 1024 * 1024
NEG_BIG = -1e30
ROW_BLOCK = 512
SUB_BLOCK = 128
WEIGHT_CHUNK_ROWS = 128
DMA_GROUP = 8
OUTPROJ_PIECES = 2

F32 = jnp.float32
BF16 = jnp.bfloat16
U32 = jnp.uint32


def _params(sem):
    return pltpu.CompilerParams(dimension_semantics=sem, vmem_limit_bytes=VMEM_LIMIT)


def _gelu(x):
    return 0.5 * x * (1.0 + jnp.tanh(0.7978845608028654 * (x + 0.044715 * (x * x * x))))


def _rms(x, g):
    return x * lax.rsqrt(jnp.mean(x * x, axis=-1, keepdims=True) + EPS) * g


def _pack_bf16_pair(lo, hi):
    lo_bits = lax.bitcast_convert_type(lo.astype(BF16).astype(F32), U32)
    hi_bits = lax.bitcast_convert_type(hi.astype(BF16).astype(F32), U32)
    return (lo_bits >> 16) | (hi_bits & jnp.uint32(0xFFFF0000))


def _unpack_lo(v):
    return lax.bitcast_convert_type(v << 16, F32)


def _unpack_hi(v):
    return lax.bitcast_convert_type(v & jnp.uint32(0xFFFF0000), F32)


def _pair_specs(tm, d, n_first):
    return [pl.BlockSpec((tm, d), lambda i: (jnp.minimum(i, n_first - 1), 0)),
            pl.BlockSpec((tm, d), lambda i: (jnp.maximum(i - n_first, 0), 0))]


def _pair_tile(xa_ref, xb_ref, n_first):
    return jnp.where(pl.program_id(0) < n_first, xa_ref[...], xb_ref[...])


def _inproj_kernel(xa_ref, xb_ref, g_ref, w_ref, z_ref, *, n_chunk, n_first):
    xn = _rms(_pair_tile(xa_ref, xb_ref, n_first), g_ref[...]).astype(BF16)
    for n in range(w_ref.shape[1] // n_chunk):
        cols = slice(n * n_chunk, (n + 1) * n_chunk)
        z_ref[:, cols] = jnp.dot(xn, w_ref[:, cols], preferred_element_type=F32).astype(z_ref.dtype)


def _inproj(xa, xb, g, w, tm):
    d = xa.shape[1]
    t = xa.shape[0] + xb.shape[0]
    d_in = w.shape[1]
    n_first = xa.shape[0] // tm
    return pl.pallas_call(
        functools.partial(_inproj_kernel, n_chunk=min(1024, d_in), n_first=n_first),
        grid=(t // tm,),
        in_specs=_pair_specs(tm, d, n_first) + [
            pl.BlockSpec((1, d), lambda i: (0, 0)),
            pl.BlockSpec((d, d_in), lambda i: (0, 0), pipeline_mode=pl.Buffered(1)),
        ],
        out_specs=pl.BlockSpec((tm, d_in), lambda i: (i, 0)),
        out_shape=jax.ShapeDtypeStruct((t, d_in), BF16),
        compiler_params=_params(("arbitrary",)),
        name="inproj",
    )(xa, xb, g, w)


def _gmlp_kernel(u_ref, v_ref, lng_ref, lnb_ref, ws_ref, bs_ref, ga_ref, o_ref, mix_ref, *, heads, chunk):
    v = _gelu(v_ref[...].astype(F32))
    mu = jnp.mean(v, axis=-1, keepdims=True)
    vc = v - mu
    var = jnp.mean(vc * vc, axis=-1, keepdims=True)
    vn = (vc * lax.rsqrt(var + EPS) * lng_ref[...] + lnb_ref[...]).astype(BF16)
    tm, dg = vn.shape
    hd = dg // heads
    for c in range(tm // chunk):
        rows = slice(c * chunk, (c + 1) * chunk)
        for h in range(heads):
            cols = slice(h * hd, (h + 1) * hd)
            mixed = jnp.dot(ws_ref[h], vn[rows, cols], preferred_element_type=F32) + bs_ref[h]
            mix_ref[rows, cols] = mixed
    out = _gelu(u_ref[...].astype(F32)) * mix_ref[...]
    o_ref[...] = _rms(out, ga_ref[...]).astype(o_ref.dtype)


def _gmlp(z, ln_g, ln_b, ws, bs_b, ga, tm):
    t = z.shape[0]
    heads, chunk, _ = ws.shape
    dg = ln_g.shape[1]
    vec = pl.BlockSpec((1, dg), lambda i: (0, 0))
    return pl.pallas_call(
        functools.partial(_gmlp_kernel, heads=heads, chunk=chunk),
        grid=(t // tm,),
        in_specs=[
            pl.BlockSpec((tm, dg), lambda i: (i, 0)),
            pl.BlockSpec((tm, dg), lambda i: (i, 1)),
            vec, vec,
            pl.BlockSpec(ws.shape, lambda i: (0, 0, 0)),
            pl.BlockSpec(bs_b.shape, lambda i: (0, 0, 0)),
            vec,
        ],
        out_specs=pl.BlockSpec((tm, dg), lambda i: (i, 0)),
        out_shape=jax.ShapeDtypeStruct((t, dg), BF16),
        scratch_shapes=[pltpu.VMEM((tm, dg), F32)],
        compiler_params=_params(("arbitrary",)),
        name="gmlp",
    )(z, z, ln_g, ln_b, ws, bs_b, ga)


def _rot8(x, s):
    n, w = x.shape
    return pltpu.roll(x.reshape(n // 8, 8, w), s, axis=1).reshape(n, w)


def _row_in_group(shape):
    return lax.broadcasted_iota(jnp.int32, shape, 0) & 7


def _shift_rows(x, s, fill, reverse, row):
    if reverse:
        return jnp.where(row >= 8 - s, fill, _rot8(x, 8 - s))
    return jnp.where(row < s, fill, _rot8(x, s))


def _lru_scan_chunk(a, b, h, reverse):
    row = _row_in_group(a.shape)
    for s in (1, 2, 4):
        b = a * _shift_rows(b, s, 0.0, reverse, row) + b
        a = a * _shift_rows(a, s, 1.0, reverse, row)
    n_grp = a.shape[0] // 8
    out = [None] * n_grp
    order = range(n_grp - 1, -1, -1) if reverse else range(n_grp)
    edge = 0 if reverse else 7
    for gi in order:
        rows = slice(gi * 8, gi * 8 + 8)
        hg = b[rows] + a[rows] * h
        out[gi] = hg
        h = hg[edge:edge + 1]
    return jnp.concatenate(out, axis=0), h


def _rglru_kernel(seqlen_ref, xr_ref, gr_ref, cw_ref, cb_ref, wri_ref, bri_ref, lam_ref, o_ref, hf_ref, xc_ref,
                  *, tc):
    rows, hd = xr_ref.shape
    n_ck = rows // tc
    halo = 16
    cw = cw_ref[...]
    cb = cb_ref[...]
    seq = seqlen_ref[pl.program_id(0)]

    def starts_seq(c):
        return lax.rem(c * tc, seq) == 0

    def ends_seq(c):
        return lax.rem((c + 1) * tc, seq) == 0

    def conv_chunk(c):
        t0 = pl.multiple_of(c * tc, tc)
        main = xr_ref[pl.ds(t0, tc), :].astype(F32)
        p0 = pl.multiple_of(jnp.maximum(t0 - halo, 0), halo)
        n0 = pl.multiple_of(jnp.minimum(t0 + tc, rows - halo), halo)
        prev = jnp.where(starts_seq(c), 0.0, xr_ref[pl.ds(p0, halo), :].astype(F32))
        nxt = jnp.where(ends_seq(c), 0.0, xr_ref[pl.ds(n0, halo), :].astype(F32))
        ext = jnp.concatenate([prev, main, nxt], axis=0)
        row = _row_in_group(main.shape)

        def earlier(s):
            r = _rot8(ext, s)
            return jnp.where(row < s, r[halo - 8:halo - 8 + tc], r[halo:halo + tc])

        r7 = _rot8(ext, 7)
        later = jnp.where(row >= 7, r7[halo + 8:halo + 8 + tc], r7[halo:halo + tc])
        acc = cb + cw[2:3] * main
        acc = acc + cw[0:1] * earlier(2)
        acc = acc + cw[1:2] * earlier(1)
        acc = acc + cw[3:4] * later
        return acc

    def direction(d, xc, h):
        g = jnp.tanh(jnp.dot(xc.astype(BF16), wri_ref[d], preferred_element_type=F32) + bri_ref[d])
        i = 0.5 * g[:, hd:] + 0.5
        nl = -lam_ref[d]
        softplus = jnp.maximum(nl, 0.0) + jnp.log1p(jnp.exp(-jnp.abs(nl)))
        half_c = (-0.5 * LRU_C) * softplus
        log_a = half_c * g[:, :hd] + half_c
        a = jnp.exp(log_a)
        b = jnp.sqrt(-jnp.tanh(log_a) * (a * a + 1.0)) * i * xc
        return _lru_scan_chunk(a, b, h, reverse=(d == 1))

    def fwd(c, h):
        t0 = pl.multiple_of(c * tc, tc)
        xc = conv_chunk(c)
        xc_ref[pl.ds(t0, tc), :] = xc
        hs, h = direction(0, xc, jnp.where(starts_seq(c), 0.0, h))
        hf_ref[pl.ds(t0, tc), :] = hs
        return h

    def bwd(k, h):
        c = n_ck - 1 - k
        t0 = pl.multiple_of(c * tc, tc)
        hs, h = direction(1, xc_ref[pl.ds(t0, tc), :], jnp.where(ends_seq(c), 0.0, h))
        tot = hf_ref[pl.ds(t0, tc), :] + hs
        o_ref[pl.ds(t0, tc), :] = (_gelu(gr_ref[pl.ds(t0, tc), :].astype(F32)) * tot).astype(o_ref.dtype)
        return h

    h0 = jnp.zeros((1, hd), F32)
    lax.fori_loop(0, n_ck, fwd, h0)
    lax.fori_loop(0, n_ck, bwd, h0)


def _rglru(z, unit_seq, unit, col_x, col_g, cw, cb, wri, bri, lam, tc):
    t = z.shape[0]
    _, heads, hd, _ = wri.shape
    grid_spec = pltpu.PrefetchScalarGridSpec(
        num_scalar_prefetch=1,
        grid=(t // unit, heads),
        in_specs=[
            pl.BlockSpec((unit, hd), lambda u, h, sl: (u, col_x + h)),
            pl.BlockSpec((unit, hd), lambda u, h, sl: (u, col_g + h)),
            pl.BlockSpec((cw.shape[0], hd), lambda u, h, sl: (0, h)),
            pl.BlockSpec((1, hd), lambda u, h, sl: (0, h)),
            pl.BlockSpec((2, None, hd, 2 * hd), lambda u, h, sl: (0, h, 0, 0)),
            pl.BlockSpec((2, None, 1, 2 * hd), lambda u, h, sl: (0, h, 0, 0)),
            pl.BlockSpec((2, None, 1, hd), lambda u, h, sl: (0, h, 0, 0)),
        ],
        out_specs=pl.BlockSpec((unit, hd), lambda u, h, sl: (u, h)),
        scratch_shapes=[pltpu.VMEM((unit, hd), F32), pltpu.VMEM((unit, hd), F32)],
    )
    return pl.pallas_call(
        functools.partial(_rglru_kernel, tc=tc),
        grid_spec=grid_spec,
        out_shape=jax.ShapeDtypeStruct((t, heads * hd), BF16),
        compiler_params=_params(("arbitrary", "arbitrary")),
        name="rglru",
    )(unit_seq, z, z, cw, cb, wri, bri, lam)


def _outproj_kernel(xa_ref, xb_ref, a_ref, b_ref, gb_ref, wo_ref, gf_ref, rw_ref, rb_ref,
                    x1_ref, hp_ref, rt_ref, rg_ref, cnt_ref, carry_ref, *, n_first):
    step = pl.program_id(0)

    @pl.when(step == 0)
    def _():
        carry_ref[...] = jnp.zeros_like(carry_ref)

    da = a_ref.shape[1]
    tm = a_ref.shape[0]
    x_tile = _pair_tile(xa_ref, xb_ref, n_first)

    def project(rows):
        obn = _rms(b_ref[rows, :].astype(F32), gb_ref[...]).astype(BF16)
        y = jnp.dot(a_ref[rows, :], wo_ref[:da], preferred_element_type=F32)
        return y + jnp.dot(obn, wo_ref[da:], preferred_element_type=F32)

    def route(rows, y, carry):
        n = rows.stop - rows.start
        x1 = x_tile[rows, :] + y
        x1_ref[rows, :] = x1
        hf = _rms(x1, gf_ref[...])
        half = hf.shape[1] // 2
        hp_ref[rows, :] = _pack_bf16_pair(hf[:, :half], hf[:, half:])

        logits = jnp.dot(hf.astype(BF16), rw_ref[...], preferred_element_type=F32) + rb_ref[...]
        lane = lax.broadcasted_iota(jnp.int32, logits.shape, 1)
        lane_f = lane.astype(F32)
        work = logits
        hits, ids, vals = [], [], []
        for _ in range(TOP_K):
            m = jnp.max(work, axis=-1, keepdims=True)
            sel = jnp.min(jnp.where(work == m, lane_f, float(LANES)), axis=-1, keepdims=True)
            hit = lane_f == sel
            hits.append(hit)
            ids.append(sel)
            vals.append(m)
            work = jnp.where(hit, -jnp.inf, work)
        exps = [jnp.exp(v - vals[0]) for v in vals]
        denom = exps[0]
        for e in exps[1:]:
            denom = denom + e

        onehot = hits[0]
        for hit in hits[1:]:
            onehot = onehot | hit
        cnt = onehot.astype(F32)
        rr = lax.broadcasted_iota(jnp.int32, (n, n), 0)
        cc = lax.broadcasted_iota(jnp.int32, (n, n), 1)
        tril = (rr > cc).astype(BF16)
        before = jnp.dot(tril, cnt.astype(BF16), preferred_element_type=F32) + carry

        ri = jnp.zeros(logits.shape, F32)
        rg = jnp.zeros(logits.shape, F32)
        for k in range(TOP_K):
            rank_k = jnp.sum(jnp.where(hits[k], before, 0.0), axis=-1, keepdims=True)
            ri = jnp.where(lane == k, ids[k], ri)
            ri = jnp.where(lane == TOP_K + k, rank_k, ri)
            rg = jnp.where(lane == k, exps[k] / denom, rg)
        rt_ref[:, rows] = ri.T[:2 * TOP_K, :].astype(jnp.int32)
        rg_ref[rows, :] = rg
        return carry + jnp.sum(cnt, axis=0, keepdims=True)

    n_piece = OUTPROJ_PIECES if tm % (OUTPROJ_PIECES * LANES) == 0 else 1
    pieces = [slice(i * tm // n_piece, (i + 1) * tm // n_piece) for i in range(n_piece)]
    carry = carry_ref[...]
    ys = [project(pieces[0])]
    for i in range(n_piece):
        if i + 1 < n_piece:
            ys.append(project(pieces[i + 1]))
        carry = route(pieces[i], ys[i], carry)
    carry_ref[...] = carry
    cnt_ref[...] = carry.astype(jnp.int32)


def _outproj(xa, xb, out_a, out_b, gb, wo, gf, rw, rb, tm):
    d = xa.shape[1]
    t = xa.shape[0] + xb.shape[0]
    n_first = xa.shape[0] // tm
    da = out_a.shape[1]
    db = out_b.shape[1]
    row = lambda i: (i, 0)
    fixed = lambda i: (0, 0)
    return pl.pallas_call(
        functools.partial(_outproj_kernel, n_first=n_first),
        grid=(t // tm,),
        in_specs=_pair_specs(tm, d, n_first) + [
            pl.BlockSpec((tm, da), row),
            pl.BlockSpec((tm, db), row),
            pl.BlockSpec((1, db), fixed),
            pl.BlockSpec(wo.shape, fixed, pipeline_mode=pl.Buffered(1)),
            pl.BlockSpec((1, d), fixed),
            pl.BlockSpec(rw.shape, fixed),
            pl.BlockSpec((1, LANES), fixed),
        ],
        out_specs=[
            pl.BlockSpec((tm, d), row),
            pl.BlockSpec((tm, d // 2), row),
            pl.BlockSpec((2 * TOP_K, tm), lambda i: (0, i)),
            pl.BlockSpec((tm, LANES), row),
            pl.BlockSpec((1, LANES), fixed),
        ],
        out_shape=[
            jax.ShapeDtypeStruct((t, d), F32),
            jax.ShapeDtypeStruct((t, d // 2), U32),
            jax.ShapeDtypeStruct((2 * TOP_K, t), jnp.int32),
            jax.ShapeDtypeStruct((t, LANES), F32),
            jax.ShapeDtypeStruct((1, LANES), jnp.int32),
        ],
        scratch_shapes=[pltpu.VMEM((1, LANES), F32)],
        compiler_params=_params(("arbitrary",)),
        name="outproj_router",
    )(xa, xb, out_a, out_b, gb, wo, gf, rw, rb)


def _tile_slots(dest_kt, tm):
    k, t = dest_kt.shape
    return dest_kt.reshape(k, t // tm, tm).transpose(1, 0, 2).reshape(t // tm, 1, k * tm)


def _dispatch_kernel(zneed_ref, dest_ref, hp_ref, xs_ref, zero_ref, sem, zsem, *, sub, group):
    step = pl.program_id(0)
    tm = hp_ref.shape[0]
    n_zc = zneed_ref.shape[0]

    def zero_copy(j):
        return pltpu.make_async_copy(zero_ref, xs_ref.at[pl.ds(pl.multiple_of(j * sub, sub), sub), :], zsem)

    @pl.when(step == 0)
    def _():
        zero_ref[...] = jnp.zeros_like(zero_ref)

        def start(j, c):
            @pl.when(zneed_ref[j] != 0)
            def _():
                zero_copy(j).start()
            return c

        def wait(j, c):
            @pl.when(zneed_ref[j] != 0)
            def _():
                zero_copy(j).wait()
            return c

        lax.fori_loop(0, n_zc, start, 0)
        lax.fori_loop(0, n_zc, wait, 0)

    def issue(g, c):
        base = g * group
        for r in range(group):
            for k in range(TOP_K):
                d = dest_ref[0, 0, k * tm + base + r]
                pltpu.make_async_copy(hp_ref.at[pl.ds(base + r, 1), :], xs_ref.at[pl.ds(d, 1), :],
                                      sem).start(priority=k % 2)
        return c

    lax.fori_loop(0, tm // group, issue, 0)
    for _ in range(TOP_K):
        pltpu.make_async_copy(hp_ref, xs_ref.at[pl.ds(0, tm), :], sem).wait()


def _dispatch(zneed, dest_kt, hp, n_slots, sub, tm):
    t, half = hp.shape
    grid_spec = pltpu.PrefetchScalarGridSpec(
        num_scalar_prefetch=1,
        grid=(t // tm,),
        in_specs=[
            pl.BlockSpec((1, 1, tm * TOP_K), lambda i, *_: (i, 0, 0), memory_space=pltpu.SMEM),
            pl.BlockSpec((tm, half), lambda i, *_: (i, 0)),
        ],
        out_specs=pl.BlockSpec(memory_space=pl.ANY),
        scratch_shapes=[pltpu.VMEM((sub, half), U32), pltpu.SemaphoreType.DMA, pltpu.SemaphoreType.DMA],
    )
    return pl.pallas_call(
        functools.partial(_dispatch_kernel, sub=sub, group=min(DMA_GROUP, tm)),
        grid_spec=grid_spec,
        out_shape=jax.ShapeDtypeStruct((n_slots, half), U32),
        compiler_params=_params(("arbitrary",)),
        name="dispatch",
    )(zneed, _tile_slots(dest_kt, tm), hp)


def _gmm_kernel(plan_ref, x_ref, bias_ref, w_hbm, o_ref, wbuf, stage, sem, *rest, mode, kc, sub, nw):
    blk = pl.program_id(0)
    expert = plan_ref[0, blk]
    half_idx = plan_ref[1, blk]
    nxt = plan_ref[3, blk]
    valid = blk < plan_ref[7, 0]
    n_sub = jnp.where(valid, plan_ref[6, blk], 0)
    n_ch = w_hbm.shape[1] // kc

    def chunk_rows(k):
        return pl.ds(pl.multiple_of(k * kc, kc), kc)

    def chunk_copy(e, k, st):
        return pltpu.make_async_copy(w_hbm.at[e, chunk_rows(k), :], stage.at[st], sem.at[st])

    def land(e, k, dst):
        st = k % 2
        chunk_copy(e, k, st).wait()
        wbuf[dst, chunk_rows(k), :] = stage[st].astype(BF16)

        @pl.when(k + 2 < n_ch)
        def _():
            chunk_copy(e, k + 2, st).start()

    @pl.when(blk == 0)
    def _():
        chunk_copy(expert, 0, 0).start()
        chunk_copy(expert, 1, 1).start()
        lax.fori_loop(0, n_ch, lambda k, c: (land(expert, k, half_idx), c)[1], 0)

    @pl.when(valid & (plan_ref[2, blk] == 1) & (nxt >= 0))
    def _():
        chunk_copy(nxt, 0, 0).start()
        chunk_copy(nxt, 1, 1).start()

    if mode == "gate_up":
        xb_ref, = rest
        half = x_ref.shape[1]
        de = o_ref.shape[1]

        def run(rows):
            v = x_ref[rows, :]
            xb_ref[rows, :half] = _unpack_lo(v).astype(BF16)
            xb_ref[rows, half:] = _unpack_hi(v).astype(BF16)
            x = xb_ref[rows, :]
            for n in range(de // nw):
                cg = slice(n * nw, (n + 1) * nw)
                cu = slice(de + n * nw, de + (n + 1) * nw)
                gate = jnp.dot(x, wbuf[half_idx, :, cg], preferred_element_type=F32) + bias_ref[:, cg]
                up = jnp.dot(x, wbuf[half_idx, :, cu], preferred_element_type=F32) + bias_ref[:, cu]
                gate = jnp.minimum(gate, SWIGLU_LIMIT)
                up = jnp.clip(up, -SWIGLU_LIMIT, SWIGLU_LIMIT)
                act = (up + 1.0) * (gate * jax.nn.sigmoid(SWIGLU_ALPHA * gate))
                o_ref[rows, cg] = act.astype(o_ref.dtype)
    else:
        half = o_ref.shape[1]

        def run(rows):
            x = x_ref[rows, :]
            for n in range(half // nw):
                cl = slice(n * nw, (n + 1) * nw)
                ch = slice(half + n * nw, half + (n + 1) * nw)
                lo = jnp.dot(x, wbuf[half_idx, :, cl], preferred_element_type=F32) + bias_ref[:, cl]
                hi = jnp.dot(x, wbuf[half_idx, :, ch], preferred_element_type=F32) + bias_ref[:, ch]
                o_ref[rows, cl] = _pack_bf16_pair(lo, hi)

    full = x_ref.shape[0] // sub

    @pl.when(n_sub < full)
    def _():
        o_ref[...] = jnp.zeros_like(o_ref)

    @pl.when(n_sub == full)
    def _():
        run(slice(None))

    @pl.when((n_sub > 0) & (n_sub < full))
    def _():
        def body(i, c):
            run(pl.ds(pl.multiple_of(i * sub, sub), sub))
            return c
        lax.fori_loop(0, n_sub, body, 0)

    lax.fori_loop(plan_ref[4, blk], plan_ref[5, blk], lambda k, c: (land(nxt, k, 1 - half_idx), c)[1], 0)


def _gmm(plan, x, bias, w, mode, bm, sub, kc):
    n_slots = x.shape[0]
    n_exp, kdim, ndim = w.shape
    n_blk = n_slots // bm
    if mode == "gate_up":
        out_cols, out_dtype = ndim // 2, BF16
        extra = [pltpu.VMEM((bm, kdim), BF16)]
    else:
        out_cols, out_dtype = ndim // 2, U32
        extra = []

    def live(b, plan):
        return (jnp.minimum(b, plan[7, 0] - 1), 0)

    grid_spec = pltpu.PrefetchScalarGridSpec(
        num_scalar_prefetch=1,
        grid=(n_blk,),
        in_specs=[
            pl.BlockSpec((bm, x.shape[1]), live),
            pl.BlockSpec((None, 1, ndim), lambda b, plan: (plan[0, b], 0, 0)),
            pl.BlockSpec(memory_space=pl.ANY),
        ],
        out_specs=pl.BlockSpec((bm, out_cols), lambda b, plan: (b, 0)),
        scratch_shapes=[
            pltpu.VMEM((2, kdim, ndim), BF16),
            pltpu.VMEM((2, kc, ndim), F32),
            pltpu.SemaphoreType.DMA((2,)),
        ] + extra,
    )
    return pl.pallas_call(
        functools.partial(_gmm_kernel, mode=mode, kc=kc, sub=sub, nw=min(512, out_cols)),
        grid_spec=grid_spec,
        out_shape=jax.ShapeDtypeStruct((n_slots, out_cols), out_dtype),
        compiler_params=_params(("arbitrary",)),
        name="experts_" + mode,
    )(plan, x, bias, w)


def _expert_plan(counts, bm, sub, n_blk, n_ch):
    n_exp = counts.shape[0]
    ar = jnp.arange(n_exp, dtype=jnp.int32)
    nb_e = (counts + bm - 1) // bm
    blk_end = jnp.cumsum(nb_e)
    blk_start = blk_end - nb_e
    n_valid = blk_end[-1]
    blk = jnp.arange(n_blk, dtype=jnp.int32)
    last = jnp.minimum(blk, n_valid - 1)
    be = jnp.minimum(jnp.sum(blk_end[None, :] <= last[:, None], axis=1), n_exp - 1).astype(jnp.int32)
    mine = be[:, None] == ar[None, :]

    def of_block(per_expert):
        return jnp.sum(jnp.where(mine, per_expert[None, :], 0), axis=1)

    valid = blk < n_valid
    i_in = last - of_block(blk_start)
    nb_b = of_block(nb_e)
    nonempty = nb_e > 0
    ordinal = jnp.cumsum(nonempty.astype(jnp.int32)) - 1
    later = (ar[None, :] > ar[:, None]) & nonempty[None, :]
    nxt_e = jnp.min(jnp.where(later, ar[None, :], n_exp), axis=1)
    nxt_e = jnp.where(nxt_e >= n_exp, -1, nxt_e)
    nxt_b = of_block(nxt_e)
    has_next = valid & (nxt_b >= 0)
    lo = jnp.where(has_next, i_in * n_ch // nb_b, 0)
    hi = jnp.where(has_next, (i_in + 1) * n_ch // nb_b, 0)
    rows = jnp.where(valid, jnp.clip(of_block(counts) - i_in * bm, 0, bm), 0)
    nsub = (rows + sub - 1) // sub
    plan = jnp.stack([
        be, of_block(ordinal) % 2, (valid & (i_in == 0)).astype(jnp.int32), nxt_b, lo, hi, nsub,
        jnp.full((n_blk,), n_valid, jnp.int32),
    ]).astype(jnp.int32)
    per = bm // sub
    sub_rows = rows[:, None] - jnp.arange(per, dtype=jnp.int32)[None, :] * sub
    zneed = (sub_rows < sub).astype(jnp.int32).reshape(n_blk * per)
    return blk_start * bm, plan, zneed


def _combine_kernel(dcur_ref, dnxt_ref, x1_ref, rg_ref, gfin_ref, os_ref, ya_ref, yb_ref, gbuf_ref, sem,
                    *, final_norm, n_first, cw):
    step = pl.program_id(0)
    last = pl.num_programs(0) - 1
    tm = x1_ref.shape[0]
    half = gbuf_ref.shape[3]
    cur = step % 2
    n_cc = half // cw

    def token_copies(dref, r, buf):
        for k in range(TOP_K):
            d = dref[0, 0, k * tm + r]
            pltpu.make_async_copy(os_ref.at[pl.ds(d, 1), :], gbuf_ref.at[buf, k, pl.ds(r, 1), :],
                                  sem.at[buf]).start(priority=k % 2)

    def wait_tile(buf):
        for k in range(TOP_K):
            pltpu.make_async_copy(os_ref.at[pl.ds(0, tm), :], gbuf_ref.at[buf, k], sem.at[buf]).wait()

    @pl.when(step == 0)
    def _():
        def body(g, c):
            for r in range(DMA_GROUP):
                token_copies(dcur_ref, g * DMA_GROUP + r, 0)
            return c
        lax.fori_loop(0, tm // DMA_GROUP, body, 0)

    wait_tile(cur)

    def finish(y_ref):
        rg = rg_ref[...]
        ss = jnp.zeros((tm, 1), F32)
        share = tm // n_cc
        for c in range(n_cc):
            cl = slice(c * cw, (c + 1) * cw)
            ch = slice(half + c * cw, half + (c + 1) * cw)
            lo = x1_ref[:, cl]
            hi = x1_ref[:, ch]
            for k in range(TOP_K):
                g = rg[:, k:k + 1]
                v = gbuf_ref[cur, k, :, cl]
                lo = lo + g * _unpack_lo(v)
                hi = hi + g * _unpack_hi(v)
            if final_norm:
                ss = ss + jnp.sum(lo * lo, axis=-1, keepdims=True) + jnp.sum(hi * hi, axis=-1, keepdims=True)
            y_ref[:, cl] = lo
            y_ref[:, ch] = hi
            for r in range(c * share, (c + 1) * share):
                token_copies(dnxt_ref, r, 1 - cur)
        if final_norm:
            y_ref[...] = y_ref[...] * lax.rsqrt(ss / (2 * half) + EPS) * gfin_ref[...]

    @pl.when(step < n_first)
    def _():
        finish(ya_ref)

    @pl.when(step >= n_first)
    def _():
        finish(yb_ref)

    @pl.when(step == last)
    def _():
        wait_tile(1 - cur)


def _combine(dest_kt, x1, rg, gfin, os_, t_first, tm, final_norm):
    t, d = x1.shape
    half = d // 2
    n_tiles = t // tm
    n_first = t_first // tm
    dest3 = _tile_slots(dest_kt, tm)
    return pl.pallas_call(
        functools.partial(_combine_kernel, final_norm=final_norm, n_first=n_first, cw=min(256, half)),
        grid=(n_tiles,),
        in_specs=[
            pl.BlockSpec((1, 1, tm * TOP_K), lambda i: (i, 0, 0), memory_space=pltpu.SMEM),
            pl.BlockSpec((1, 1, tm * TOP_K), lambda i: (jnp.minimum(i + 1, n_tiles - 1), 0, 0),
                         memory_space=pltpu.SMEM),
            pl.BlockSpec((tm, d), lambda i: (i, 0)),
            pl.BlockSpec((tm, LANES), lambda i: (i, 0)),
            pl.BlockSpec((1, d), lambda i: (0, 0)),
            pl.BlockSpec(memory_space=pl.ANY),
        ],
        out_specs=_pair_specs(tm, d, n_first),
        out_shape=[jax.ShapeDtypeStruct((t_first, d), F32), jax.ShapeDtypeStruct((t - t_first, d), F32)],
        scratch_shapes=[pltpu.VMEM((2, TOP_K, tm, half), U32), pltpu.SemaphoreType.DMA((2,))],
        compiler_params=_params(("arbitrary",)),
        name="combine",
    )(dest3, dest3, x1, rg, gfin, os_)


def _tile(n, pref):
    while n % pref:
        pref //= 2
    return pref


def _layer(xa, xb, seq_a, seq_b, p, final_g):
    ta, d = xa.shape
    t = ta + xb.shape[0]
    dg = p["gmlp_ln_g"].shape[-1]
    chunk = p["gmlp_ws"].shape[-1]
    d_lru = p["conv_w"].shape[-1]
    heads_l, hd_l = p["lru_wr"].shape[1], p["lru_wr"].shape[2]
    n_exp = p["router_w"].shape[-1]
    assert p["w_down"].shape[1] == d, "expert width must equal the model width (shared weight-chunk plan)"

    unit = max(seq_a, seq_b)
    assert unit % seq_a == 0 and unit % seq_b == 0 and ta % unit == 0 and (t - ta) % unit == 0
    tm = _tile(math.gcd(ta, t - ta), 512)
    z = _inproj(xa, xb, p["mix_norm_g"][None], p["w_in"].astype(BF16), tm)

    bs_b = jnp.broadcast_to(p["gmlp_bs"][:, :, None], p["gmlp_ws"].shape).astype(F32)
    out_a = _gmlp(z, p["gmlp_ln_g"][None], p["gmlp_ln_b"][None], p["gmlp_ws"].astype(BF16), bs_b,
                  p["out_norm_a"][None], max(tm, chunk))

    wri = (0.5 * jnp.concatenate([p["lru_wr"], p["lru_wi"]], axis=-1)).astype(BF16)
    bri = 0.5 * jnp.concatenate([p["lru_br"].reshape(2, heads_l, 1, hd_l),
                                 p["lru_bi"].reshape(2, heads_l, 1, hd_l)], axis=-1)
    lam = p["lru_lam"].reshape(2, heads_l, 1, hd_l)
    unit_seq = jnp.asarray([seq_a] * (ta // unit) + [seq_b] * ((t - ta) // unit), jnp.int32)
    out_b = _rglru(z, unit_seq, unit, 2 * dg // hd_l, (2 * dg + d_lru) // hd_l,
                   p["conv_w"], p["conv_b"][None], wri, bri, lam, _tile(math.gcd(seq_a, seq_b), 512))

    rw = jnp.zeros((d, LANES), BF16).at[:, :n_exp].set(p["router_w"].astype(BF16))
    rb = jnp.full((1, LANES), NEG_BIG, F32).at[0, :n_exp].set(p["router_b"])
    x1, hp, rt, rg, cnt = _outproj(xa, xb, out_a, out_b, p["out_norm_b"][None], p["w_out"].astype(BF16),
                                   p["ffn_norm_g"][None], rw, rb, tm)

    m = t * TOP_K
    bm = _tile(m, ROW_BLOCK)
    sub = min(SUB_BLOCK, bm)
    kc = _tile(d, WEIGHT_CHUNK_ROWS)
    counts = cnt[0, :n_exp]
    n_blk = m // bm + n_exp
    n_slots = n_blk * bm
    pad_start, plan, zneed = _expert_plan(counts, bm, sub, n_blk, d // kc)
    top_e = rt[:TOP_K]
    base = jnp.sum(jnp.where(top_e[..., None] == jnp.arange(n_exp, dtype=jnp.int32), pad_start, 0), axis=-1)
    dest_kt = base + rt[TOP_K:]

    xs = _dispatch(zneed, dest_kt, hp, n_slots, sub, _tile(t, 1024))
    act = _gmm(plan, xs, p["b_gate_up"][:, None, :], p["w_gate_up"], "gate_up", bm, sub, kc)
    os_ = _gmm(plan, act, p["b_down"][:, None, :], p["w_down"], "down", bm, sub, kc)
    gfin = jnp.ones((1, d), F32) if final_g is None else final_g[None]
    return _combine(dest_kt, x1, rg, gfin, os_, ta, _tile(math.gcd(ta, t - ta), 256), final_g is not None)


_LAYER_KEYS = ("mix_norm_g", "w_in", "gmlp_ln_g", "gmlp_ln_b", "gmlp_ws", "gmlp_bs", "conv_w", "conv_b",
               "lru_wr", "lru_br", "lru_wi", "lru_bi", "lru_lam", "out_norm_a", "out_norm_b", "w_out",
               "ffn_norm_g", "router_w", "router_b", "w_gate_up", "b_gate_up", "w_down", "b_down")


def kernel(x_prompt, x_sample, mix_norm_g, w_in, gmlp_ln_g, gmlp_ln_b, gmlp_ws, gmlp_bs, conv_w, conv_b, lru_wr, lru_br, lru_wi, lru_bi, lru_lam, out_norm_a, out_norm_b, w_out, ffn_norm_g, router_w, router_b, w_gate_up, b_gate_up, w_down, b_down, final_norm_g):
    stacked = dict(zip(_LAYER_KEYS, (mix_norm_g, w_in, gmlp_ln_g, gmlp_ln_b, gmlp_ws, gmlp_bs, conv_w, conv_b,
                                     lru_wr, lru_br, lru_wi, lru_bi, lru_lam, out_norm_a, out_norm_b, w_out,
                                     ffn_norm_g, router_w, router_b, w_gate_up, b_gate_up, w_down, b_down)))
    depth = w_in.shape[0]
    bp, sp, d = x_prompt.shape
    bs, ss, _ = x_sample.shape
    xa = x_prompt.reshape(bp * sp, d)
    xb = x_sample.reshape(bs * ss, d)
    for layer in range(depth):
        p = {k: v[layer] for k, v in stacked.items()}
        xa, xb = _layer(xa, xb, sp, ss, p, final_norm_g if layer == depth - 1 else None)
    return xa.reshape(bp, sp, d), xb.reshape(bs, ss, d)
```

```python
import functools
import math

import jax
import jax.numpy as jnp
from jax import lax
from jax.experimental import pallas as pl
from jax.experimental.pallas import tpu as pltpu

TOP_K = 4
LRU_C = 8.0
SWIGLU_LIMIT = 7.0
SWIGLU_ALPHA = 1.702
EPS = 1e-6

LANES = 128
VMEM_LIMIT = 56 * 1024 * 1024
NEG_BIG = -1e30
ROW_BLOCK = 512
SUB_BLOCK = 128
WEIGHT_CHUNK_ROWS = 128
DMA_GROUP = 8
OUTPROJ_PIECES = 2

F32 = jnp.float32
BF16 = jnp.bfloat16
U32 = jnp.uint32


def _params(sem):
    return pltpu.CompilerParams(dimension_semantics=sem, vmem_limit_bytes=VMEM_LIMIT)


def _gelu(x):
    return 0.5 * x * (1.0 + jnp.tanh(0.7978845608028654 * (x + 0.044715 * (x * x * x))))


def _rms(x, g):
    return x * lax.rsqrt(jnp.mean(x * x, axis=-1, keepdims=True) + EPS) * g


def _pack_bf16_pair(lo, hi):
    lo_bits = lax.bitcast_convert_type(lo.astype(BF16).astype(F32), U32)
    hi_bits = lax.bitcast_convert_type(hi.astype(BF16).astype(F32), U32)
    return (lo_bits >> 16) | (hi_bits & jnp.uint32(0xFFFF0000))


def _unpack_lo(v):
    return lax.bitcast_convert_type(v << 16, F32)


def _unpack_hi(v):
    return lax.bitcast_convert_type(v & jnp.uint32(0xFFFF0000), F32)


def _pair_specs(tm, d, n_first):
    return [pl.BlockSpec((tm, d), lambda i: (jnp.minimum(i, n_first - 1), 0)),
            pl.BlockSpec((tm, d), lambda i: (jnp.maximum(i - n_first, 0), 0))]


def _pair_tile(xa_ref, xb_ref, n_first):
    return jnp.where(pl.program_id(0) < n_first, xa_ref[...], xb_ref[...])


def _inproj_kernel(xa_ref, xb_ref, g_ref, w_ref, lng_ref, lnb_ref, ws_ref, bs_ref, ga_ref, z_ref, oa_ref, mix_ref,
                   *, n_first, heads, chunk):
    xn = _rms(_pair_tile(xa_ref, xb_ref, n_first), g_ref[...]).astype(BF16)
    tm = xn.shape[0]
    dg = lng_ref.shape[1]
    hd = dg // heads

    def project(c0, width):
        return jnp.dot(xn, w_ref[:, c0:c0 + width], preferred_element_type=F32)

    v = _gelu(project(dg, dg))
    u = project(0, dg)
    mu = jnp.mean(v, axis=-1, keepdims=True)
    vc = v - mu
    var = jnp.mean(vc * vc, axis=-1, keepdims=True)
    vn = (vc * lax.rsqrt(var + EPS) * lng_ref[...] + lnb_ref[...]).astype(BF16)
    n_rest = z_ref.shape[1] // dg
    for n in range(n_rest):
        z_ref[:, n * dg:(n + 1) * dg] = project((2 + n) * dg, dg).astype(z_ref.dtype)
    for c in range(tm // chunk):
        rows = slice(c * chunk, (c + 1) * chunk)
        for h in range(heads):
            cols = slice(h * hd, (h + 1) * hd)
            mixed = jnp.dot(ws_ref[h], vn[rows, cols], preferred_element_type=F32) + bs_ref[h]
            mix_ref[rows, cols] = mixed
    out = _gelu(u) * mix_ref[...]
    oa_ref[...] = _rms(out, ga_ref[...]).astype(oa_ref.dtype)


def _inproj(xa, xb, g, w, ln_g, ln_b, ws, bs_b, ga, tm):
    d = xa.shape[1]
    t = xa.shape[0] + xb.shape[0]
    d_in = w.shape[1]
    heads, chunk, _ = ws.shape
    dg = ln_g.shape[1]
    assert (d_in - 2 * dg) % dg == 0
    n_first = xa.shape[0] // tm
    vec = pl.BlockSpec((1, dg), lambda i: (0, 0))
    return pl.pallas_call(
        functools.partial(_inproj_kernel, n_first=n_first, heads=heads, chunk=chunk),
        grid=(t // tm,),
        in_specs=_pair_specs(tm, d, n_first) + [
            pl.BlockSpec((1, d), lambda i: (0, 0)),
            pl.BlockSpec((d, d_in), lambda i: (0, 0), pipeline_mode=pl.Buffered(1)),
            vec, vec,
            pl.BlockSpec(ws.shape, lambda i: (0, 0, 0)),
            pl.BlockSpec(bs_b.shape, lambda i: (0, 0, 0)),
            vec,
        ],
        out_specs=[pl.BlockSpec((tm, d_in - 2 * dg), lambda i: (i, 0)), pl.BlockSpec((tm, dg), lambda i: (i, 0))],
        out_shape=[jax.ShapeDtypeStruct((t, d_in - 2 * dg), BF16), jax.ShapeDtypeStruct((t, dg), BF16)],
        scratch_shapes=[pltpu.VMEM((tm, dg), F32)],
        compiler_params=_params(("arbitrary",)),
        name="inproj_gmlp",
    )(xa, xb, g, w, ln_g, ln_b, ws, bs_b, ga)


def _rot8(x, s):
    n, w = x.shape
    return pltpu.roll(x.reshape(n // 8, 8, w), s, axis=1).reshape(n, w)


def _row_in_group(shape):
    return lax.broadcasted_iota(jnp.int32, shape, 0) & 7


def _shift_rows(x, s, fill, reverse, row):
    if reverse:
        return jnp.where(row >= 8 - s, fill, _rot8(x, 8 - s))
    return jnp.where(row < s, fill, _rot8(x, s))


def _lru_scan_chunk(a, b, h, reverse):
    row = _row_in_group(a.shape)
    for s in (1, 2, 4):
        b = a * _shift_rows(b, s, 0.0, reverse, row) + b
        a = a * _shift_rows(a, s, 1.0, reverse, row)
    n_grp = a.shape[0] // 8
    out = [None] * n_grp
    order = range(n_grp - 1, -1, -1) if reverse else range(n_grp)
    edge = 0 if reverse else 7
    for gi in order:
        rows = slice(gi * 8, gi * 8 + 8)
        hg = b[rows] + a[rows] * h
        out[gi] = hg
        h = hg[edge:edge + 1]
    return jnp.concatenate(out, axis=0), h


def _rglru_kernel(seqlen_ref, xr_ref, gr_ref, cw_ref, cb_ref, wri_ref, bri_ref, lam_ref, o_ref, hf_ref, xc_ref,
                  *, tc):
    rows, hd = xr_ref.shape
    n_ck = rows // tc
    halo = 16
    cw = cw_ref[...]
    cb = cb_ref[...]
    seq = seqlen_ref[pl.program_id(0)]

    def starts_seq(c):
        return lax.rem(c * tc, seq) == 0

    def ends_seq(c):
        return lax.rem((c + 1) * tc, seq) == 0

    def conv_chunk(c):
        t0 = pl.multiple_of(c * tc, tc)
        main = xr_ref[pl.ds(t0, tc), :].astype(F32)
        p0 = pl.multiple_of(jnp.maximum(t0 - halo, 0), halo)
        n0 = pl.multiple_of(jnp.minimum(t0 + tc, rows - halo), halo)
        prev = jnp.where(starts_seq(c), 0.0, xr_ref[pl.ds(p0, halo), :].astype(F32))
        nxt = jnp.where(ends_seq(c), 0.0, xr_ref[pl.ds(n0, halo), :].astype(F32))
        ext = jnp.concatenate([prev, main, nxt], axis=0)
        row = _row_in_group(main.shape)

        def earlier(s):
            r = _rot8(ext, s)
            return jnp.where(row < s, r[halo - 8:halo - 8 + tc], r[halo:halo + tc])

        r7 = _rot8(ext, 7)
        later = jnp.where(row >= 7, r7[halo + 8:halo + 8 + tc], r7[halo:halo + tc])
        acc = cb + cw[2:3] * main
        acc = acc + cw[0:1] * earlier(2)
        acc = acc + cw[1:2] * earlier(1)
        acc = acc + cw[3:4] * later
        return acc

    def direction(d, xc, h):
        g = jnp.tanh(jnp.dot(xc.astype(BF16), wri_ref[d], preferred_element_type=F32) + bri_ref[d])
        i = 0.5 * g[:, hd:] + 0.5
        nl = -lam_ref[d]
        softplus = jnp.maximum(nl, 0.0) + jnp.log1p(jnp.exp(-jnp.abs(nl)))
        half_c = (-0.5 * LRU_C) * softplus
        log_a = half_c * g[:, :hd] + half_c
        a = jnp.exp(log_a)
        b = jnp.sqrt(-jnp.tanh(log_a) * (a * a + 1.0)) * i * xc
        return _lru_scan_chunk(a, b, h, reverse=(d == 1))

    def fwd(c, h):
        t0 = pl.multiple_of(c * tc, tc)
        xc = conv_chunk(c)
        xc_ref[pl.ds(t0, tc), :] = xc
        hs, h = direction(0, xc, jnp.where(starts_seq(c), 0.0, h))
        hf_ref[pl.ds(t0, tc), :] = hs
        return h

    def bwd(k, h):
        c = n_ck - 1 - k
        t0 = pl.multiple_of(c * tc, tc)
        hs, h = direction(1, xc_ref[pl.ds(t0, tc), :], jnp.where(ends_seq(c), 0.0, h))
        tot = hf_ref[pl.ds(t0, tc), :] + hs
        o_ref[pl.ds(t0, tc), :] = (_gelu(gr_ref[pl.ds(t0, tc), :].astype(F32)) * tot).astype(o_ref.dtype)
        return h

    h0 = jnp.zeros((1, hd), F32)
    lax.fori_loop(0, n_ck, fwd, h0)
    lax.fori_loop(0, n_ck, bwd, h0)


def _rglru(z, unit_seq, unit, col_x, col_g, cw, cb, wri, bri, lam, tc):
    t = z.shape[0]
    _, heads, hd, _ = wri.shape
    grid_spec = pltpu.PrefetchScalarGridSpec(
        num_scalar_prefetch=1,
        grid=(t // unit, heads),
        in_specs=[
            pl.BlockSpec((unit, hd), lambda u, h, sl: (u, col_x + h)),
            pl.BlockSpec((unit, hd), lambda u, h, sl: (u, col_g + h)),
            pl.BlockSpec((cw.shape[0], hd), lambda u, h, sl: (0, h)),
            pl.BlockSpec((1, hd), lambda u, h, sl: (0, h)),
            pl.BlockSpec((2, None, hd, 2 * hd), lambda u, h, sl: (0, h, 0, 0)),
            pl.BlockSpec((2, None, 1, 2 * hd), lambda u, h, sl: (0, h, 0, 0)),
            pl.BlockSpec((2, None, 1, hd), lambda u, h, sl: (0, h, 0, 0)),
        ],
        out_specs=pl.BlockSpec((unit, hd), lambda u, h, sl: (u, h)),
        scratch_shapes=[pltpu.VMEM((unit, hd), F32), pltpu.VMEM((unit, hd), F32)],
    )
    return pl.pallas_call(
        functools.partial(_rglru_kernel, tc=tc),
        grid_spec=grid_spec,
        out_shape=jax.ShapeDtypeStruct((t, heads * hd), BF16),
        compiler_params=_params(("arbitrary", "arbitrary")),
        name="rglru",
    )(unit_seq, z, z, cw, cb, wri, bri, lam)


def _outproj_kernel(xa_ref, xb_ref, a_ref, b_ref, gb_ref, wo_ref, gf_ref, rw_ref, rb_ref,
                    x1_ref, hp_ref, rt_ref, rg_ref, cnt_ref, carry_ref, *, n_first):
    step = pl.program_id(0)

    @pl.when(step == 0)
    def _():
        carry_ref[...] = jnp.zeros_like(carry_ref)

    da = a_ref.shape[1]
    tm = a_ref.shape[0]
    x_tile = _pair_tile(xa_ref, xb_ref, n_first)

    def project(rows):
        obn = _rms(b_ref[rows, :].astype(F32), gb_ref[...]).astype(BF16)
        y = jnp.dot(a_ref[rows, :], wo_ref[:da], preferred_element_type=F32)
        return y + jnp.dot(obn, wo_ref[da:], preferred_element_type=F32)

    def route(rows, y, carry):
        n = rows.stop - rows.start
        x1 = x_tile[rows, :] + y
        x1_ref[rows, :] = x1
        hf = _rms(x1, gf_ref[...])
        half = hf.shape[1] // 2
        hp_ref[rows, :] = _pack_bf16_pair(hf[:, :half], hf[:, half:])

        logits = jnp.dot(hf.astype(BF16), rw_ref[...], preferred_element_type=F32) + rb_ref[...]
        lane = lax.broadcasted_iota(jnp.int32, logits.shape, 1)
        lane_f = lane.astype(F32)
        work = logits
        hits, ids, vals = [], [], []
        for _ in range(TOP_K):
            m = jnp.max(work, axis=-1, keepdims=True)
            sel = jnp.min(jnp.where(work == m, lane_f, float(LANES)), axis=-1, keepdims=True)
            hit = lane_f == sel
            hits.append(hit)
            ids.append(sel)
            vals.append(m)
            work = jnp.where(hit, -jnp.inf, work)
        exps = [jnp.exp(v - vals[0]) for v in vals]
        denom = exps[0]
        for e in exps[1:]:
            denom = denom + e

        onehot = hits[0]
        for hit in hits[1:]:
            onehot = onehot | hit
        cnt = onehot.astype(F32)
        rr = lax.broadcasted_iota(jnp.int32, (n, n), 0)
        cc = lax.broadcasted_iota(jnp.int32, (n, n), 1)
        tril = (rr > cc).astype(BF16)
        before = jnp.dot(tril, cnt.astype(BF16), preferred_element_type=F32) + carry

        ri = jnp.zeros(logits.shape, F32)
        rg = jnp.zeros(logits.shape, F32)
        for k in range(TOP_K):
            rank_k = jnp.sum(jnp.where(hits[k], before, 0.0), axis=-1, keepdims=True)
            ri = jnp.where(lane == k, ids[k], ri)
            ri = jnp.where(lane == TOP_K + k, rank_k, ri)
            rg = jnp.where(lane == k, exps[k] / denom, rg)
        rt_ref[:, rows] = ri.T[:2 * TOP_K, :].astype(jnp.int32)
        rg_ref[rows, :] = rg
        return carry + jnp.sum(cnt, axis=0, keepdims=True)

    n_piece = OUTPROJ_PIECES if tm % (OUTPROJ_PIECES * LANES) == 0 else 1
    pieces = [slice(i * tm // n_piece, (i + 1) * tm // n_piece) for i in range(n_piece)]
    carry = carry_ref[...]
    ys = [project(pieces[0])]
    for i in range(n_piece):
        if i + 1 < n_piece:
            ys.append(project(pieces[i + 1]))
        carry = route(pieces[i], ys[i], carry)
    carry_ref[...] = carry
    cnt_ref[...] = carry.astype(jnp.int32)


def _outproj(xa, xb, out_a, out_b, gb, wo, gf, rw, rb, tm):
    d = xa.shape[1]
    t = xa.shape[0] + xb.shape[0]
    n_first = xa.shape[0] // tm
    da = out_a.shape[1]
    db = out_b.shape[1]
    row = lambda i: (i, 0)
    fixed = lambda i: (0, 0)
    return pl.pallas_call(
        functools.partial(_outproj_kernel, n_first=n_first),
        grid=(t // tm,),
        in_specs=_pair_specs(tm, d, n_first) + [
            pl.BlockSpec((tm, da), row),
            pl.BlockSpec((tm, db), row),
            pl.BlockSpec((1, db), fixed),
            pl.BlockSpec(wo.shape, fixed, pipeline_mode=pl.Buffered(1)),
            pl.BlockSpec((1, d), fixed),
            pl.BlockSpec(rw.shape, fixed),
            pl.BlockSpec((1, LANES), fixed),
        ],
        out_specs=[
            pl.BlockSpec((tm, d), row),
            pl.BlockSpec((tm, d // 2), row),
            pl.BlockSpec((2 * TOP_K, tm), lambda i: (0, i)),
            pl.BlockSpec((tm, LANES), row),
            pl.BlockSpec((1, LANES), fixed),
        ],
        out_shape=[
            jax.ShapeDtypeStruct((t, d), F32),
            jax.ShapeDtypeStruct((t, d // 2), U32),
            jax.ShapeDtypeStruct((2 * TOP_K, t), jnp.int32),
            jax.ShapeDtypeStruct((t, LANES), F32),
            jax.ShapeDtypeStruct((1, LANES), jnp.int32),
        ],
        scratch_shapes=[pltpu.VMEM((1, LANES), F32)],
        compiler_params=_params(("arbitrary",)),
        name="outproj_router",
    )(xa, xb, out_a, out_b, gb, wo, gf, rw, rb)


def _tile_slots(dest_kt, tm):
    k, t = dest_kt.shape
    return dest_kt.reshape(k, t // tm, tm).transpose(1, 0, 2).reshape(t // tm, 1, k * tm)


def _dispatch_kernel(zneed_ref, dest_ref, hp_ref, xs_ref, zero_ref, sem, zsem, *, sub, group):
    step = pl.program_id(0)
    tm = hp_ref.shape[0]
    n_zc = zneed_ref.shape[0]

    def zero_copy(j):
        return pltpu.make_async_copy(zero_ref, xs_ref.at[pl.ds(pl.multiple_of(j * sub, sub), sub), :], zsem)

    @pl.when(step == 0)
    def _():
        zero_ref[...] = jnp.zeros_like(zero_ref)

        def start(j, c):
            @pl.when(zneed_ref[j] != 0)
            def _():
                zero_copy(j).start()
            return c

        def wait(j, c):
            @pl.when(zneed_ref[j] != 0)
            def _():
                zero_copy(j).wait()
            return c

        lax.fori_loop(0, n_zc, start, 0)
        lax.fori_loop(0, n_zc, wait, 0)

    def issue(g, c):
        base = g * group
        for r in range(group):
            for k in range(TOP_K):
                d = dest_ref[0, 0, k * tm + base + r]
                pltpu.make_async_copy(hp_ref.at[pl.ds(base + r, 1), :], xs_ref.at[pl.ds(d, 1), :],
                                      sem).start(priority=k % 2)
        return c

    lax.fori_loop(0, tm // group, issue, 0)
    for _ in range(TOP_K):
        pltpu.make_async_copy(hp_ref, xs_ref.at[pl.ds(0, tm), :], sem).wait()


def _dispatch(zneed, dest_kt, hp, n_slots, sub, tm):
    t, half = hp.shape
    grid_spec = pltpu.PrefetchScalarGridSpec(
        num_scalar_prefetch=1,
        grid=(t // tm,),
        in_specs=[
            pl.BlockSpec((1, 1, tm * TOP_K), lambda i, *_: (i, 0, 0), memory_space=pltpu.SMEM),
            pl.BlockSpec((tm, half), lambda i, *_: (i, 0)),
        ],
        out_specs=pl.BlockSpec(memory_space=pl.ANY),
        scratch_shapes=[pltpu.VMEM((sub, half), U32), pltpu.SemaphoreType.DMA, pltpu.SemaphoreType.DMA],
    )
    return pl.pallas_call(
        functools.partial(_dispatch_kernel, sub=sub, group=min(DMA_GROUP, tm)),
        grid_spec=grid_spec,
        out_shape=jax.ShapeDtypeStruct((n_slots, half), U32),
        compiler_params=_params(("arbitrary",)),
        name="dispatch",
    )(zneed, _tile_slots(dest_kt, tm), hp)


def _gmm_kernel(plan_ref, x_ref, bias_ref, w_hbm, o_ref, wbuf, stage, sem, *rest, mode, kc, sub, nw):
    blk = pl.program_id(0)
    expert = plan_ref[0, blk]
    half_idx = plan_ref[1, blk]
    nxt = plan_ref[3, blk]
    valid = blk < plan_ref[7, 0]
    n_sub = jnp.where(valid, plan_ref[6, blk], 0)
    n_ch = w_hbm.shape[1] // kc

    def chunk_rows(k):
        return pl.ds(pl.multiple_of(k * kc, kc), kc)

    def chunk_copy(e, k, st):
        return pltpu.make_async_copy(w_hbm.at[e, chunk_rows(k), :], stage.at[st], sem.at[st])

    def land(e, k, dst):
        st = k % 2
        chunk_copy(e, k, st).wait()
        wbuf[dst, chunk_rows(k), :] = stage[st].astype(BF16)

        @pl.when(k + 2 < n_ch)
        def _():
            chunk_copy(e, k + 2, st).start()

    @pl.when(blk == 0)
    def _():
        chunk_copy(expert, 0, 0).start()
        chunk_copy(expert, 1, 1).start()
        lax.fori_loop(0, n_ch, lambda k, c: (land(expert, k, half_idx), c)[1], 0)

    @pl.when(valid & (plan_ref[2, blk] == 1) & (nxt >= 0))
    def _():
        chunk_copy(nxt, 0, 0).start()
        chunk_copy(nxt, 1, 1).start()

    if mode == "gate_up":
        xb_ref, = rest
        half = x_ref.shape[1]
        de = o_ref.shape[1]

        def run(rows):
            v = x_ref[rows, :]
            xb_ref[rows, :half] = _unpack_lo(v).astype(BF16)
            xb_ref[rows, half:] = _unpack_hi(v).astype(BF16)
            x = xb_ref[rows, :]
            for n in range(de // nw):
                cg = slice(n * nw, (n + 1) * nw)
                cu = slice(de + n * nw, de + (n + 1) * nw)
                gate = jnp.dot(x, wbuf[half_idx, :, cg], preferred_element_type=F32) + bias_ref[:, cg]
                up = jnp.dot(x, wbuf[half_idx, :, cu], preferred_element_type=F32) + bias_ref[:, cu]
                gate = jnp.minimum(gate, SWIGLU_LIMIT)
                up = jnp.clip(up, -SWIGLU_LIMIT, SWIGLU_LIMIT)
                act = (up + 1.0) * (gate * jax.nn.sigmoid(SWIGLU_ALPHA * gate))
                o_ref[rows, cg] = act.astype(o_ref.dtype)
    else:
        half = o_ref.shape[1]

        def run(rows):
            x = x_ref[rows, :]
            for n in range(half // nw):
                cl = slice(n * nw, (n + 1) * nw)
                ch = slice(half + n * nw, half + (n + 1) * nw)
                lo = jnp.dot(x, wbuf[half_idx, :, cl], preferred_element_type=F32) + bias_ref[:, cl]
                hi = jnp.dot(x, wbuf[half_idx, :, ch], preferred_element_type=F32) + bias_ref[:, ch]
                o_ref[rows, cl] = _pack_bf16_pair(lo, hi)

    full = x_ref.shape[0] // sub

    @pl.when(n_sub < full)
    def _():
        o_ref[...] = jnp.zeros_like(o_ref)

    @pl.when(n_sub == full)
    def _():
        run(slice(None))

    @pl.when((n_sub > 0) & (n_sub < full))
    def _():
        def body(i, c):
            run(pl.ds(pl.multiple_of(i * sub, sub), sub))
            return c
        lax.fori_loop(0, n_sub, body, 0)

    lax.fori_loop(plan_ref[4, blk], plan_ref[5, blk], lambda k, c: (land(nxt, k, 1 - half_idx), c)[1], 0)


def _gmm(plan, x, bias, w, mode, bm, sub, kc):
    n_slots = x.shape[0]
    n_exp, kdim, ndim = w.shape
    n_blk = n_slots // bm
    if mode == "gate_up":
        out_cols, out_dtype = ndim // 2, BF16
        extra = [pltpu.VMEM((bm, kdim), BF16)]
    else:
        out_cols, out_dtype = ndim // 2, U32
        extra = []

    def live(b, plan):
        return (jnp.minimum(b, plan[7, 0] - 1), 0)

    grid_spec = pltpu.PrefetchScalarGridSpec(
        num_scalar_prefetch=1,
        grid=(n_blk,),
        in_specs=[
            pl.BlockSpec((bm, x.shape[1]), live),
            pl.BlockSpec((None, 1, ndim), lambda b, plan: (plan[0, b], 0, 0)),
            pl.BlockSpec(memory_space=pl.ANY),
        ],
        out_specs=pl.BlockSpec((bm, out_cols), lambda b, plan: (b, 0)),
        scratch_shapes=[
            pltpu.VMEM((2, kdim, ndim), BF16),
            pltpu.VMEM((2, kc, ndim), F32),
            pltpu.SemaphoreType.DMA((2,)),
        ] + extra,
    )
    return pl.pallas_call(
        functools.partial(_gmm_kernel, mode=mode, kc=kc, sub=sub, nw=min(512, out_cols)),
        grid_spec=grid_spec,
        out_shape=jax.ShapeDtypeStruct((n_slots, out_cols), out_dtype),
        compiler_params=_params(("arbitrary",)),
        name="experts_" + mode,
    )(plan, x, bias, w)


def _expert_plan(counts, bm, sub, n_blk, n_ch):
    n_exp = counts.shape[0]
    ar = jnp.arange(n_exp, dtype=jnp.int32)
    nb_e = (counts + bm - 1) // bm
    blk_end = jnp.cumsum(nb_e)
    blk_start = blk_end - nb_e
    n_valid = blk_end[-1]
    blk = jnp.arange(n_blk, dtype=jnp.int32)
    last = jnp.minimum(blk, n_valid - 1)
    be = jnp.minimum(jnp.sum(blk_end[None, :] <= last[:, None], axis=1), n_exp - 1).astype(jnp.int32)
    mine = be[:, None] == ar[None, :]

    def of_block(per_expert):
        return jnp.sum(jnp.where(mine, per_expert[None, :], 0), axis=1)

    valid = blk < n_valid
    i_in = last - of_block(blk_start)
    nb_b = of_block(nb_e)
    nonempty = nb_e > 0
    ordinal = jnp.cumsum(nonempty.astype(jnp.int32)) - 1
    later = (ar[None, :] > ar[:, None]) & nonempty[None, :]
    nxt_e = jnp.min(jnp.where(later, ar[None, :], n_exp), axis=1)
    nxt_e = jnp.where(nxt_e >= n_exp, -1, nxt_e)
    nxt_b = of_block(nxt_e)
    has_next = valid & (nxt_b >= 0)
    lo = jnp.where(has_next, i_in * n_ch // nb_b, 0)
    hi = jnp.where(has_next, (i_in + 1) * n_ch // nb_b, 0)
    rows = jnp.where(valid, jnp.clip(of_block(counts) - i_in * bm, 0, bm), 0)
    nsub = (rows + sub - 1) // sub
    plan = jnp.stack([
        be, of_block(ordinal) % 2, (valid & (i_in == 0)).astype(jnp.int32), nxt_b, lo, hi, nsub,
        jnp.full((n_blk,), n_valid, jnp.int32),
    ]).astype(jnp.int32)
    per = bm // sub
    sub_rows = rows[:, None] - jnp.arange(per, dtype=jnp.int32)[None, :] * sub
    zneed = (sub_rows < sub).astype(jnp.int32).reshape(n_blk * per)
    return blk_start * bm, plan, zneed


def _combine_kernel(dcur_ref, dnxt_ref, x1_ref, rg_ref, gfin_ref, os_ref, ya_ref, yb_ref, gbuf_ref, sem,
                    *, final_norm, n_first, cw):
    step = pl.program_id(0)
    last = pl.num_programs(0) - 1
    tm = x1_ref.shape[0]
    half = gbuf_ref.shape[3]
    cur = step % 2
    n_cc = half // cw

    def token_copies(dref, r, buf):
        for k in range(TOP_K):
            d = dref[0, 0, k * tm + r]
            pltpu.make_async_copy(os_ref.at[pl.ds(d, 1), :], gbuf_ref.at[buf, k, pl.ds(r, 1), :],
                                  sem.at[buf]).start(priority=k % 2)

    def wait_tile(buf):
        for k in range(TOP_K):
            pltpu.make_async_copy(os_ref.at[pl.ds(0, tm), :], gbuf_ref.at[buf, k], sem.at[buf]).wait()

    @pl.when(step == 0)
    def _():
        def body(g, c):
            for r in range(DMA_GROUP):
                token_copies(dcur_ref, g * DMA_GROUP + r, 0)
            return c
        lax.fori_loop(0, tm // DMA_GROUP, body, 0)

    wait_tile(cur)

    def finish(y_ref):
        rg = rg_ref[...]
        ss = jnp.zeros((tm, 1), F32)
        share = tm // n_cc
        for c in range(n_cc):
            cl = slice(c * cw, (c + 1) * cw)
            ch = slice(half + c * cw, half + (c + 1) * cw)
            lo = x1_ref[:, cl]
            hi = x1_ref[:, ch]
            for k in range(TOP_K):
                g = rg[:, k:k + 1]
                v = gbuf_ref[cur, k, :, cl]
                lo = lo + g * _unpack_lo(v)
                hi = hi + g * _unpack_hi(v)
            if final_norm:
                ss = ss + jnp.sum(lo * lo, axis=-1, keepdims=True) + jnp.sum(hi * hi, axis=-1, keepdims=True)
            y_ref[:, cl] = lo
            y_ref[:, ch] = hi
            for r in range(c * share, (c + 1) * share):
                token_copies(dnxt_ref, r, 1 - cur)
        if final_norm:
            y_ref[...] = y_ref[...] * lax.rsqrt(ss / (2 * half) + EPS) * gfin_ref[...]

    @pl.when(step < n_first)
    def _():
        finish(ya_ref)

    @pl.when(step >= n_first)
    def _():
        finish(yb_ref)

    @pl.when(step == last)
    def _():
        wait_tile(1 - cur)


def _combine(dest_kt, x1, rg, gfin, os_, t_first, tm, final_norm):
    t, d = x1.shape
    half = d // 2
    n_tiles = t // tm
    n_first = t_first // tm
    dest3 = _tile_slots(dest_kt, tm)
    return pl.pallas_call(
        functools.partial(_combine_kernel, final_norm=final_norm, n_first=n_first, cw=min(256, half)),
        grid=(n_tiles,),
        in_specs=[
            pl.BlockSpec((1, 1, tm * TOP_K), lambda i: (i, 0, 0), memory_space=pltpu.SMEM),
            pl.BlockSpec((1, 1, tm * TOP_K), lambda i: (jnp.minimum(i + 1, n_tiles - 1), 0, 0),
                         memory_space=pltpu.SMEM),
            pl.BlockSpec((tm, d), lambda i: (i, 0)),
            pl.BlockSpec((tm, LANES), lambda i: (i, 0)),
            pl.BlockSpec((1, d), lambda i: (0, 0)),
            pl.BlockSpec(memory_space=pl.ANY),
        ],
        out_specs=_pair_specs(tm, d, n_first),
        out_shape=[jax.ShapeDtypeStruct((t_first, d), F32), jax.ShapeDtypeStruct((t - t_first, d), F32)],
        scratch_shapes=[pltpu.VMEM((2, TOP_K, tm, half), U32), pltpu.SemaphoreType.DMA((2,))],
        compiler_params=_params(("arbitrary",)),
        name="combine",
    )(dest3, dest3, x1, rg, gfin, os_)


def _tile(n, pref):
    while n % pref:
        pref //= 2
    return pref


def _layer(xa, xb, seq_a, seq_b, p, final_g):
    ta, d = xa.shape
    t = ta + xb.shape[0]
    dg = p["gmlp_ln_g"].shape[-1]
    chunk = p["gmlp_ws"].shape[-1]
    d_lru = p["conv_w"].shape[-1]
    heads_l, hd_l = p["lru_wr"].shape[1], p["lru_wr"].shape[2]
    n_exp = p["router_w"].shape[-1]
    assert p["w_down"].shape[1] == d, "expert width must equal the model width (shared weight-chunk plan)"

    unit = max(seq_a, seq_b)
    assert unit % seq_a == 0 and unit % seq_b == 0 and ta % unit == 0 and (t - ta) % unit == 0
    tm = _tile(math.gcd(ta, t - ta), 512)
    assert tm % chunk == 0 and seq_a % chunk == 0 and seq_b % chunk == 0
    bs_b = jnp.broadcast_to(p["gmlp_bs"][:, :, None], p["gmlp_ws"].shape).astype(F32)
    z, out_a = _inproj(xa, xb, p["mix_norm_g"][None], p["w_in"].astype(BF16), p["gmlp_ln_g"][None],
                       p["gmlp_ln_b"][None], p["gmlp_ws"].astype(BF16), bs_b, p["out_norm_a"][None], tm)

    wri = (0.5 * jnp.concatenate([p["lru_wr"], p["lru_wi"]], axis=-1)).astype(BF16)
    bri = 0.5 * jnp.concatenate([p["lru_br"].reshape(2, heads_l, 1, hd_l),
                                 p["lru_bi"].reshape(2, heads_l, 1, hd_l)], axis=-1)
    lam = p["lru_lam"].reshape(2, heads_l, 1, hd_l)
    unit_seq = jnp.asarray([seq_a] * (ta // unit) + [seq_b] * ((t - ta) // unit), jnp.int32)
    out_b = _rglru(z, unit_seq, unit, 0, d_lru // hd_l,
                   p["conv_w"], p["conv_b"][None], wri, bri, lam, _tile(math.gcd(seq_a, seq_b), 512))

    rw = jnp.zeros((d, LANES), BF16).at[:, :n_exp].set(p["router_w"].astype(BF16))
    rb = jnp.full((1, LANES), NEG_BIG, F32).at[0, :n_exp].set(p["router_b"])
    x1, hp, rt, rg, cnt = _outproj(xa, xb, out_a, out_b, p["out_norm_b"][None], p["w_out"].astype(BF16),
                                   p["ffn_norm_g"][None], rw, rb, tm)

    m = t * TOP_K
    bm = _tile(m, ROW_BLOCK)
    sub = min(SUB_BLOCK, bm)
    kc = _tile(d, WEIGHT_CHUNK_ROWS)
    counts = cnt[0, :n_exp]
    n_blk = m // bm + n_exp
    n_slots = n_blk * bm
    pad_start, plan, zneed = _expert_plan(counts, bm, sub, n_blk, d // kc)
    top_e = rt[:TOP_K]
    base = jnp.sum(jnp.where(top_e[..., None] == jnp.arange(n_exp, dtype=jnp.int32), pad_start, 0), axis=-1)
    dest_kt = base + rt[TOP_K:]

    xs = _dispatch(zneed, dest_kt, hp, n_slots, sub, _tile(t, 1024))
    act = _gmm(plan, xs, p["b_gate_up"][:, None, :], p["w_gate_up"], "gate_up", bm, sub, kc)
    os_ = _gmm(plan, act, p["b_down"][:, None, :], p["w_down"], "down", bm, sub, kc)
    gfin = jnp.ones((1, d), F32) if final_g is None else final_g[None]
    return _combine(dest_kt, x1, rg, gfin, os_, ta, _tile(math.gcd(ta, t - ta), 256), final_g is not None)


_LAYER_KEYS = ("mix_norm_g", "w_in", "gmlp_ln_g", "gmlp_ln_b", "gmlp_ws", "gmlp_bs", "conv_w", "conv_b",
               "lru_wr", "lru_br", "lru_wi", "lru_bi", "lru_lam", "out_norm_a", "out_norm_b", "w_out",
               "ffn_norm_g", "router_w", "router_b", "w_gate_up", "b_gate_up", "w_down", "b_down")


def kernel(x_prompt, x_sample, mix_norm_g, w_in, gmlp_ln_g, gmlp_ln_b, gmlp_ws, gmlp_bs, conv_w, conv_b, lru_wr, lru_br, lru_wi, lru_bi, lru_lam, out_norm_a, out_norm_b, w_out, ffn_norm_g, router_w, router_b, w_gate_up, b_gate_up, w_down, b_down, final_norm_g):
    stacked = dict(zip(_LAYER_KEYS, (mix_norm_g, w_in, gmlp_ln_g, gmlp_ln_b, gmlp_ws, gmlp_bs, conv_w, conv_b,
                                     lru_wr, lru_br, lru_wi, lru_bi, lru_lam, out_norm_a, out_norm_b, w_out,
                                     ffn_norm_g, router_w, router_b, w_gate_up, b_gate_up, w_down, b_down)))
    depth = w_in.shape[0]
    bp, sp, d = x_prompt.shape
    bs, ss, _ = x_sample.shape
    xa = x_prompt.reshape(bp * sp, d)
    xb = x_sample.reshape(bs * ss, d)
    for layer in range(depth):
        p = {k: v[layer] for k, v in stacked.items()}
        xa, xb = _layer(xa, xb, sp, ss, p, final_norm_g if layer == depth - 1 else None)
    return xa.reshape(bp, sp, d), xb.reshape(bs, ss, d)
```

```python
import functools
import math

import jax
import jax.numpy as jnp
from jax import lax
from jax.experimental import pallas as pl
from jax.experimental.pallas import tpu as pltpu

TOP_K = 4
LRU_C = 8.0
SWIGLU_LIMIT = 7.0
SWIGLU_ALPHA = 1.702
EPS = 1e-6

LANES = 128
SUBLANES = 8
VMEM_LIMIT = 56 * 1024 * 1024
NEG_BIG = -1e30
ROW_BLOCK = 512
SUB_BLOCK = 128
WEIGHT_CHUNK_ROWS = 128
COMBINE_COLUMNS = 256
OUTPROJ_PIECES = 2

F32 = jnp.float32
BF16 = jnp.bfloat16
U32 = jnp.uint32


def _params(sem):
    return pltpu.CompilerParams(dimension_semantics=sem, vmem_limit_bytes=VMEM_LIMIT)


def _gelu(x):
    return 0.5 * x * (1.0 + jnp.tanh(0.7978845608028654 * (x + 0.044715 * (x * x * x))))


def _rms(x, g):
    return x * lax.rsqrt(jnp.mean(x * x, axis=-1, keepdims=True) + EPS) * g


def _pack_bf16_pair(lo, hi):
    lo_bits = lax.bitcast_convert_type(lo.astype(BF16).astype(F32), U32)
    hi_bits = lax.bitcast_convert_type(hi.astype(BF16).astype(F32), U32)
    return (lo_bits >> 16) | (hi_bits & jnp.uint32(0xFFFF0000))


def _unpack_lo(v):
    return lax.bitcast_convert_type(v << 16, F32)


def _unpack_hi(v):
    return lax.bitcast_convert_type(v & jnp.uint32(0xFFFF0000), F32)


def _pair_specs(tm, d, n_first):
    return [pl.BlockSpec((tm, d), lambda i: (jnp.minimum(i, n_first - 1), 0)),
            pl.BlockSpec((tm, d), lambda i: (jnp.maximum(i - n_first, 0), 0))]


def _pair_tile(xa_ref, xb_ref, n_first):
    return jnp.where(pl.program_id(0) < n_first, xa_ref[...], xb_ref[...])


def _inproj_kernel(xa_ref, xb_ref, g_ref, w_ref, lng_ref, lnb_ref, ws_ref, bs_ref, ga_ref, z_ref, oa_ref, mix_ref,
                   *, n_first, heads, chunk):
    xn = _rms(_pair_tile(xa_ref, xb_ref, n_first), g_ref[...]).astype(BF16)
    tm = xn.shape[0]
    dg = lng_ref.shape[1]
    hd = dg // heads

    def project(c0, width):
        return jnp.dot(xn, w_ref[:, c0:c0 + width], preferred_element_type=F32)

    v = _gelu(project(dg, dg))
    u = project(0, dg)
    mu = jnp.mean(v, axis=-1, keepdims=True)
    vc = v - mu
    var = jnp.mean(vc * vc, axis=-1, keepdims=True)
    vn = (vc * lax.rsqrt(var + EPS) * lng_ref[...] + lnb_ref[...]).astype(BF16)
    n_rest = z_ref.shape[1] // dg
    for n in range(n_rest):
        z_ref[:, n * dg:(n + 1) * dg] = project((2 + n) * dg, dg).astype(z_ref.dtype)
    for c in range(tm // chunk):
        rows = slice(c * chunk, (c + 1) * chunk)
        for h in range(heads):
            cols = slice(h * hd, (h + 1) * hd)
            mixed = jnp.dot(ws_ref[h], vn[rows, cols], preferred_element_type=F32) + bs_ref[h]
            mix_ref[rows, cols] = mixed
    out = _gelu(u) * mix_ref[...]
    oa_ref[...] = _rms(out, ga_ref[...]).astype(oa_ref.dtype)


def _inproj(xa, xb, g, w, ln_g, ln_b, ws, bs_b, ga, tm):
    d = xa.shape[1]
    t = xa.shape[0] + xb.shape[0]
    d_in = w.shape[1]
    heads, chunk, _ = ws.shape
    dg = ln_g.shape[1]
    assert (d_in - 2 * dg) % dg == 0
    n_first = xa.shape[0] // tm
    vec = pl.BlockSpec((1, dg), lambda i: (0, 0))
    return pl.pallas_call(
        functools.partial(_inproj_kernel, n_first=n_first, heads=heads, chunk=chunk),
        grid=(t // tm,),
        in_specs=_pair_specs(tm, d, n_first) + [
            pl.BlockSpec((1, d), lambda i: (0, 0)),
            pl.BlockSpec((d, d_in), lambda i: (0, 0), pipeline_mode=pl.Buffered(1)),
            vec, vec,
            pl.BlockSpec(ws.shape, lambda i: (0, 0, 0)),
            pl.BlockSpec(bs_b.shape, lambda i: (0, 0, 0)),
            vec,
        ],
        out_specs=[pl.BlockSpec((tm, d_in - 2 * dg), lambda i: (i, 0)), pl.BlockSpec((tm, dg), lambda i: (i, 0))],
        out_shape=[jax.ShapeDtypeStruct((t, d_in - 2 * dg), BF16), jax.ShapeDtypeStruct((t, dg), BF16)],
        scratch_shapes=[pltpu.VMEM((tm, dg), F32)],
        compiler_params=_params(("arbitrary",)),
        name="inproj_gmlp",
    )(xa, xb, g, w, ln_g, ln_b, ws, bs_b, ga)


def _rot8(x, s):
    n, w = x.shape
    return pltpu.roll(x.reshape(n // 8, 8, w), s, axis=1).reshape(n, w)


def _row_in_group(shape):
    return lax.broadcasted_iota(jnp.int32, shape, 0) & 7


def _shift_rows(x, s, fill, reverse, row):
    if reverse:
        return jnp.where(row >= 8 - s, fill, _rot8(x, 8 - s))
    return jnp.where(row < s, fill, _rot8(x, s))


def _lru_scan_chunk(a, b, h, reverse):
    row = _row_in_group(a.shape)
    for s in (1, 2, 4):
        b = a * _shift_rows(b, s, 0.0, reverse, row) + b
        a = a * _shift_rows(a, s, 1.0, reverse, row)
    n_grp = a.shape[0] // 8
    out = [None] * n_grp
    order = range(n_grp - 1, -1, -1) if reverse else range(n_grp)
    edge = 0 if reverse else 7
    for gi in order:
        rows = slice(gi * 8, gi * 8 + 8)
        hg = b[rows] + a[rows] * h
        out[gi] = hg
        h = hg[edge:edge + 1]
    return jnp.concatenate(out, axis=0), h


def _rglru_kernel(seqlen_ref, xr_ref, gr_ref, cw_ref, cb_ref, wri_ref, bri_ref, lam_ref, o_ref, hf_ref, xc_ref,
                  *, tc):
    rows, hd = xr_ref.shape
    n_ck = rows // tc
    halo = 16
    cw = cw_ref[...]
    cb = cb_ref[...]
    seq = seqlen_ref[pl.program_id(0)]

    def starts_seq(c):
        return lax.rem(c * tc, seq) == 0

    def ends_seq(c):
        return lax.rem((c + 1) * tc, seq) == 0

    def conv_chunk(c):
        t0 = pl.multiple_of(c * tc, tc)
        main = xr_ref[pl.ds(t0, tc), :].astype(F32)
        p0 = pl.multiple_of(jnp.maximum(t0 - halo, 0), halo)
        n0 = pl.multiple_of(jnp.minimum(t0 + tc, rows - halo), halo)
        prev = jnp.where(starts_seq(c), 0.0, xr_ref[pl.ds(p0, halo), :].astype(F32))
        nxt = jnp.where(ends_seq(c), 0.0, xr_ref[pl.ds(n0, halo), :].astype(F32))
        ext = jnp.concatenate([prev, main, nxt], axis=0)
        row = _row_in_group(main.shape)

        def earlier(s):
            r = _rot8(ext, s)
            return jnp.where(row < s, r[halo - 8:halo - 8 + tc], r[halo:halo + tc])

        r7 = _rot8(ext, 7)
        later = jnp.where(row >= 7, r7[halo + 8:halo + 8 + tc], r7[halo:halo + tc])
        acc = cb + cw[2:3] * main
        acc = acc + cw[0:1] * earlier(2)
        acc = acc + cw[1:2] * earlier(1)
        acc = acc + cw[3:4] * later
        return acc

    def direction(d, xc, h):
        g = jnp.tanh(jnp.dot(xc.astype(BF16), wri_ref[d], preferred_element_type=F32) + bri_ref[d])
        i = 0.5 * g[:, hd:] + 0.5
        nl = -lam_ref[d]
        softplus = jnp.maximum(nl, 0.0) + jnp.log1p(jnp.exp(-jnp.abs(nl)))
        half_c = (-0.5 * LRU_C) * softplus
        log_a = half_c * g[:, :hd] + half_c
        a = jnp.exp(log_a)
        b = jnp.sqrt(-jnp.tanh(log_a) * (a * a + 1.0)) * i * xc
        return _lru_scan_chunk(a, b, h, reverse=(d == 1))

    def fwd(c, h):
        t0 = pl.multiple_of(c * tc, tc)
        xc = conv_chunk(c)
        xc_ref[pl.ds(t0, tc), :] = xc
        hs, h = direction(0, xc, jnp.where(starts_seq(c), 0.0, h))
        hf_ref[pl.ds(t0, tc), :] = hs
        return h

    def bwd(k, h):
        c = n_ck - 1 - k
        t0 = pl.multiple_of(c * tc, tc)
        hs, h = direction(1, xc_ref[pl.ds(t0, tc), :], jnp.where(ends_seq(c), 0.0, h))
        tot = hf_ref[pl.ds(t0, tc), :] + hs
        o_ref[pl.ds(t0, tc), :] = (_gelu(gr_ref[pl.ds(t0, tc), :].astype(F32)) * tot).astype(o_ref.dtype)
        return h

    h0 = jnp.zeros((1, hd), F32)
    lax.fori_loop(0, n_ck, fwd, h0)
    lax.fori_loop(0, n_ck, bwd, h0)


def _rglru(z, unit_seq, unit, col_x, col_g, cw, cb, wri, bri, lam, tc):
    t = z.shape[0]
    _, heads, hd, _ = wri.shape
    grid_spec = pltpu.PrefetchScalarGridSpec(
        num_scalar_prefetch=1,
        grid=(t // unit, heads),
        in_specs=[
            pl.BlockSpec((unit, hd), lambda u, h, sl: (u, col_x + h)),
            pl.BlockSpec((unit, hd), lambda u, h, sl: (u, col_g + h)),
            pl.BlockSpec((cw.shape[0], hd), lambda u, h, sl: (0, h)),
            pl.BlockSpec((1, hd), lambda u, h, sl: (0, h)),
            pl.BlockSpec((2, None, hd, 2 * hd), lambda u, h, sl: (0, h, 0, 0)),
            pl.BlockSpec((2, None, 1, 2 * hd), lambda u, h, sl: (0, h, 0, 0)),
            pl.BlockSpec((2, None, 1, hd), lambda u, h, sl: (0, h, 0, 0)),
        ],
        out_specs=pl.BlockSpec((unit, hd), lambda u, h, sl: (u, h)),
        scratch_shapes=[pltpu.VMEM((unit, hd), F32), pltpu.VMEM((unit, hd), F32)],
    )
    return pl.pallas_call(
        functools.partial(_rglru_kernel, tc=tc),
        grid_spec=grid_spec,
        out_shape=jax.ShapeDtypeStruct((t, heads * hd), BF16),
        compiler_params=_params(("arbitrary", "arbitrary")),
        name="rglru",
    )(unit_seq, z, z, cw, cb, wri, bri, lam)


def _outproj_kernel(xa_ref, xb_ref, a_ref, b_ref, gb_ref, wo_ref, gf_ref, rw_ref, rb_ref,
                    x1_ref, hp_ref, rt_ref, rg_ref, cnt_ref, carry_ref, *, n_first):
    step = pl.program_id(0)

    @pl.when(step == 0)
    def _():
        carry_ref[...] = jnp.zeros_like(carry_ref)

    da = a_ref.shape[1]
    tm = a_ref.shape[0]
    x_tile = _pair_tile(xa_ref, xb_ref, n_first)

    def project(rows):
        obn = _rms(b_ref[rows, :].astype(F32), gb_ref[...]).astype(BF16)
        y = jnp.dot(a_ref[rows, :], wo_ref[:da], preferred_element_type=F32)
        return y + jnp.dot(obn, wo_ref[da:], preferred_element_type=F32)

    def route(rows, y, carry):
        n = rows.stop - rows.start
        x1 = x_tile[rows, :] + y
        x1_ref[rows, :] = x1
        hf = _rms(x1, gf_ref[...])
        half = hf.shape[1] // 2
        hp_ref[rows, :] = _pack_bf16_pair(hf[:, :half], hf[:, half:])

        logits = jnp.dot(hf.astype(BF16), rw_ref[...], preferred_element_type=F32) + rb_ref[...]
        lane = lax.broadcasted_iota(jnp.int32, logits.shape, 1)
        lane_f = lane.astype(F32)
        work = logits
        hits, ids, vals = [], [], []
        for _ in range(TOP_K):
            m = jnp.max(work, axis=-1, keepdims=True)
            sel = jnp.min(jnp.where(work == m, lane_f, float(LANES)), axis=-1, keepdims=True)
            hit = lane_f == sel
            hits.append(hit)
            ids.append(sel)
            vals.append(m)
            work = jnp.where(hit, -jnp.inf, work)
        exps = [jnp.exp(v - vals[0]) for v in vals]
        denom = exps[0]
        for e in exps[1:]:
            denom = denom + e

        onehot = hits[0]
        for hit in hits[1:]:
            onehot = onehot | hit
        cnt = onehot.astype(F32)
        rr = lax.broadcasted_iota(jnp.int32, (n, n), 0)
        cc = lax.broadcasted_iota(jnp.int32, (n, n), 1)
        tril = (rr > cc).astype(BF16)
        before = jnp.dot(tril, cnt.astype(BF16), preferred_element_type=F32) + carry

        ri = jnp.zeros(logits.shape, F32)
        rg = jnp.zeros(logits.shape, F32)
        for k in range(TOP_K):
            rank_k = jnp.sum(jnp.where(hits[k], before, 0.0), axis=-1, keepdims=True)
            ri = jnp.where(lane == k, ids[k], ri)
            ri = jnp.where(lane == TOP_K + k, rank_k, ri)
            rg = jnp.where(lane == k, exps[k] / denom, rg)
        rt_ref[:, rows] = ri.T[:2 * TOP_K, :].astype(jnp.int32)
        rg_ref[rows, :] = rg
        return carry + jnp.sum(cnt, axis=0, keepdims=True)

    n_piece = OUTPROJ_PIECES if tm % (OUTPROJ_PIECES * LANES) == 0 else 1
    pieces = [slice(i * tm // n_piece, (i + 1) * tm // n_piece) for i in range(n_piece)]
    carry = carry_ref[...]
    ys = [project(pieces[0])]
    for i in range(n_piece):
        if i + 1 < n_piece:
            ys.append(project(pieces[i + 1]))
        carry = route(pieces[i], ys[i], carry)
    carry_ref[...] = carry
    cnt_ref[...] = carry.astype(jnp.int32)


def _outproj(xa, xb, out_a, out_b, gb, wo, gf, rw, rb, tm):
    d = xa.shape[1]
    t = xa.shape[0] + xb.shape[0]
    n_first = xa.shape[0] // tm
    da = out_a.shape[1]
    db = out_b.shape[1]
    row = lambda i: (i, 0)
    fixed = lambda i: (0, 0)
    return pl.pallas_call(
        functools.partial(_outproj_kernel, n_first=n_first),
        grid=(t // tm,),
        in_specs=_pair_specs(tm, d, n_first) + [
            pl.BlockSpec((tm, da), row),
            pl.BlockSpec((tm, db), row),
            pl.BlockSpec((1, db), fixed),
            pl.BlockSpec(wo.shape, fixed, pipeline_mode=pl.Buffered(1)),
            pl.BlockSpec((1, d), fixed),
            pl.BlockSpec(rw.shape, fixed),
            pl.BlockSpec((1, LANES), fixed),
        ],
        out_specs=[
            pl.BlockSpec((tm, d), row),
            pl.BlockSpec((tm, d // 2), row),
            pl.BlockSpec((2 * TOP_K, tm), lambda i: (0, i)),
            pl.BlockSpec((tm, LANES), row),
            pl.BlockSpec((1, LANES), fixed),
        ],
        out_shape=[
            jax.ShapeDtypeStruct((t, d), F32),
            jax.ShapeDtypeStruct((t, d // 2), U32),
            jax.ShapeDtypeStruct((2 * TOP_K, t), jnp.int32),
            jax.ShapeDtypeStruct((t, LANES), F32),
            jax.ShapeDtypeStruct((1, LANES), jnp.int32),
        ],
        scratch_shapes=[pltpu.VMEM((1, LANES), F32)],
        compiler_params=_params(("arbitrary",)),
        name="outproj_router",
    )(xa, xb, out_a, out_b, gb, wo, gf, rw, rb)


def _tile_slots(dest_kt, tm):
    k, t = dest_kt.shape
    return dest_kt.reshape(k, t // tm, tm).transpose(1, 0, 2).reshape(t // tm, 1, k * tm)


def _dispatch_kernel(zneed_ref, dest_ref, hp_ref, xs_ref, zero_ref, sem, zsem, *, sub):
    step = pl.program_id(0)
    n_grp, group, _ = hp_ref.shape
    tm = n_grp * group
    n_zc = zneed_ref.shape[0]

    def zero_copy(j):
        return pltpu.make_async_copy(zero_ref, xs_ref.at[pl.ds(pl.multiple_of(j * sub, sub), sub), :], zsem)

    @pl.when(step == 0)
    def _():
        zero_ref[...] = jnp.zeros_like(zero_ref)

        def start(j, c):
            @pl.when(zneed_ref[j] != 0)
            def _():
                zero_copy(j).start()
            return c

        def wait(j, c):
            @pl.when(zneed_ref[j] != 0)
            def _():
                zero_copy(j).wait()
            return c

        lax.fori_loop(0, n_zc, start, 0)
        lax.fori_loop(0, n_zc, wait, 0)

    def issue(g, c):
        base = g * group
        for r in range(group):
            for k in range(TOP_K):
                d = dest_ref[0, 0, k * tm + base + r]
                pltpu.make_async_copy(hp_ref.at[g, pl.ds(r, 1), :], xs_ref.at[pl.ds(d, 1), :],
                                      sem).start(priority=k % 2)
        return c

    lax.fori_loop(0, n_grp, issue, 0)
    for _ in range(TOP_K):
        pltpu.make_async_copy(xs_ref.at[pl.ds(0, tm), :], xs_ref.at[pl.ds(0, tm), :], sem).wait()


def _dispatch(zneed, dest_kt, hp, n_slots, sub, tm):
    t, half = hp.shape
    grid_spec = pltpu.PrefetchScalarGridSpec(
        num_scalar_prefetch=1,
        grid=(t // tm,),
        in_specs=[
            pl.BlockSpec((1, 1, tm * TOP_K), lambda i, *_: (i, 0, 0), memory_space=pltpu.SMEM),
            pl.BlockSpec((tm // SUBLANES, SUBLANES, half), lambda i, *_: (i, 0, 0)),
        ],
        out_specs=pl.BlockSpec(memory_space=pl.ANY),
        scratch_shapes=[pltpu.VMEM((sub, half), U32), pltpu.SemaphoreType.DMA, pltpu.SemaphoreType.DMA],
    )
    return pl.pallas_call(
        functools.partial(_dispatch_kernel, sub=sub),
        grid_spec=grid_spec,
        out_shape=jax.ShapeDtypeStruct((n_slots, half), U32),
        compiler_params=_params(("arbitrary",)),
        name="dispatch",
    )(zneed, _tile_slots(dest_kt, tm), hp.reshape(t // SUBLANES, SUBLANES, half))


def _gmm_kernel(plan_ref, x_ref, bias_ref, w_hbm, o_ref, wbuf, stage, sem, *rest, mode, kc, sub, nw):
    blk = pl.program_id(0)
    expert = plan_ref[0, blk]
    half_idx = plan_ref[1, blk]
    nxt = plan_ref[3, blk]
    valid = blk < plan_ref[7, 0]
    n_sub = jnp.where(valid, plan_ref[6, blk], 0)
    n_ch = w_hbm.shape[1] // kc

    def chunk_rows(k):
        return pl.ds(pl.multiple_of(k * kc, kc), kc)

    def chunk_copy(e, k, st):
        return pltpu.make_async_copy(w_hbm.at[e, chunk_rows(k), :], stage.at[st], sem.at[st])

    def land(e, k, dst):
        st = k % 2
        chunk_copy(e, k, st).wait()
        wbuf[dst, chunk_rows(k), :] = stage[st].astype(BF16)

        @pl.when(k + 2 < n_ch)
        def _():
            chunk_copy(e, k + 2, st).start()

    @pl.when(blk == 0)
    def _():
        chunk_copy(expert, 0, 0).start()
        chunk_copy(expert, 1, 1).start()
        lax.fori_loop(0, n_ch, lambda k, c: (land(expert, k, half_idx), c)[1], 0)

    @pl.when(valid & (plan_ref[2, blk] == 1) & (nxt >= 0))
    def _():
        chunk_copy(nxt, 0, 0).start()
        chunk_copy(nxt, 1, 1).start()

    if mode == "gate_up":
        xb_ref, = rest
        half = x_ref.shape[1]
        de = o_ref.shape[1]

        def run(rows):
            v = x_ref[rows, :]
            xb_ref[rows, :half] = _unpack_lo(v).astype(BF16)
            xb_ref[rows, half:] = _unpack_hi(v).astype(BF16)
            x = xb_ref[rows, :]
            for n in range(de // nw):
                cg = slice(n * nw, (n + 1) * nw)
                cu = slice(de + n * nw, de + (n + 1) * nw)
                gate = jnp.dot(x, wbuf[half_idx, :, cg], preferred_element_type=F32) + bias_ref[:, cg]
                up = jnp.dot(x, wbuf[half_idx, :, cu], preferred_element_type=F32) + bias_ref[:, cu]
                gate = jnp.minimum(gate, SWIGLU_LIMIT)
                up = jnp.clip(up, -SWIGLU_LIMIT, SWIGLU_LIMIT)
                act = (up + 1.0) * (gate * jax.nn.sigmoid(SWIGLU_ALPHA * gate))
                o_ref[rows, cg] = act.astype(o_ref.dtype)
    else:
        half = o_ref.shape[1]

        def run(rows):
            x = x_ref[rows, :]
            for n in range(half // nw):
                cl = slice(n * nw, (n + 1) * nw)
                ch = slice(half + n * nw, half + (n + 1) * nw)
                lo = jnp.dot(x, wbuf[half_idx, :, cl], preferred_element_type=F32) + bias_ref[:, cl]
                hi = jnp.dot(x, wbuf[half_idx, :, ch], preferred_element_type=F32) + bias_ref[:, ch]
                o_ref[rows, cl] = _pack_bf16_pair(lo, hi)

    full = x_ref.shape[0] // sub

    @pl.when(n_sub < full)
    def _():
        o_ref[...] = jnp.zeros_like(o_ref)

    @pl.when(n_sub == full)
    def _():
        run(slice(None))

    @pl.when((n_sub > 0) & (n_sub < full))
    def _():
        def body(i, c):
            run(pl.ds(pl.multiple_of(i * sub, sub), sub))
            return c
        lax.fori_loop(0, n_sub, body, 0)

    lax.fori_loop(plan_ref[4, blk], plan_ref[5, blk], lambda k, c: (land(nxt, k, 1 - half_idx), c)[1], 0)


def _gmm(plan, x, bias, w, mode, bm, sub, kc):
    n_slots = x.shape[0]
    n_exp, kdim, ndim = w.shape
    n_blk = n_slots // bm
    if mode == "gate_up":
        out_cols, out_dtype = ndim // 2, BF16
        extra = [pltpu.VMEM((bm, kdim), BF16)]
    else:
        out_cols, out_dtype = ndim // 2, U32
        extra = []

    def live(b, plan):
        return (jnp.minimum(b, plan[7, 0] - 1), 0)

    grid_spec = pltpu.PrefetchScalarGridSpec(
        num_scalar_prefetch=1,
        grid=(n_blk,),
        in_specs=[
            pl.BlockSpec((bm, x.shape[1]), live),
            pl.BlockSpec((None, 1, ndim), lambda b, plan: (plan[0, b], 0, 0)),
            pl.BlockSpec(memory_space=pl.ANY),
        ],
        out_specs=pl.BlockSpec((bm, out_cols), lambda b, plan: (b, 0)),
        scratch_shapes=[
            pltpu.VMEM((2, kdim, ndim), BF16),
            pltpu.VMEM((2, kc, ndim), F32),
            pltpu.SemaphoreType.DMA((2,)),
        ] + extra,
    )
    return pl.pallas_call(
        functools.partial(_gmm_kernel, mode=mode, kc=kc, sub=sub, nw=min(512, out_cols)),
        grid_spec=grid_spec,
        out_shape=jax.ShapeDtypeStruct((n_slots, out_cols), out_dtype),
        compiler_params=_params(("arbitrary",)),
        name="experts_" + mode,
    )(plan, x, bias, w)


def _expert_plan(counts, bm, sub, n_blk, n_ch):
    n_exp = counts.shape[0]
    ar = jnp.arange(n_exp, dtype=jnp.int32)
    nb_e = (counts + bm - 1) // bm
    blk_end = jnp.cumsum(nb_e)
    blk_start = blk_end - nb_e
    n_valid = blk_end[-1]
    blk = jnp.arange(n_blk, dtype=jnp.int32)
    last = jnp.minimum(blk, n_valid - 1)
    be = jnp.minimum(jnp.sum(blk_end[None, :] <= last[:, None], axis=1), n_exp - 1).astype(jnp.int32)
    mine = be[:, None] == ar[None, :]

    def of_block(per_expert):
        return jnp.sum(jnp.where(mine, per_expert[None, :], 0), axis=1)

    valid = blk < n_valid
    i_in = last - of_block(blk_start)
    nb_b = of_block(nb_e)
    nonempty = nb_e > 0
    ordinal = jnp.cumsum(nonempty.astype(jnp.int32)) - 1
    later = (ar[None, :] > ar[:, None]) & nonempty[None, :]
    nxt_e = jnp.min(jnp.where(later, ar[None, :], n_exp), axis=1)
    nxt_e = jnp.where(nxt_e >= n_exp, -1, nxt_e)
    nxt_b = of_block(nxt_e)
    has_next = valid & (nxt_b >= 0)
    lo = jnp.where(has_next, i_in * n_ch // nb_b, 0)
    hi = jnp.where(has_next, (i_in + 1) * n_ch // nb_b, 0)
    rows = jnp.where(valid, jnp.clip(of_block(counts) - i_in * bm, 0, bm), 0)
    nsub = (rows + sub - 1) // sub
    plan = jnp.stack([
        be, of_block(ordinal) % 2, (valid & (i_in == 0)).astype(jnp.int32), nxt_b, lo, hi, nsub,
        jnp.full((n_blk,), n_valid, jnp.int32),
    ]).astype(jnp.int32)
    per = bm // sub
    sub_rows = rows[:, None] - jnp.arange(per, dtype=jnp.int32)[None, :] * sub
    zneed = (sub_rows < sub).astype(jnp.int32).reshape(n_blk * per)
    return blk_start * bm, plan, zneed


def _combine_kernel(dcur_ref, dnxt_ref, x1_ref, rg_ref, gfin_ref, os_ref, ya_ref, yb_ref, gbuf_ref, sem,
                    *, final_norm, n_first, cw):
    step = pl.program_id(0)
    last = pl.num_programs(0) - 1
    tm = x1_ref.shape[0]
    half = gbuf_ref.shape[4]
    cur = step % 2

    def token_copies(dref, grp, row, buf):
        for k in range(TOP_K):
            d = dref[0, 0, k * tm + grp * SUBLANES + row]
            pltpu.make_async_copy(os_ref.at[pl.ds(d, 1), :], gbuf_ref.at[buf, k, grp, pl.ds(row, 1), :],
                                  sem.at[buf]).start(priority=k % 2)

    def wait_tile(buf):
        for k in range(TOP_K):
            pltpu.make_async_copy(os_ref.at[pl.ds(0, tm), :], os_ref.at[pl.ds(0, tm), :], sem.at[buf]).wait()

    @pl.when(step == 0)
    def _():
        def body(g, c):
            for r in range(SUBLANES):
                token_copies(dcur_ref, g, r, 0)
            return c
        lax.fori_loop(0, tm // SUBLANES, body, 0)

    wait_tile(cur)

    def finish(y_ref):
        rg = rg_ref[...]
        ss = jnp.zeros((tm, 1), F32)
        n_cc = half // cw
        share = tm // n_cc
        for c in range(n_cc):
            cl = slice(c * cw, (c + 1) * cw)
            ch = slice(half + c * cw, half + (c + 1) * cw)
            lo = x1_ref[:, cl]
            hi = x1_ref[:, ch]
            for k in range(TOP_K):
                g = rg[:, k:k + 1]
                v = gbuf_ref[cur, k, :, :, cl].reshape(tm, cw)
                lo = lo + g * _unpack_lo(v)
                hi = hi + g * _unpack_hi(v)
            if final_norm:
                ss = ss + jnp.sum(lo * lo, axis=-1, keepdims=True) + jnp.sum(hi * hi, axis=-1, keepdims=True)
            y_ref[:, cl] = lo
            y_ref[:, ch] = hi
            for r in range(c * share, (c + 1) * share):
                token_copies(dnxt_ref, r // SUBLANES, r % SUBLANES, 1 - cur)
        if final_norm:
            y_ref[...] = y_ref[...] * lax.rsqrt(ss / (2 * half) + EPS) * gfin_ref[...]

    @pl.when(step < n_first)
    def _():
        finish(ya_ref)

    @pl.when(step >= n_first)
    def _():
        finish(yb_ref)

    @pl.when(step == last)
    def _():
        wait_tile(1 - cur)


def _combine(dest_kt, x1, rg, gfin, os_, t_first, tm, final_norm):
    t, d = x1.shape
    half = d // 2
    n_tiles = t // tm
    n_first = t_first // tm
    dest3 = _tile_slots(dest_kt, tm)
    return pl.pallas_call(
        functools.partial(_combine_kernel, final_norm=final_norm, n_first=n_first,
                          cw=min(COMBINE_COLUMNS, half)),
        grid=(n_tiles,),
        in_specs=[
            pl.BlockSpec((1, 1, tm * TOP_K), lambda i: (i, 0, 0), memory_space=pltpu.SMEM),
            pl.BlockSpec((1, 1, tm * TOP_K), lambda i: (jnp.minimum(i + 1, n_tiles - 1), 0, 0),
                         memory_space=pltpu.SMEM),
            pl.BlockSpec((tm, d), lambda i: (i, 0)),
            pl.BlockSpec((tm, LANES), lambda i: (i, 0)),
            pl.BlockSpec((1, d), lambda i: (0, 0)),
            pl.BlockSpec(memory_space=pl.ANY),
        ],
        out_specs=_pair_specs(tm, d, n_first),
        out_shape=[jax.ShapeDtypeStruct((t_first, d), F32), jax.ShapeDtypeStruct((t - t_first, d), F32)],
        scratch_shapes=[pltpu.VMEM((2, TOP_K, tm // SUBLANES, SUBLANES, half), U32),
                        pltpu.SemaphoreType.DMA((2,))],
        compiler_params=_params(("arbitrary",)),
        name="combine",
    )(dest3, dest3, x1, rg, gfin, os_)


def _tile(n, pref):
    while n % pref:
        pref //= 2
    return pref


def _layer(xa, xb, seq_a, seq_b, p, final_g):
    ta, d = xa.shape
    t = ta + xb.shape[0]
    dg = p["gmlp_ln_g"].shape[-1]
    chunk = p["gmlp_ws"].shape[-1]
    d_lru = p["conv_w"].shape[-1]
    heads_l, hd_l = p["lru_wr"].shape[1], p["lru_wr"].shape[2]
    n_exp = p["router_w"].shape[-1]
    assert p["w_down"].shape[1] == d, "expert width must equal the model width (shared weight-chunk plan)"

    unit = max(seq_a, seq_b)
    assert unit % seq_a == 0 and unit % seq_b == 0 and ta % unit == 0 and (t - ta) % unit == 0
    tm = _tile(math.gcd(ta, t - ta), 512)
    assert tm % chunk == 0 and seq_a % chunk == 0 and seq_b % chunk == 0
    bs_b = jnp.broadcast_to(p["gmlp_bs"][:, :, None], p["gmlp_ws"].shape).astype(F32)
    z, out_a = _inproj(xa, xb, p["mix_norm_g"][None], p["w_in"].astype(BF16), p["gmlp_ln_g"][None],
                       p["gmlp_ln_b"][None], p["gmlp_ws"].astype(BF16), bs_b, p["out_norm_a"][None], tm)

    wri = (0.5 * jnp.concatenate([p["lru_wr"], p["lru_wi"]], axis=-1)).astype(BF16)
    bri = 0.5 * jnp.concatenate([p["lru_br"].reshape(2, heads_l, 1, hd_l),
                                 p["lru_bi"].reshape(2, heads_l, 1, hd_l)], axis=-1)
    lam = p["lru_lam"].reshape(2, heads_l, 1, hd_l)
    unit_seq = jnp.asarray([seq_a] * (ta // unit) + [seq_b] * ((t - ta) // unit), jnp.int32)
    out_b = _rglru(z, unit_seq, unit, 0, d_lru // hd_l,
                   p["conv_w"], p["conv_b"][None], wri, bri, lam, _tile(math.gcd(seq_a, seq_b), 512))

    rw = jnp.zeros((d, LANES), BF16).at[:, :n_exp].set(p["router_w"].astype(BF16))
    rb = jnp.full((1, LANES), NEG_BIG, F32).at[0, :n_exp].set(p["router_b"])
    x1, hp, rt, rg, cnt = _outproj(xa, xb, out_a, out_b, p["out_norm_b"][None], p["w_out"].astype(BF16),
                                   p["ffn_norm_g"][None], rw, rb, tm)

    m = t * TOP_K
    bm = _tile(m, ROW_BLOCK)
    sub = min(SUB_BLOCK, bm)
    kc = _tile(d, WEIGHT_CHUNK_ROWS)
    counts = cnt[0, :n_exp]
    n_blk = m // bm + n_exp
    n_slots = n_blk * bm
    pad_start, plan, zneed = _expert_plan(counts, bm, sub, n_blk, d // kc)
    top_e = rt[:TOP_K]
    base = jnp.sum(jnp.where(top_e[..., None] == jnp.arange(n_exp, dtype=jnp.int32), pad_start, 0), axis=-1)
    dest_kt = base + rt[TOP_K:]

    xs = _dispatch(zneed, dest_kt, hp, n_slots, sub, _tile(t, 1024))
    act = _gmm(plan, xs, p["b_gate_up"][:, None, :], p["w_gate_up"], "gate_up", bm, sub, kc)
    os_ = _gmm(plan, act, p["b_down"][:, None, :], p["w_down"], "down", bm, sub, kc)
    gfin = jnp.ones((1, d), F32) if final_g is None else final_g[None]
    return _combine(dest_kt, x1, rg, gfin, os_, ta, _tile(math.gcd(ta, t - ta), 256), final_g is not None)


_LAYER_KEYS = ("mix_norm_g", "w_in", "gmlp_ln_g", "gmlp_ln_b", "gmlp_ws", "gmlp_bs", "conv_w", "conv_b",
               "lru_wr", "lru_br", "lru_wi", "lru_bi", "lru_lam", "out_norm_a", "out_norm_b", "w_out",
               "ffn_norm_g", "router_w", "router_b", "w_gate_up", "b_gate_up", "w_down", "b_down")


def kernel(x_prompt, x_sample, mix_norm_g, w_in, gmlp_ln_g, gmlp_ln_b, gmlp_ws, gmlp_bs, conv_w, conv_b, lru_wr, lru_br, lru_wi, lru_bi, lru_lam, out_norm_a, out_norm_b, w_out, ffn_norm_g, router_w, router_b, w_gate_up, b_gate_up, w_down, b_down, final_norm_g):
    stacked = dict(zip(_LAYER_KEYS, (mix_norm_g, w_in, gmlp_ln_g, gmlp_ln_b, gmlp_ws, gmlp_bs, conv_w, conv_b,
                                     lru_wr, lru_br, lru_wi, lru_bi, lru_lam, out_norm_a, out_norm_b, w_out,
                                     ffn_norm_g, router_w, router_b, w_gate_up, b_gate_up, w_down, b_down)))
    depth = w_in.shape[0]
    bp, sp, d = x_prompt.shape
    bs, ss, _ = x_sample.shape
    xa = x_prompt.reshape(bp * sp, d)
    xb = x_sample.reshape(bs * ss, d)
    for layer in range(depth):
        p = {k: v[layer] for k, v in stacked.items()}
        xa, xb = _layer(xa, xb, sp, ss, p, final_norm_g if layer == depth - 1 else None)
    return xa.reshape(bp, sp, d), xb.reshape(bs, ss, d)
```

```python
import functools
import math

import jax
import jax.numpy as jnp
from jax import lax
from jax.experimental import pallas as pl
from jax.experimental.pallas import tpu as pltpu

TOP_K = 4
LRU_C = 8.0
SWIGLU_LIMIT = 7.0
SWIGLU_ALPHA = 1.702
EPS = 1e-6

LANES = 128
SUBLANES = 8
VMEM_LIMIT = 56 * 1024 * 1024
NEG_BIG = -1e30
ROW_BLOCK = 512
SUB_BLOCK = 128
WEIGHT_CHUNK_ROWS = 128
EXPERT_DOT_COLUMNS = 512
COMBINE_COLUMNS = 256
OUTPROJ_PIECES = 2

F32 = jnp.float32
BF16 = jnp.bfloat16
U32 = jnp.uint32


def _params(sem):
    return pltpu.CompilerParams(dimension_semantics=sem, vmem_limit_bytes=VMEM_LIMIT)


def _gelu(x):
    return 0.5 * x * (1.0 + jnp.tanh(0.7978845608028654 * (x + 0.044715 * (x * x * x))))


def _rms(x, g):
    return x * lax.rsqrt(jnp.mean(x * x, axis=-1, keepdims=True) + EPS) * g


def _pack_bf16_pair(lo, hi):
    lo_bits = lax.bitcast_convert_type(lo.astype(BF16).astype(F32), U32)
    hi_bits = lax.bitcast_convert_type(hi.astype(BF16).astype(F32), U32)
    return (lo_bits >> 16) | (hi_bits & jnp.uint32(0xFFFF0000))


def _unpack_lo(v):
    return lax.bitcast_convert_type(v << 16, F32)


def _unpack_hi(v):
    return lax.bitcast_convert_type(v & jnp.uint32(0xFFFF0000), F32)


def _pair_specs(tm, d, n_first):
    return [pl.BlockSpec((tm, d), lambda i: (jnp.minimum(i, n_first - 1), 0)),
            pl.BlockSpec((tm, d), lambda i: (jnp.maximum(i - n_first, 0), 0))]


def _pair_tile(xa_ref, xb_ref, n_first):
    return jnp.where(pl.program_id(0) < n_first, xa_ref[...], xb_ref[...])


def _inproj_kernel(xa_ref, xb_ref, g_ref, w_ref, lng_ref, lnb_ref, ws_ref, bs_ref, ga_ref, z_ref, oa_ref, mix_ref,
                   *, n_first, heads, chunk):
    xn = _rms(_pair_tile(xa_ref, xb_ref, n_first), g_ref[...]).astype(BF16)
    tm = xn.shape[0]
    dg = lng_ref.shape[1]
    hd = dg // heads

    def project(c0, width):
        return jnp.dot(xn, w_ref[:, c0:c0 + width], preferred_element_type=F32)

    v = _gelu(project(dg, dg))
    u = project(0, dg)
    mu = jnp.mean(v, axis=-1, keepdims=True)
    vc = v - mu
    var = jnp.mean(vc * vc, axis=-1, keepdims=True)
    vn = (vc * lax.rsqrt(var + EPS) * lng_ref[...] + lnb_ref[...]).astype(BF16)
    n_rest = z_ref.shape[1] // dg
    for n in range(n_rest):
        z_ref[:, n * dg:(n + 1) * dg] = project((2 + n) * dg, dg).astype(z_ref.dtype)
    for c in range(tm // chunk):
        rows = slice(c * chunk, (c + 1) * chunk)
        for h in range(heads):
            cols = slice(h * hd, (h + 1) * hd)
            mixed = jnp.dot(ws_ref[h], vn[rows, cols], preferred_element_type=F32) + bs_ref[h]
            mix_ref[rows, cols] = mixed
    out = _gelu(u) * mix_ref[...]
    oa_ref[...] = _rms(out, ga_ref[...]).astype(oa_ref.dtype)


def _inproj(xa, xb, g, w, ln_g, ln_b, ws, bs_b, ga, tm):
    d = xa.shape[1]
    t = xa.shape[0] + xb.shape[0]
    d_in = w.shape[1]
    heads, chunk, _ = ws.shape
    dg = ln_g.shape[1]
    assert (d_in - 2 * dg) % dg == 0
    n_first = xa.shape[0] // tm
    vec = pl.BlockSpec((1, dg), lambda i: (0, 0))
    return pl.pallas_call(
        functools.partial(_inproj_kernel, n_first=n_first, heads=heads, chunk=chunk),
        grid=(t // tm,),
        in_specs=_pair_specs(tm, d, n_first) + [
            pl.BlockSpec((1, d), lambda i: (0, 0)),
            pl.BlockSpec((d, d_in), lambda i: (0, 0), pipeline_mode=pl.Buffered(1)),
            vec, vec,
            pl.BlockSpec(ws.shape, lambda i: (0, 0, 0)),
            pl.BlockSpec(bs_b.shape, lambda i: (0, 0, 0)),
            vec,
        ],
        out_specs=[pl.BlockSpec((tm, d_in - 2 * dg), lambda i: (i, 0)), pl.BlockSpec((tm, dg), lambda i: (i, 0))],
        out_shape=[jax.ShapeDtypeStruct((t, d_in - 2 * dg), BF16), jax.ShapeDtypeStruct((t, dg), BF16)],
        scratch_shapes=[pltpu.VMEM((tm, dg), F32)],
        compiler_params=_params(("arbitrary",)),
        name="inproj_gmlp",
    )(xa, xb, g, w, ln_g, ln_b, ws, bs_b, ga)


def _rot8(x, s):
    n, w = x.shape
    return pltpu.roll(x.reshape(n // 8, 8, w), s, axis=1).reshape(n, w)


def _row_in_group(shape):
    return lax.broadcasted_iota(jnp.int32, shape, 0) & 7


def _shift_rows(x, s, fill, reverse, row):
    if reverse:
        return jnp.where(row >= 8 - s, fill, _rot8(x, 8 - s))
    return jnp.where(row < s, fill, _rot8(x, s))


def _lru_scan_chunk(a, b, h, reverse):
    row = _row_in_group(a.shape)
    for s in (1, 2, 4):
        b = a * _shift_rows(b, s, 0.0, reverse, row) + b
        a = a * _shift_rows(a, s, 1.0, reverse, row)
    n_grp = a.shape[0] // 8
    out = [None] * n_grp
    order = range(n_grp - 1, -1, -1) if reverse else range(n_grp)
    edge = 0 if reverse else 7
    for gi in order:
        rows = slice(gi * 8, gi * 8 + 8)
        hg = b[rows] + a[rows] * h
        out[gi] = hg
        h = hg[edge:edge + 1]
    return jnp.concatenate(out, axis=0), h


def _rglru_kernel(seqlen_ref, xr_ref, gr_ref, cw_ref, cb_ref, wri_ref, bri_ref, lam_ref, o_ref, hf_ref, xc_ref,
                  *, tc):
    rows, hd = xr_ref.shape
    n_ck = rows // tc
    halo = 16
    cw = cw_ref[...]
    cb = cb_ref[...]
    seq = seqlen_ref[pl.program_id(0)]

    def starts_seq(c):
        return lax.rem(c * tc, seq) == 0

    def ends_seq(c):
        return lax.rem((c + 1) * tc, seq) == 0

    def conv_chunk(c):
        t0 = pl.multiple_of(c * tc, tc)
        main = xr_ref[pl.ds(t0, tc), :].astype(F32)
        p0 = pl.multiple_of(jnp.maximum(t0 - halo, 0), halo)
        n0 = pl.multiple_of(jnp.minimum(t0 + tc, rows - halo), halo)
        prev = jnp.where(starts_seq(c), 0.0, xr_ref[pl.ds(p0, halo), :].astype(F32))
        nxt = jnp.where(ends_seq(c), 0.0, xr_ref[pl.ds(n0, halo), :].astype(F32))
        ext = jnp.concatenate([prev, main, nxt], axis=0)
        row = _row_in_group(main.shape)

        def earlier(s):
            r = _rot8(ext, s)
            return jnp.where(row < s, r[halo - 8:halo - 8 + tc], r[halo:halo + tc])

        r7 = _rot8(ext, 7)
        later = jnp.where(row >= 7, r7[halo + 8:halo + 8 + tc], r7[halo:halo + tc])
        acc = cb + cw[2:3] * main
        acc = acc + cw[0:1] * earlier(2)
        acc = acc + cw[1:2] * earlier(1)
        acc = acc + cw[3:4] * later
        return acc

    def direction(d, xc, h):
        g = jnp.tanh(jnp.dot(xc.astype(BF16), wri_ref[d], preferred_element_type=F32) + bri_ref[d])
        i = 0.5 * g[:, hd:] + 0.5
        nl = -lam_ref[d]
        softplus = jnp.maximum(nl, 0.0) + jnp.log1p(jnp.exp(-jnp.abs(nl)))
        half_c = (-0.5 * LRU_C) * softplus
        log_a = half_c * g[:, :hd] + half_c
        a = jnp.exp(log_a)
        b = jnp.sqrt(-jnp.tanh(log_a) * (a * a + 1.0)) * i * xc
        return _lru_scan_chunk(a, b, h, reverse=(d == 1))

    def fwd(c, h):
        t0 = pl.multiple_of(c * tc, tc)
        xc = conv_chunk(c)
        xc_ref[pl.ds(t0, tc), :] = xc
        hs, h = direction(0, xc, jnp.where(starts_seq(c), 0.0, h))
        hf_ref[pl.ds(t0, tc), :] = hs
        return h

    def bwd(k, h):
        c = n_ck - 1 - k
        t0 = pl.multiple_of(c * tc, tc)
        hs, h = direction(1, xc_ref[pl.ds(t0, tc), :], jnp.where(ends_seq(c), 0.0, h))
        tot = hf_ref[pl.ds(t0, tc), :] + hs
        o_ref[pl.ds(t0, tc), :] = (_gelu(gr_ref[pl.ds(t0, tc), :].astype(F32)) * tot).astype(o_ref.dtype)
        return h

    h0 = jnp.zeros((1, hd), F32)
    lax.fori_loop(0, n_ck, fwd, h0)
    lax.fori_loop(0, n_ck, bwd, h0)


def _rglru(z, unit_seq, unit, col_x, col_g, cw, cb, wri, bri, lam, tc):
    t = z.shape[0]
    _, heads, hd, _ = wri.shape
    grid_spec = pltpu.PrefetchScalarGridSpec(
        num_scalar_prefetch=1,
        grid=(t // unit, heads),
        in_specs=[
            pl.BlockSpec((unit, hd), lambda u, h, sl: (u, col_x + h)),
            pl.BlockSpec((unit, hd), lambda u, h, sl: (u, col_g + h)),
            pl.BlockSpec((cw.shape[0], hd), lambda u, h, sl: (0, h)),
            pl.BlockSpec((1, hd), lambda u, h, sl: (0, h)),
            pl.BlockSpec((2, None, hd, 2 * hd), lambda u, h, sl: (0, h, 0, 0)),
            pl.BlockSpec((2, None, 1, 2 * hd), lambda u, h, sl: (0, h, 0, 0)),
            pl.BlockSpec((2, None, 1, hd), lambda u, h, sl: (0, h, 0, 0)),
        ],
        out_specs=pl.BlockSpec((unit, hd), lambda u, h, sl: (u, h)),
        scratch_shapes=[pltpu.VMEM((unit, hd), F32), pltpu.VMEM((unit, hd), F32)],
    )
    return pl.pallas_call(
        functools.partial(_rglru_kernel, tc=tc),
        grid_spec=grid_spec,
        out_shape=jax.ShapeDtypeStruct((t, heads * hd), BF16),
        compiler_params=_params(("arbitrary", "arbitrary")),
        name="rglru",
    )(unit_seq, z, z, cw, cb, wri, bri, lam)


def _outproj_kernel(xa_ref, xb_ref, a_ref, b_ref, gb_ref, wo_ref, gf_ref, rw_ref, rb_ref,
                    x1_ref, hp_ref, rt_ref, rg_ref, cnt_ref, carry_ref, *, n_first):
    step = pl.program_id(0)

    @pl.when(step == 0)
    def _():
        carry_ref[...] = jnp.zeros_like(carry_ref)

    da = a_ref.shape[1]
    tm = a_ref.shape[0]
    x_tile = _pair_tile(xa_ref, xb_ref, n_first)

    def project(rows):
        obn = _rms(b_ref[rows, :].astype(F32), gb_ref[...]).astype(BF16)
        y = jnp.dot(a_ref[rows, :], wo_ref[:da], preferred_element_type=F32)
        return y + jnp.dot(obn, wo_ref[da:], preferred_element_type=F32)

    def route(rows, y, carry):
        n = rows.stop - rows.start
        x1 = x_tile[rows, :] + y
        x1_ref[rows, :] = x1
        hf = _rms(x1, gf_ref[...])
        half = hf.shape[1] // 2
        hp_ref[rows, :] = _pack_bf16_pair(hf[:, :half], hf[:, half:])

        logits = jnp.dot(hf.astype(BF16), rw_ref[...], preferred_element_type=F32) + rb_ref[...]
        lane = lax.broadcasted_iota(jnp.int32, logits.shape, 1)
        lane_f = lane.astype(F32)
        work = logits
        hits, ids, vals = [], [], []
        for _ in range(TOP_K):
            m = jnp.max(work, axis=-1, keepdims=True)
            sel = jnp.min(jnp.where(work == m, lane_f, float(LANES)), axis=-1, keepdims=True)
            hit = lane_f == sel
            hits.append(hit)
            ids.append(sel)
            vals.append(m)
            work = jnp.where(hit, -jnp.inf, work)
        exps = [jnp.exp(v - vals[0]) for v in vals]
        denom = exps[0]
        for e in exps[1:]:
            denom = denom + e

        onehot = hits[0]
        for hit in hits[1:]:
            onehot = onehot | hit
        cnt = onehot.astype(F32)
        rr = lax.broadcasted_iota(jnp.int32, (n, n), 0)
        cc = lax.broadcasted_iota(jnp.int32, (n, n), 1)
        tril = (rr > cc).astype(BF16)
        before = jnp.dot(tril, cnt.astype(BF16), preferred_element_type=F32) + carry

        ri = jnp.zeros(logits.shape, F32)
        rg = jnp.zeros(logits.shape, F32)
        for k in range(TOP_K):
            rank_k = jnp.sum(jnp.where(hits[k], before, 0.0), axis=-1, keepdims=True)
            ri = jnp.where(lane == k, ids[k], ri)
            ri = jnp.where(lane == TOP_K + k, rank_k, ri)
            rg = jnp.where(lane == k, exps[k] / denom, rg)
        rt_ref[:, rows] = ri.T[:2 * TOP_K, :].astype(jnp.int32)
        rg_ref[rows, :] = rg
        return carry + jnp.sum(cnt, axis=0, keepdims=True)

    n_piece = OUTPROJ_PIECES if tm % (OUTPROJ_PIECES * LANES) == 0 else 1
    pieces = [slice(i * tm // n_piece, (i + 1) * tm // n_piece) for i in range(n_piece)]
    carry = carry_ref[...]
    ys = [project(pieces[0])]
    for i in range(n_piece):
        if i + 1 < n_piece:
            ys.append(project(pieces[i + 1]))
        carry = route(pieces[i], ys[i], carry)
    carry_ref[...] = carry
    cnt_ref[...] = carry.astype(jnp.int32)


def _outproj(xa, xb, out_a, out_b, gb, wo, gf, rw, rb, tm):
    d = xa.shape[1]
    t = xa.shape[0] + xb.shape[0]
    n_first = xa.shape[0] // tm
    da = out_a.shape[1]
    db = out_b.shape[1]
    row = lambda i: (i, 0)
    fixed = lambda i: (0, 0)
    return pl.pallas_call(
        functools.partial(_outproj_kernel, n_first=n_first),
        grid=(t // tm,),
        in_specs=_pair_specs(tm, d, n_first) + [
            pl.BlockSpec((tm, da), row),
            pl.BlockSpec((tm, db), row),
            pl.BlockSpec((1, db), fixed),
            pl.BlockSpec(wo.shape, fixed, pipeline_mode=pl.Buffered(1)),
            pl.BlockSpec((1, d), fixed),
            pl.BlockSpec(rw.shape, fixed),
            pl.BlockSpec((1, LANES), fixed),
        ],
        out_specs=[
            pl.BlockSpec((tm, d), row),
            pl.BlockSpec((tm, d // 2), row),
            pl.BlockSpec((2 * TOP_K, tm), lambda i: (0, i)),
            pl.BlockSpec((tm, LANES), row),
            pl.BlockSpec((1, LANES), fixed),
        ],
        out_shape=[
            jax.ShapeDtypeStruct((t, d), F32),
            jax.ShapeDtypeStruct((t, d // 2), U32),
            jax.ShapeDtypeStruct((2 * TOP_K, t), jnp.int32),
            jax.ShapeDtypeStruct((t, LANES), F32),
            jax.ShapeDtypeStruct((1, LANES), jnp.int32),
        ],
        scratch_shapes=[pltpu.VMEM((1, LANES), F32)],
        compiler_params=_params(("arbitrary",)),
        name="outproj_router",
    )(xa, xb, out_a, out_b, gb, wo, gf, rw, rb)


def _tile_slots(dest_kt, tm):
    k, t = dest_kt.shape
    return dest_kt.reshape(k, t // tm, tm).transpose(1, 0, 2).reshape(t // tm, 1, k * tm)


def _dispatch_kernel(zneed_ref, dest_ref, hp_ref, xs_ref, zero_ref, sem, zsem, *, sub):
    step = pl.program_id(0)
    n_grp, group, _ = hp_ref.shape
    tm = n_grp * group
    n_zc = zneed_ref.shape[0]

    def zero_copy(j):
        return pltpu.make_async_copy(zero_ref, xs_ref.at[pl.ds(pl.multiple_of(j * sub, sub), sub), :], zsem)

    @pl.when(step == 0)
    def _():
        zero_ref[...] = jnp.zeros_like(zero_ref)

        def start(j, c):
            @pl.when(zneed_ref[j] != 0)
            def _():
                zero_copy(j).start()
            return c

        def wait(j, c):
            @pl.when(zneed_ref[j] != 0)
            def _():
                zero_copy(j).wait()
            return c

        lax.fori_loop(0, n_zc, start, 0)
        lax.fori_loop(0, n_zc, wait, 0)

    def issue(g, c):
        base = g * group
        for r in range(group):
            for k in range(TOP_K):
                d = dest_ref[0, 0, k * tm + base + r]
                pltpu.make_async_copy(hp_ref.at[g, pl.ds(r, 1), :], xs_ref.at[pl.ds(d, 1), :],
                                      sem).start(priority=k % 2)
        return c

    lax.fori_loop(0, n_grp, issue, 0)
    for _ in range(TOP_K):
        pltpu.make_async_copy(xs_ref.at[pl.ds(0, tm), :], xs_ref.at[pl.ds(0, tm), :], sem).wait()


def _dispatch(zneed, dest_kt, hp, n_slots, sub, tm):
    t, half = hp.shape
    grid_spec = pltpu.PrefetchScalarGridSpec(
        num_scalar_prefetch=1,
        grid=(t // tm,),
        in_specs=[
            pl.BlockSpec((1, 1, tm * TOP_K), lambda i, *_: (i, 0, 0), memory_space=pltpu.SMEM),
            pl.BlockSpec((tm // SUBLANES, SUBLANES, half), lambda i, *_: (i, 0, 0)),
        ],
        out_specs=pl.BlockSpec(memory_space=pl.ANY),
        scratch_shapes=[pltpu.VMEM((sub, half), U32), pltpu.SemaphoreType.DMA, pltpu.SemaphoreType.DMA],
    )
    return pl.pallas_call(
        functools.partial(_dispatch_kernel, sub=sub),
        grid_spec=grid_spec,
        out_shape=jax.ShapeDtypeStruct((n_slots, half), U32),
        compiler_params=_params(("arbitrary",)),
        name="dispatch",
    )(zneed, _tile_slots(dest_kt, tm), hp.reshape(t // SUBLANES, SUBLANES, half))


def _gmm_kernel(plan_ref, x_ref, bias_ref, w_hbm, o_ref, wbuf, stage, sem, *rest, mode, kc, sub, nw):
    blk = pl.program_id(0)
    expert = plan_ref[0, blk]
    half_idx = plan_ref[1, blk]
    nxt = plan_ref[3, blk]
    valid = blk < plan_ref[7, 0]
    n_sub = jnp.where(valid, plan_ref[6, blk], 0)
    n_ch = w_hbm.shape[1] // kc

    def chunk_rows(k):
        return pl.ds(pl.multiple_of(k * kc, kc), kc)

    def chunk_copy(e, k, st):
        return pltpu.make_async_copy(w_hbm.at[e, chunk_rows(k), :], stage.at[st], sem.at[st])

    def land(e, k, dst):
        st = k % 2
        chunk_copy(e, k, st).wait()
        wbuf[dst, chunk_rows(k), :] = stage[st].astype(BF16)

        @pl.when(k + 2 < n_ch)
        def _():
            chunk_copy(e, k + 2, st).start()

    @pl.when(blk == 0)
    def _():
        chunk_copy(expert, 0, 0).start()
        chunk_copy(expert, 1, 1).start()
        lax.fori_loop(0, n_ch, lambda k, c: (land(expert, k, half_idx), c)[1], 0)

    @pl.when(valid & (plan_ref[2, blk] == 1) & (nxt >= 0))
    def _():
        chunk_copy(nxt, 0, 0).start()
        chunk_copy(nxt, 1, 1).start()

    if mode == "gate_up":
        xb_ref, = rest
        half = x_ref.shape[1]
        de = o_ref.shape[1]

        def run(rows):
            v = x_ref[rows, :]
            xb_ref[rows, :half] = _unpack_lo(v).astype(BF16)
            xb_ref[rows, half:] = _unpack_hi(v).astype(BF16)
            x = xb_ref[rows, :]
            for n in range(de // nw):
                cg = slice(n * nw, (n + 1) * nw)
                cu = slice(de + n * nw, de + (n + 1) * nw)
                gate = jnp.dot(x, wbuf[half_idx, :, cg], preferred_element_type=F32) + bias_ref[:, cg]
                up = jnp.dot(x, wbuf[half_idx, :, cu], preferred_element_type=F32) + bias_ref[:, cu]
                gate = jnp.minimum(gate, SWIGLU_LIMIT)
                up = jnp.clip(up, -SWIGLU_LIMIT, SWIGLU_LIMIT)
                act = (up + 1.0) * (gate * jax.nn.sigmoid(SWIGLU_ALPHA * gate))
                o_ref[rows, cg] = act.astype(o_ref.dtype)
    else:
        half = o_ref.shape[1]

        def run(rows):
            x = x_ref[rows, :]
            for n in range(half // nw):
                cl = slice(n * nw, (n + 1) * nw)
                ch = slice(half + n * nw, half + (n + 1) * nw)
                lo = jnp.dot(x, wbuf[half_idx, :, cl], preferred_element_type=F32) + bias_ref[:, cl]
                hi = jnp.dot(x, wbuf[half_idx, :, ch], preferred_element_type=F32) + bias_ref[:, ch]
                o_ref[rows, cl] = _pack_bf16_pair(lo, hi)

    full = x_ref.shape[0] // sub

    @pl.when(n_sub < full)
    def _():
        o_ref[...] = jnp.zeros_like(o_ref)

    @pl.when(n_sub == full)
    def _():
        run(slice(None))

    @pl.when((n_sub > 0) & (n_sub < full))
    def _():
        def body(i, c):
            run(pl.ds(pl.multiple_of(i * sub, sub), sub))
            return c
        lax.fori_loop(0, n_sub, body, 0)

    lax.fori_loop(plan_ref[4, blk], plan_ref[5, blk], lambda k, c: (land(nxt, k, 1 - half_idx), c)[1], 0)


def _gmm(plan, x, bias, w, mode, bm, sub, kc):
    n_slots = x.shape[0]
    n_exp, kdim, ndim = w.shape
    n_blk = n_slots // bm
    if mode == "gate_up":
        out_cols, out_dtype = ndim // 2, BF16
        extra = [pltpu.VMEM((bm, kdim), BF16)]
    else:
        out_cols, out_dtype = ndim // 2, U32
        extra = []

    def live(b, plan):
        return (jnp.minimum(b, plan[7, 0] - 1), 0)

    grid_spec = pltpu.PrefetchScalarGridSpec(
        num_scalar_prefetch=1,
        grid=(n_blk,),
        in_specs=[
            pl.BlockSpec((bm, x.shape[1]), live),
            pl.BlockSpec((None, 1, ndim), lambda b, plan: (plan[0, b], 0, 0)),
            pl.BlockSpec(memory_space=pl.ANY),
        ],
        out_specs=pl.BlockSpec((bm, out_cols), lambda b, plan: (b, 0)),
        scratch_shapes=[
            pltpu.VMEM((2, kdim, ndim), BF16),
            pltpu.VMEM((2, kc, ndim), F32),
            pltpu.SemaphoreType.DMA((2,)),
        ] + extra,
    )
    return pl.pallas_call(
        functools.partial(_gmm_kernel, mode=mode, kc=kc, sub=sub, nw=min(EXPERT_DOT_COLUMNS, out_cols)),
        grid_spec=grid_spec,
        out_shape=jax.ShapeDtypeStruct((n_slots, out_cols), out_dtype),
        compiler_params=_params(("arbitrary",)),
        name="experts_" + mode,
    )(plan, x, bias, w)


def _expert_plan(counts, bm, sub, n_blk, n_ch):
    n_exp = counts.shape[0]
    ar = jnp.arange(n_exp, dtype=jnp.int32)
    nb_e = (counts + bm - 1) // bm
    blk_end = jnp.cumsum(nb_e)
    blk_start = blk_end - nb_e
    n_valid = blk_end[-1]
    blk = jnp.arange(n_blk, dtype=jnp.int32)
    last = jnp.minimum(blk, n_valid - 1)
    be = jnp.minimum(jnp.sum(blk_end[None, :] <= last[:, None], axis=1), n_exp - 1).astype(jnp.int32)
    mine = be[:, None] == ar[None, :]

    def of_block(per_expert):
        return jnp.sum(jnp.where(mine, per_expert[None, :], 0), axis=1)

    valid = blk < n_valid
    i_in = last - of_block(blk_start)
    nb_b = of_block(nb_e)
    nonempty = nb_e > 0
    ordinal = jnp.cumsum(nonempty.astype(jnp.int32)) - 1
    later = (ar[None, :] > ar[:, None]) & nonempty[None, :]
    nxt_e = jnp.min(jnp.where(later, ar[None, :], n_exp), axis=1)
    nxt_e = jnp.where(nxt_e >= n_exp, -1, nxt_e)
    nxt_b = of_block(nxt_e)
    has_next = valid & (nxt_b >= 0)
    lo = jnp.where(has_next, i_in * n_ch // nb_b, 0)
    hi = jnp.where(has_next, (i_in + 1) * n_ch // nb_b, 0)
    rows = jnp.where(valid, jnp.clip(of_block(counts) - i_in * bm, 0, bm), 0)
    nsub = (rows + sub - 1) // sub
    plan = jnp.stack([
        be, of_block(ordinal) % 2, (valid & (i_in == 0)).astype(jnp.int32), nxt_b, lo, hi, nsub,
        jnp.full((n_blk,), n_valid, jnp.int32),
    ]).astype(jnp.int32)
    per = bm // sub
    sub_rows = rows[:, None] - jnp.arange(per, dtype=jnp.int32)[None, :] * sub
    zneed = (sub_rows < sub).astype(jnp.int32).reshape(n_blk * per)
    return blk_start * bm, plan, zneed


def _combine_kernel(dcur_ref, dnxt_ref, x1_ref, rg_ref, gfin_ref, os_ref, ya_ref, yb_ref, gbuf_ref, sem,
                    *, final_norm, n_first, cw):
    step = pl.program_id(0)
    last = pl.num_programs(0) - 1
    tm = x1_ref.shape[0]
    half = gbuf_ref.shape[4]
    cur = step % 2

    def token_copies(dref, grp, row, buf):
        for k in range(TOP_K):
            d = dref[0, 0, k * tm + grp * SUBLANES + row]
            pltpu.make_async_copy(os_ref.at[pl.ds(d, 1), :], gbuf_ref.at[buf, k, grp, pl.ds(row, 1), :],
                                  sem.at[buf]).start(priority=k % 2)

    def wait_tile(buf):
        for k in range(TOP_K):
            pltpu.make_async_copy(os_ref.at[pl.ds(0, tm), :], os_ref.at[pl.ds(0, tm), :], sem.at[buf]).wait()

    @pl.when(step == 0)
    def _():
        def body(g, c):
            for r in range(SUBLANES):
                token_copies(dcur_ref, g, r, 0)
            return c
        lax.fori_loop(0, tm // SUBLANES, body, 0)

    wait_tile(cur)

    def finish(y_ref):
        rg = rg_ref[...]
        ss = jnp.zeros((tm, 1), F32)
        n_cc = half // cw
        share = tm // n_cc
        for c in range(n_cc):
            cl = slice(c * cw, (c + 1) * cw)
            ch = slice(half + c * cw, half + (c + 1) * cw)
            lo = x1_ref[:, cl]
            hi = x1_ref[:, ch]
            for k in range(TOP_K):
                g = rg[:, k:k + 1]
                v = gbuf_ref[cur, k, :, :, cl].reshape(tm, cw)
                lo = lo + g * _unpack_lo(v)
                hi = hi + g * _unpack_hi(v)
            if final_norm:
                ss = ss + jnp.sum(lo * lo, axis=-1, keepdims=True) + jnp.sum(hi * hi, axis=-1, keepdims=True)
            y_ref[:, cl] = lo
            y_ref[:, ch] = hi
            for r in range(c * share, (c + 1) * share):
                token_copies(dnxt_ref, r // SUBLANES, r % SUBLANES, 1 - cur)
        if final_norm:
            y_ref[...] = y_ref[...] * lax.rsqrt(ss / (2 * half) + EPS) * gfin_ref[...]

    @pl.when(step < n_first)
    def _():
        finish(ya_ref)

    @pl.when(step >= n_first)
    def _():
        finish(yb_ref)

    @pl.when(step == last)
    def _():
        wait_tile(1 - cur)


def _combine(dest_kt, x1, rg, gfin, os_, t_first, tm, final_norm):
    t, d = x1.shape
    half = d // 2
    n_tiles = t // tm
    n_first = t_first // tm
    dest3 = _tile_slots(dest_kt, tm)
    return pl.pallas_call(
        functools.partial(_combine_kernel, final_norm=final_norm, n_first=n_first,
                          cw=min(COMBINE_COLUMNS, half)),
        grid=(n_tiles,),
        in_specs=[
            pl.BlockSpec((1, 1, tm * TOP_K), lambda i: (i, 0, 0), memory_space=pltpu.SMEM),
            pl.BlockSpec((1, 1, tm * TOP_K), lambda i: (jnp.minimum(i + 1, n_tiles - 1), 0, 0),
                         memory_space=pltpu.SMEM),
            pl.BlockSpec((tm, d), lambda i: (i, 0)),
            pl.BlockSpec((tm, LANES), lambda i: (i, 0)),
            pl.BlockSpec((1, d), lambda i: (0, 0)),
            pl.BlockSpec(memory_space=pl.ANY),
        ],
        out_specs=_pair_specs(tm, d, n_first),
        out_shape=[jax.ShapeDtypeStruct((t_first, d), F32), jax.ShapeDtypeStruct((t - t_first, d), F32)],
        scratch_shapes=[pltpu.VMEM((2, TOP_K, tm // SUBLANES, SUBLANES, half), U32),
                        pltpu.SemaphoreType.DMA((2,))],
        compiler_params=_params(("arbitrary",)),
        name="combine",
    )(dest3, dest3, x1, rg, gfin, os_)


def _tile(n, pref):
    while n % pref:
        pref //= 2
    return pref


def _layer(xa, xb, seq_a, seq_b, p, final_g):
    ta, d = xa.shape
    t = ta + xb.shape[0]
    dg = p["gmlp_ln_g"].shape[-1]
    chunk = p["gmlp_ws"].shape[-1]
    d_lru = p["conv_w"].shape[-1]
    heads_l, hd_l = p["lru_wr"].shape[1], p["lru_wr"].shape[2]
    n_exp = p["router_w"].shape[-1]
    assert p["w_down"].shape[1] == d, "expert width must equal the model width (shared weight-chunk plan)"

    unit = max(seq_a, seq_b)
    assert unit % seq_a == 0 and unit % seq_b == 0 and ta % unit == 0 and (t - ta) % unit == 0
    tm = _tile(math.gcd(ta, t - ta), 512)
    assert tm % chunk == 0 and seq_a % chunk == 0 and seq_b % chunk == 0
    bs_b = jnp.broadcast_to(p["gmlp_bs"][:, :, None], p["gmlp_ws"].shape).astype(F32)
    z, out_a = _inproj(xa, xb, p["mix_norm_g"][None], p["w_in"].astype(BF16), p["gmlp_ln_g"][None],
                       p["gmlp_ln_b"][None], p["gmlp_ws"].astype(BF16), bs_b, p["out_norm_a"][None], tm)

    wri = (0.5 * jnp.concatenate([p["lru_wr"], p["lru_wi"]], axis=-1)).astype(BF16)
    bri = 0.5 * jnp.concatenate([p["lru_br"].reshape(2, heads_l, 1, hd_l),
                                 p["lru_bi"].reshape(2, heads_l, 1, hd_l)], axis=-1)
    lam = p["lru_lam"].reshape(2, heads_l, 1, hd_l)
    unit_seq = jnp.asarray([seq_a] * (ta // unit) + [seq_b] * ((t - ta) // unit), jnp.int32)
    out_b = _rglru(z, unit_seq, unit, 0, d_lru // hd_l,
                   p["conv_w"], p["conv_b"][None], wri, bri, lam, _tile(math.gcd(seq_a, seq_b), 512))

    rw = jnp.zeros((d, LANES), BF16).at[:, :n_exp].set(p["router_w"].astype(BF16))
    rb = jnp.full((1, LANES), NEG_BIG, F32).at[0, :n_exp].set(p["router_b"])
    x1, hp, rt, rg, cnt = _outproj(xa, xb, out_a, out_b, p["out_norm_b"][None], p["w_out"].astype(BF16),
                                   p["ffn_norm_g"][None], rw, rb, tm)

    m = t * TOP_K
    bm = _tile(m, ROW_BLOCK)
    sub = min(SUB_BLOCK, bm)
    kc = _tile(d, WEIGHT_CHUNK_ROWS)
    counts = cnt[0, :n_exp]
    n_blk = m // bm + n_exp
    n_slots = n_blk * bm
    pad_start, plan, zneed = _expert_plan(counts, bm, sub, n_blk, d // kc)
    top_e = rt[:TOP_K]
    base = jnp.sum(jnp.where(top_e[..., None] == jnp.arange(n_exp, dtype=jnp.int32), pad_start, 0), axis=-1)
    dest_kt = base + rt[TOP_K:]

    xs = _dispatch(zneed, dest_kt, hp, n_slots, sub, _tile(t, 1024))
    act = _gmm(plan, xs, p["b_gate_up"][:, None, :], p["w_gate_up"], "gate_up", bm, sub, kc)
    os_ = _gmm(plan, act, p["b_down"][:, None, :], p["w_down"], "down", bm, sub, kc)
    gfin = jnp.ones((1, d), F32) if final_g is None else final_g[None]
    return _combine(dest_kt, x1, rg, gfin, os_, ta, _tile(math.gcd(ta, t - ta), 256), final_g is not None)


_LAYER_KEYS = ("mix_norm_g", "w_in", "gmlp_ln_g", "gmlp_ln_b", "gmlp_ws", "gmlp_bs", "conv_w", "conv_b",
               "lru_wr", "lru_br", "lru_wi", "lru_bi", "lru_lam", "out_norm_a", "out_norm_b", "w_out",
               "ffn_norm_g", "router_w", "router_b", "w_gate_up", "b_gate_up", "w_down", "b_down")


def kernel(x_prompt, x_sample, mix_norm_g, w_in, gmlp_ln_g, gmlp_ln_b, gmlp_ws, gmlp_bs, conv_w, conv_b, lru_wr, lru_br, lru_wi, lru_bi, lru_lam, out_norm_a, out_norm_b, w_out, ffn_norm_g, router_w, router_b, w_gate_up, b_gate_up, w_down, b_down, final_norm_g):
    stacked = dict(zip(_LAYER_KEYS, (mix_norm_g, w_in, gmlp_ln_g, gmlp_ln_b, gmlp_ws, gmlp_bs, conv_w, conv_b,
                                     lru_wr, lru_br, lru_wi, lru_bi, lru_lam, out_norm_a, out_norm_b, w_out,
                                     ffn_norm_g, router_w, router_b, w_gate_up, b_gate_up, w_down, b_down)))
    depth = w_in.shape[0]
    bp, sp, d = x_prompt.shape
    bs, ss, _ = x_sample.shape
    xa = x_prompt.reshape(bp * sp, d)
    xb = x_sample.reshape(bs * ss, d)
    for layer in range(depth):
        p = {k: v[layer] for k, v in stacked.items()}
        xa, xb = _layer(xa, xb, sp, ss, p, final_norm_g if layer == depth - 1 else None)
    return xa.reshape(bp, sp, d), xb.reshape(bs, ss, d)
```

```python
import functools
import math

import jax
import jax.numpy as jnp
from jax import lax
from jax.experimental import pallas as pl
from jax.experimental.pallas import tpu as pltpu

TOP_K = 4
LRU_C = 8.0
SWIGLU_LIMIT = 7.0
SWIGLU_ALPHA = 1.702
EPS = 1e-6

LANES = 128
SUBLANES = 8
VMEM_LIMIT = 56 * 1024 * 1024
NEG_BIG = -1e30
ROW_BLOCK = 512
SUB_BLOCK = 128
WEIGHT_CHUNK_ROWS = 128
EXPERT_DOT_COLUMNS = 512
COMBINE_COLUMNS = 256
OUTPROJ_PIECES = 2

F32 = jnp.float32
BF16 = jnp.bfloat16
U32 = jnp.uint32


def _params(sem):
    return pltpu.CompilerParams(dimension_semantics=sem, vmem_limit_bytes=VMEM_LIMIT)


def _gelu(x):
    return 0.5 * x * (1.0 + jnp.tanh(0.7978845608028654 * (x + 0.044715 * (x * x * x))))


def _rms(x, g):
    return x * lax.rsqrt(jnp.mean(x * x, axis=-1, keepdims=True) + EPS) * g


def _pack_bf16_pair(lo, hi):
    lo_bits = lax.bitcast_convert_type(lo.astype(BF16).astype(F32), U32)
    hi_bits = lax.bitcast_convert_type(hi.astype(BF16).astype(F32), U32)
    return (lo_bits >> 16) | (hi_bits & jnp.uint32(0xFFFF0000))


def _unpack_lo(v):
    return lax.bitcast_convert_type(v << 16, F32)


def _unpack_hi(v):
    return lax.bitcast_convert_type(v & jnp.uint32(0xFFFF0000), F32)


def _pair_specs(tm, d, n_first):
    return [pl.BlockSpec((tm, d), lambda i: (jnp.minimum(i, n_first - 1), 0)),
            pl.BlockSpec((tm, d), lambda i: (jnp.maximum(i - n_first, 0), 0))]


def _pair_tile(xa_ref, xb_ref, n_first):
    return jnp.where(pl.program_id(0) < n_first, xa_ref[...], xb_ref[...])


def _inproj_kernel(xa_ref, xb_ref, g_ref, w_ref, lng_ref, lnb_ref, ws_ref, bs_ref, ga_ref, z_ref, oa_ref, mix_ref,
                   *, n_first, heads, chunk):
    xn = _rms(_pair_tile(xa_ref, xb_ref, n_first), g_ref[...]).astype(BF16)
    tm = xn.shape[0]
    dg = lng_ref.shape[1]
    hd = dg // heads

    def project(c0, width):
        return jnp.dot(xn, w_ref[:, c0:c0 + width], preferred_element_type=F32)

    v = _gelu(project(dg, dg))
    u = project(0, dg)
    mu = jnp.mean(v, axis=-1, keepdims=True)
    vc = v - mu
    var = jnp.mean(vc * vc, axis=-1, keepdims=True)
    vn = (vc * lax.rsqrt(var + EPS) * lng_ref[...] + lnb_ref[...]).astype(BF16)
    n_rest = z_ref.shape[1] // dg
    for n in range(n_rest):
        z_ref[:, n * dg:(n + 1) * dg] = project((2 + n) * dg, dg).astype(z_ref.dtype)
    for c in range(tm // chunk):
        rows = slice(c * chunk, (c + 1) * chunk)
        for h in range(heads):
            cols = slice(h * hd, (h + 1) * hd)
            mixed = jnp.dot(ws_ref[h], vn[rows, cols], preferred_element_type=F32) + bs_ref[h]
            mix_ref[rows, cols] = mixed
    out = _gelu(u) * mix_ref[...]
    oa_ref[...] = _rms(out, ga_ref[...]).astype(oa_ref.dtype)


def _inproj(xa, xb, g, w, ln_g, ln_b, ws, bs_b, ga, tm):
    d = xa.shape[1]
    t = xa.shape[0] + xb.shape[0]
    d_in = w.shape[1]
    heads, chunk, _ = ws.shape
    dg = ln_g.shape[1]
    assert (d_in - 2 * dg) % dg == 0
    n_first = xa.shape[0] // tm
    vec = pl.BlockSpec((1, dg), lambda i: (0, 0))
    return pl.pallas_call(
        functools.partial(_inproj_kernel, n_first=n_first, heads=heads, chunk=chunk),
        grid=(t // tm,),
        in_specs=_pair_specs(tm, d, n_first) + [
            pl.BlockSpec((1, d), lambda i: (0, 0)),
            pl.BlockSpec((d, d_in), lambda i: (0, 0), pipeline_mode=pl.Buffered(1)),
            vec, vec,
            pl.BlockSpec(ws.shape, lambda i: (0, 0, 0)),
            pl.BlockSpec(bs_b.shape, lambda i: (0, 0, 0)),
            vec,
        ],
        out_specs=[pl.BlockSpec((tm, d_in - 2 * dg), lambda i: (i, 0)), pl.BlockSpec((tm, dg), lambda i: (i, 0))],
        out_shape=[jax.ShapeDtypeStruct((t, d_in - 2 * dg), BF16), jax.ShapeDtypeStruct((t, dg), BF16)],
        scratch_shapes=[pltpu.VMEM((tm, dg), F32)],
        compiler_params=_params(("arbitrary",)),
        name="inproj_gmlp",
    )(xa, xb, g, w, ln_g, ln_b, ws, bs_b, ga)


def _rot8(x, s):
    n, w = x.shape
    return pltpu.roll(x.reshape(n // 8, 8, w), s, axis=1).reshape(n, w)


def _row_in_group(shape):
    return lax.broadcasted_iota(jnp.int32, shape, 0) & 7


def _shift_rows(x, s, fill, reverse, row):
    if reverse:
        return jnp.where(row >= 8 - s, fill, _rot8(x, 8 - s))
    return jnp.where(row < s, fill, _rot8(x, s))


def _lru_scan_chunk(a, b, h, reverse):
    row = _row_in_group(a.shape)
    for s in (1, 2, 4):
        b = a * _shift_rows(b, s, 0.0, reverse, row) + b
        a = a * _shift_rows(a, s, 1.0, reverse, row)
    n_grp = a.shape[0] // 8
    out = [None] * n_grp
    order = range(n_grp - 1, -1, -1) if reverse else range(n_grp)
    edge = 0 if reverse else 7
    for gi in order:
        rows = slice(gi * 8, gi * 8 + 8)
        hg = b[rows] + a[rows] * h
        out[gi] = hg
        h = hg[edge:edge + 1]
    return jnp.concatenate(out, axis=0), h


def _rglru_kernel(seqlen_ref, xr_ref, gr_ref, cw_ref, cb_ref, wri_ref, bri_ref, lam_ref, o_ref, hf_ref, xc_ref,
                  *, tc):
    rows, hd = xr_ref.shape
    n_ck = rows // tc
    halo = 16
    cw = cw_ref[...]
    cb = cb_ref[...]
    seq = seqlen_ref[pl.program_id(0)]

    def starts_seq(c):
        return lax.rem(c * tc, seq) == 0

    def ends_seq(c):
        return lax.rem((c + 1) * tc, seq) == 0

    def conv_chunk(c):
        t0 = pl.multiple_of(c * tc, tc)
        main = xr_ref[pl.ds(t0, tc), :].astype(F32)
        p0 = pl.multiple_of(jnp.maximum(t0 - halo, 0), halo)
        n0 = pl.multiple_of(jnp.minimum(t0 + tc, rows - halo), halo)
        prev = jnp.where(starts_seq(c), 0.0, xr_ref[pl.ds(p0, halo), :].astype(F32))
        nxt = jnp.where(ends_seq(c), 0.0, xr_ref[pl.ds(n0, halo), :].astype(F32))
        ext = jnp.concatenate([prev, main, nxt], axis=0)
        row = _row_in_group(main.shape)

        def earlier(s):
            r = _rot8(ext, s)
            return jnp.where(row < s, r[halo - 8:halo - 8 + tc], r[halo:halo + tc])

        r7 = _rot8(ext, 7)
        later = jnp.where(row >= 7, r7[halo + 8:halo + 8 + tc], r7[halo:halo + tc])
        acc = cb + cw[2:3] * main
        acc = acc + cw[0:1] * earlier(2)
        acc = acc + cw[1:2] * earlier(1)
        acc = acc + cw[3:4] * later
        return acc

    def direction(d, xc, h):
        g = jnp.tanh(jnp.dot(xc.astype(BF16), wri_ref[d], preferred_element_type=F32) + bri_ref[d])
        i = 0.5 * g[:, hd:] + 0.5
        nl = -lam_ref[d]
        softplus = jnp.maximum(nl, 0.0) + jnp.log1p(jnp.exp(-jnp.abs(nl)))
        half_c = (-0.5 * LRU_C) * softplus
        log_a = half_c * g[:, :hd] + half_c
        a = jnp.exp(log_a)
        b = jnp.sqrt(-jnp.tanh(log_a) * (a * a + 1.0)) * i * xc
        return _lru_scan_chunk(a, b, h, reverse=(d == 1))

    def fwd(c, h):
        t0 = pl.multiple_of(c * tc, tc)
        xc = conv_chunk(c)
        xc_ref[pl.ds(t0, tc), :] = xc
        hs, h = direction(0, xc, jnp.where(starts_seq(c), 0.0, h))
        hf_ref[pl.ds(t0, tc), :] = hs
        return h

    def bwd(k, h):
        c = n_ck - 1 - k
        t0 = pl.multiple_of(c * tc, tc)
        hs, h = direction(1, xc_ref[pl.ds(t0, tc), :], jnp.where(ends_seq(c), 0.0, h))
        tot = hf_ref[pl.ds(t0, tc), :] + hs
        o_ref[pl.ds(t0, tc), :] = (_gelu(gr_ref[pl.ds(t0, tc), :].astype(F32)) * tot).astype(o_ref.dtype)
        return h

    h0 = jnp.zeros((1, hd), F32)
    lax.fori_loop(0, n_ck, fwd, h0)
    lax.fori_loop(0, n_ck, bwd, h0)


def _rglru(z, unit_seq, unit, col_x, col_g, cw, cb, wri, bri, lam, tc):
    t = z.shape[0]
    _, heads, hd, _ = wri.shape
    grid_spec = pltpu.PrefetchScalarGridSpec(
        num_scalar_prefetch=1,
        grid=(t // unit, heads),
        in_specs=[
            pl.BlockSpec((unit, hd), lambda u, h, sl: (u, col_x + h)),
            pl.BlockSpec((unit, hd), lambda u, h, sl: (u, col_g + h)),
            pl.BlockSpec((cw.shape[0], hd), lambda u, h, sl: (0, h)),
            pl.BlockSpec((1, hd), lambda u, h, sl: (0, h)),
            pl.BlockSpec((2, None, hd, 2 * hd), lambda u, h, sl: (0, h, 0, 0)),
            pl.BlockSpec((2, None, 1, 2 * hd), lambda u, h, sl: (0, h, 0, 0)),
            pl.BlockSpec((2, None, 1, hd), lambda u, h, sl: (0, h, 0, 0)),
        ],
        out_specs=pl.BlockSpec((unit, hd), lambda u, h, sl: (u, h)),
        scratch_shapes=[pltpu.VMEM((unit, hd), F32), pltpu.VMEM((unit, hd), F32)],
    )
    return pl.pallas_call(
        functools.partial(_rglru_kernel, tc=tc),
        grid_spec=grid_spec,
        out_shape=jax.ShapeDtypeStruct((t, heads * hd), BF16),
        compiler_params=_params(("arbitrary", "arbitrary")),
        name="rglru",
    )(unit_seq, z, z, cw, cb, wri, bri, lam)


def _outproj_kernel(xa_ref, xb_ref, a_ref, b_ref, gb_ref, wo_ref, gf_ref, rw_ref, rb_ref,
                    x1_ref, hp_ref, rt_ref, rg_ref, cnt_ref, carry_ref, *, n_first):
    step = pl.program_id(0)

    @pl.when(step == 0)
    def _():
        carry_ref[...] = jnp.zeros_like(carry_ref)

    da = a_ref.shape[1]
    tm = a_ref.shape[0]
    x_tile = _pair_tile(xa_ref, xb_ref, n_first)

    def project(rows):
        obn = _rms(b_ref[rows, :].astype(F32), gb_ref[...]).astype(BF16)
        y = jnp.dot(a_ref[rows, :], wo_ref[:da], preferred_element_type=F32)
        return y + jnp.dot(obn, wo_ref[da:], preferred_element_type=F32)

    def route(rows, y, carry):
        n = rows.stop - rows.start
        x1 = x_tile[rows, :] + y
        x1_ref[rows, :] = x1
        hf = _rms(x1, gf_ref[...])
        half = hf.shape[1] // 2
        hp_ref[rows, :] = _pack_bf16_pair(hf[:, :half], hf[:, half:])

        logits = jnp.dot(hf.astype(BF16), rw_ref[...], preferred_element_type=F32) + rb_ref[...]
        lane = lax.broadcasted_iota(jnp.int32, logits.shape, 1)
        lane_f = lane.astype(F32)
        work = logits
        hits, ids, vals = [], [], []
        for _ in range(TOP_K):
            m = jnp.max(work, axis=-1, keepdims=True)
            sel = jnp.min(jnp.where(work == m, lane_f, float(LANES)), axis=-1, keepdims=True)
            hit = lane_f == sel
            hits.append(hit)
            ids.append(sel)
            vals.append(m)
            work = jnp.where(hit, -jnp.inf, work)
        exps = [jnp.exp(v - vals[0]) for v in vals]
        denom = exps[0]
        for e in exps[1:]:
            denom = denom + e

        onehot = hits[0]
        for hit in hits[1:]:
            onehot = onehot | hit
        cnt = onehot.astype(F32)
        rr = lax.broadcasted_iota(jnp.int32, (n, n), 0)
        cc = lax.broadcasted_iota(jnp.int32, (n, n), 1)
        tril = (rr > cc).astype(BF16)
        before = jnp.dot(tril, cnt.astype(BF16), preferred_element_type=F32) + carry

        ri = jnp.zeros(logits.shape, F32)
        rg = jnp.zeros(logits.shape, F32)
        for k in range(TOP_K):
            rank_k = jnp.sum(jnp.where(hits[k], before, 0.0), axis=-1, keepdims=True)
            ri = jnp.where(lane == k, ids[k], ri)
            ri = jnp.where(lane == TOP_K + k, rank_k, ri)
            rg = jnp.where(lane == k, exps[k] / denom, rg)
        rt_ref[:, rows] = ri.T[:2 * TOP_K, :].astype(jnp.int32)
        rg_ref[rows, :] = rg
        return carry + jnp.sum(cnt, axis=0, keepdims=True)

    n_piece = OUTPROJ_PIECES if tm % (OUTPROJ_PIECES * LANES) == 0 else 1
    pieces = [slice(i * tm // n_piece, (i + 1) * tm // n_piece) for i in range(n_piece)]
    carry = carry_ref[...]
    ys = [project(pieces[0])]
    for i in range(n_piece):
        if i + 1 < n_piece:
            ys.append(project(pieces[i + 1]))
        carry = route(pieces[i], ys[i], carry)
    carry_ref[...] = carry
    cnt_ref[...] = carry.astype(jnp.int32)


def _outproj(xa, xb, out_a, out_b, gb, wo, gf, rw, rb, tm):
    d = xa.shape[1]
    t = xa.shape[0] + xb.shape[0]
    n_first = xa.shape[0] // tm
    da = out_a.shape[1]
    db = out_b.shape[1]
    row = lambda i: (i, 0)
    fixed = lambda i: (0, 0)
    return pl.pallas_call(
        functools.partial(_outproj_kernel, n_first=n_first),
        grid=(t // tm,),
        in_specs=_pair_specs(tm, d, n_first) + [
            pl.BlockSpec((tm, da), row),
            pl.BlockSpec((tm, db), row),
            pl.BlockSpec((1, db), fixed),
            pl.BlockSpec(wo.shape, fixed, pipeline_mode=pl.Buffered(1)),
            pl.BlockSpec((1, d), fixed),
            pl.BlockSpec(rw.shape, fixed),
            pl.BlockSpec((1, LANES), fixed),
        ],
        out_specs=[
            pl.BlockSpec((tm, d), row),
            pl.BlockSpec((tm, d // 2), row),
            pl.BlockSpec((2 * TOP_K, tm), lambda i: (0, i)),
            pl.BlockSpec((tm, LANES), row),
            pl.BlockSpec((1, LANES), fixed),
        ],
        out_shape=[
            jax.ShapeDtypeStruct((t, d), F32),
            jax.ShapeDtypeStruct((t, d // 2), U32),
            jax.ShapeDtypeStruct((2 * TOP_K, t), jnp.int32),
            jax.ShapeDtypeStruct((t, LANES), F32),
            jax.ShapeDtypeStruct((1, LANES), jnp.int32),
        ],
        scratch_shapes=[pltpu.VMEM((1, LANES), F32)],
        compiler_params=_params(("arbitrary",)),
        name="outproj_router",
    )(xa, xb, out_a, out_b, gb, wo, gf, rw, rb)


def _tile_slots(dest_kt, tm):
    k, t = dest_kt.shape
    return dest_kt.reshape(k, t // tm, tm).transpose(1, 0, 2).reshape(t // tm, 1, k * tm)


def _dispatch_kernel(zneed_ref, dest_ref, hp_ref, xs_ref, zero_ref, sem, zsem, *, sub):
    step = pl.program_id(0)
    n_grp, group, _ = hp_ref.shape
    tm = n_grp * group
    n_zc = zneed_ref.shape[0]

    def zero_copy(j):
        return pltpu.make_async_copy(zero_ref, xs_ref.at[pl.ds(pl.multiple_of(j * sub, sub), sub), :], zsem)

    @pl.when(step == 0)
    def _():
        zero_ref[...] = jnp.zeros_like(zero_ref)

        def start(j, c):
            @pl.when(zneed_ref[j] != 0)
            def _():
                zero_copy(j).start()
            return c

        def wait(j, c):
            @pl.when(zneed_ref[j] != 0)
            def _():
                zero_copy(j).wait()
            return c

        lax.fori_loop(0, n_zc, start, 0)
        lax.fori_loop(0, n_zc, wait, 0)

    def issue(g, c):
        base = g * group
        for r in range(group):
            for k in range(TOP_K):
                d = dest_ref[0, 0, k * tm + base + r]
                pltpu.make_async_copy(hp_ref.at[g, pl.ds(r, 1), :], xs_ref.at[pl.ds(d, 1), :],
                                      sem).start(priority=k % 2)
        return c

    lax.fori_loop(0, n_grp, issue, 0)
    for _ in range(TOP_K):
        pltpu.make_async_copy(xs_ref.at[pl.ds(0, tm), :], xs_ref.at[pl.ds(0, tm), :], sem).wait()


def _dispatch(zneed, dest_kt, hp, n_slots, sub, tm):
    t, half = hp.shape
    grid_spec = pltpu.PrefetchScalarGridSpec(
        num_scalar_prefetch=1,
        grid=(t // tm,),
        in_specs=[
            pl.BlockSpec((1, 1, tm * TOP_K), lambda i, *_: (i, 0, 0), memory_space=pltpu.SMEM),
            pl.BlockSpec((tm // SUBLANES, SUBLANES, half), lambda i, *_: (i, 0, 0)),
        ],
        out_specs=pl.BlockSpec(memory_space=pl.ANY),
        scratch_shapes=[pltpu.VMEM((sub, half), U32), pltpu.SemaphoreType.DMA, pltpu.SemaphoreType.DMA],
    )
    return pl.pallas_call(
        functools.partial(_dispatch_kernel, sub=sub),
        grid_spec=grid_spec,
        out_shape=jax.ShapeDtypeStruct((n_slots, half), U32),
        compiler_params=_params(("arbitrary",)),
        name="dispatch",
    )(zneed, _tile_slots(dest_kt, tm), hp.reshape(t // SUBLANES, SUBLANES, half))


def _gmm_kernel(plan_ref, x_ref, bias_ref, w_hbm, o_ref, wbuf, stage, sem, *rest, mode, kc, sub, nw):
    blk = pl.program_id(0)
    expert = plan_ref[0, blk]
    half_idx = plan_ref[1, blk]
    nxt = plan_ref[3, blk]
    valid = blk < plan_ref[7, 0]
    n_sub = jnp.where(valid, plan_ref[6, blk], 0)
    n_ch = w_hbm.shape[1] // kc

    def chunk_rows(k):
        return pl.ds(pl.multiple_of(k * kc, kc), kc)

    def chunk_copy(e, k, st):
        return pltpu.make_async_copy(w_hbm.at[e, chunk_rows(k), :], stage.at[st], sem.at[st])

    def land(e, k, dst):
        st = k % 2
        chunk_copy(e, k, st).wait()
        wbuf[dst, chunk_rows(k), :] = stage[st].astype(BF16)

        @pl.when(k + 2 < n_ch)
        def _():
            chunk_copy(e, k + 2, st).start()

    @pl.when(blk == 0)
    def _():
        chunk_copy(expert, 0, 0).start()
        chunk_copy(expert, 1, 1).start()
        lax.fori_loop(0, n_ch, lambda k, c: (land(expert, k, half_idx), c)[1], 0)

    @pl.when(valid & (plan_ref[2, blk] == 1) & (nxt >= 0))
    def _():
        chunk_copy(nxt, 0, 0).start()
        chunk_copy(nxt, 1, 1).start()

    if mode == "gate_up":
        xb_ref, = rest
        half = x_ref.shape[1]
        de = o_ref.shape[1]

        def run(rows):
            v = x_ref[rows, :]
            xb_ref[rows, :half] = _unpack_lo(v).astype(BF16)
            xb_ref[rows, half:] = _unpack_hi(v).astype(BF16)
            x = xb_ref[rows, :]
            for n in range(de // nw):
                cg = slice(n * nw, (n + 1) * nw)
                cu = slice(de + n * nw, de + (n + 1) * nw)
                gate = jnp.dot(x, wbuf[half_idx, :, cg], preferred_element_type=F32) + bias_ref[:, cg]
                up = jnp.dot(x, wbuf[half_idx, :, cu], preferred_element_type=F32) + bias_ref[:, cu]
                gate = jnp.minimum(gate, SWIGLU_LIMIT)
                up = jnp.clip(up, -SWIGLU_LIMIT, SWIGLU_LIMIT)
                act = (up + 1.0) * (gate * jax.nn.sigmoid(SWIGLU_ALPHA * gate))
                o_ref[rows, cg] = act.astype(o_ref.dtype)
    else:
        half = o_ref.shape[1]

        def run(rows):
            x = x_ref[rows, :]
            for n in range(half // nw):
                cl = slice(n * nw, (n + 1) * nw)
                ch = slice(half + n * nw, half + (n + 1) * nw)
                lo = jnp.dot(x, wbuf[half_idx, :, cl], preferred_element_type=F32) + bias_ref[:, cl]
                hi = jnp.dot(x, wbuf[half_idx, :, ch], preferred_element_type=F32) + bias_ref[:, ch]
                o_ref[rows, cl] = _pack_bf16_pair(lo, hi)

    full = x_ref.shape[0] // sub

    @pl.when(n_sub < full)
    def _():
        o_ref[...] = jnp.zeros_like(o_ref)

    @pl.when(n_sub == full)
    def _():
        run(slice(None))

    @pl.when((n_sub > 0) & (n_sub < full))
    def _():
        def body(i, c):
            run(pl.ds(pl.multiple_of(i * sub, sub), sub))
            return c
        lax.fori_loop(0, n_sub, body, 0)

    lax.fori_loop(plan_ref[4, blk], plan_ref[5, blk], lambda k, c: (land(nxt, k, 1 - half_idx), c)[1], 0)


def _gmm(plan, x, bias, w, mode, bm, sub, kc):
    n_slots = x.shape[0]
    n_exp, kdim, ndim = w.shape
    n_blk = n_slots // bm
    if mode == "gate_up":
        out_cols, out_dtype = ndim // 2, BF16
        extra = [pltpu.VMEM((bm, kdim), BF16)]
    else:
        out_cols, out_dtype = ndim // 2, U32
        extra = []

    def live(b, plan):
        return (jnp.minimum(b, plan[7, 0] - 1), 0)

    grid_spec = pltpu.PrefetchScalarGridSpec(
        num_scalar_prefetch=1,
        grid=(n_blk,),
        in_specs=[
            pl.BlockSpec((bm, x.shape[1]), live),
            pl.BlockSpec((None, 1, ndim), lambda b, plan: (plan[0, b], 0, 0)),
            pl.BlockSpec(memory_space=pl.ANY),
        ],
        out_specs=pl.BlockSpec((bm, out_cols), lambda b, plan: (b, 0)),
        scratch_shapes=[
            pltpu.VMEM((2, kdim, ndim), BF16),
            pltpu.VMEM((2, kc, ndim), F32),
            pltpu.SemaphoreType.DMA((2,)),
        ] + extra,
    )
    return pl.pallas_call(
        functools.partial(_gmm_kernel, mode=mode, kc=kc, sub=sub, nw=min(EXPERT_DOT_COLUMNS, out_cols)),
        grid_spec=grid_spec,
        out_shape=jax.ShapeDtypeStruct((n_slots, out_cols), out_dtype),
        compiler_params=_params(("arbitrary",)),
        name="experts_" + mode,
    )(plan, x, bias, w)


def _expert_plan(counts, bm, sub, n_blk, n_ch):
    n_exp = counts.shape[0]
    ar = jnp.arange(n_exp, dtype=jnp.int32)
    nb_e = (counts + bm - 1) // bm
    blk_end = jnp.cumsum(nb_e)
    blk_start = blk_end - nb_e
    n_valid = blk_end[-1]
    blk = jnp.arange(n_blk, dtype=jnp.int32)
    last = jnp.minimum(blk, n_valid - 1)
    be = jnp.minimum(jnp.sum(blk_end[None, :] <= last[:, None], axis=1), n_exp - 1).astype(jnp.int32)
    mine = be[:, None] == ar[None, :]

    def of_block(per_expert):
        return jnp.sum(jnp.where(mine, per_expert[None, :], 0), axis=1)

    valid = blk < n_valid
    i_in = last - of_block(blk_start)
    nb_b = of_block(nb_e)
    nonempty = nb_e > 0
    ordinal = jnp.cumsum(nonempty.astype(jnp.int32)) - 1
    later = (ar[None, :] > ar[:, None]) & nonempty[None, :]
    nxt_e = jnp.min(jnp.where(later, ar[None, :], n_exp), axis=1)
    nxt_e = jnp.where(nxt_e >= n_exp, -1, nxt_e)
    nxt_b = of_block(nxt_e)
    has_next = valid & (nxt_b >= 0)
    lo = jnp.where(has_next, i_in * n_ch // nb_b, 0)
    hi = jnp.where(has_next, (i_in + 1) * n_ch // nb_b, 0)
    rows = jnp.where(valid, jnp.clip(of_block(counts) - i_in * bm, 0, bm), 0)
    nsub = (rows + sub - 1) // sub
    plan = jnp.stack([
        be, of_block(ordinal) % 2, (valid & (i_in == 0)).astype(jnp.int32), nxt_b, lo, hi, nsub,
        jnp.full((n_blk,), n_valid, jnp.int32),
    ]).astype(jnp.int32)
    per = bm // sub
    sub_rows = rows[:, None] - jnp.arange(per, dtype=jnp.int32)[None, :] * sub
    zneed = (sub_rows < sub).astype(jnp.int32).reshape(n_blk * per)
    return blk_start * bm, plan, zneed


def _combine_kernel(dcur_ref, dnxt_ref, x1_ref, rg_ref, gfin_ref, os_ref, ya_ref, yb_ref, gbuf_ref, sem,
                    *, final_norm, n_first, cw):
    step = pl.program_id(0)
    last = pl.num_programs(0) - 1
    tm = x1_ref.shape[0]
    half = gbuf_ref.shape[4]
    cur = step % 2

    def token_copies(dref, grp, row, buf):
        for k in range(TOP_K):
            d = dref[0, 0, k * tm + grp * SUBLANES + row]
            pltpu.make_async_copy(os_ref.at[pl.ds(d, 1), :], gbuf_ref.at[buf, k, grp, pl.ds(row, 1), :],
                                  sem.at[buf]).start(priority=k % 2)

    def wait_tile(buf):
        for k in range(TOP_K):
            pltpu.make_async_copy(os_ref.at[pl.ds(0, tm), :], os_ref.at[pl.ds(0, tm), :], sem.at[buf]).wait()

    @pl.when(step == 0)
    def _():
        def body(g, c):
            for r in range(SUBLANES):
                token_copies(dcur_ref, g, r, 0)
            return c
        lax.fori_loop(0, tm // SUBLANES, body, 0)

    def finish(y_ref, cur):
        wait_tile(cur)
        rg = rg_ref[...]
        ss = jnp.zeros((tm, 1), F32)
        n_cc = half // cw
        share = tm // n_cc
        for c in range(n_cc):
            cl = slice(c * cw, (c + 1) * cw)
            ch = slice(half + c * cw, half + (c + 1) * cw)
            lo = x1_ref[:, cl]
            hi = x1_ref[:, ch]
            for k in range(TOP_K):
                g = rg[:, k:k + 1]
                v = gbuf_ref[cur, k, :, :, cl].reshape(tm, cw)
                lo = lo + g * _unpack_lo(v)
                hi = hi + g * _unpack_hi(v)
            if final_norm:
                ss = ss + jnp.sum(lo * lo, axis=-1, keepdims=True) + jnp.sum(hi * hi, axis=-1, keepdims=True)
            y_ref[:, cl] = lo
            y_ref[:, ch] = hi
            for r in range(c * share, (c + 1) * share):
                token_copies(dnxt_ref, r // SUBLANES, r % SUBLANES, 1 - cur)
        if final_norm:
            y_ref[...] = y_ref[...] * lax.rsqrt(ss / (2 * half) + EPS) * gfin_ref[...]

    for parity in range(2):
        @pl.when((step < n_first) & (cur == parity))
        def _():
            finish(ya_ref, parity)

        @pl.when((step >= n_first) & (cur == parity))
        def _():
            finish(yb_ref, parity)

    @pl.when(step == last)
    def _():
        wait_tile(1 - cur)


def _combine(dest_kt, x1, rg, gfin, os_, t_first, tm, final_norm):
    t, d = x1.shape
    half = d // 2
    n_tiles = t // tm
    n_first = t_first // tm
    dest3 = _tile_slots(dest_kt, tm)
    return pl.pallas_call(
        functools.partial(_combine_kernel, final_norm=final_norm, n_first=n_first,
                          cw=min(COMBINE_COLUMNS, half)),
        grid=(n_tiles,),
        in_specs=[
            pl.BlockSpec((1, 1, tm * TOP_K), lambda i: (i, 0, 0), memory_space=pltpu.SMEM),
            pl.BlockSpec((1, 1, tm * TOP_K), lambda i: (jnp.minimum(i + 1, n_tiles - 1), 0, 0),
                         memory_space=pltpu.SMEM),
            pl.BlockSpec((tm, d), lambda i: (i, 0)),
            pl.BlockSpec((tm, LANES), lambda i: (i, 0)),
            pl.BlockSpec((1, d), lambda i: (0, 0)),
            pl.BlockSpec(memory_space=pl.ANY),
        ],
        out_specs=_pair_specs(tm, d, n_first),
        out_shape=[jax.ShapeDtypeStruct((t_first, d), F32), jax.ShapeDtypeStruct((t - t_first, d), F32)],
        scratch_shapes=[pltpu.VMEM((2, TOP_K, tm // SUBLANES, SUBLANES, half), U32),
                        pltpu.SemaphoreType.DMA((2,))],
        compiler_params=_params(("arbitrary",)),
        name="combine",
    )(dest3, dest3, x1, rg, gfin, os_)


def _tile(n, pref):
    while n % pref:
        pref //= 2
    return pref


def _layer(xa, xb, seq_a, seq_b, p, final_g):
    ta, d = xa.shape
    t = ta + xb.shape[0]
    dg = p["gmlp_ln_g"].shape[-1]
    chunk = p["gmlp_ws"].shape[-1]
    d_lru = p["conv_w"].shape[-1]
    heads_l, hd_l = p["lru_wr"].shape[1], p["lru_wr"].shape[2]
    n_exp = p["router_w"].shape[-1]
    assert p["w_down"].shape[1] == d, "expert width must equal the model width (shared weight-chunk plan)"

    unit = max(seq_a, seq_b)
    assert unit % seq_a == 0 and unit % seq_b == 0 and ta % unit == 0 and (t - ta) % unit == 0
    tm = _tile(math.gcd(ta, t - ta), 512)
    assert tm % chunk == 0 and seq_a % chunk == 0 and seq_b % chunk == 0
    bs_b = jnp.broadcast_to(p["gmlp_bs"][:, :, None], p["gmlp_ws"].shape).astype(F32)
    z, out_a = _inproj(xa, xb, p["mix_norm_g"][None], p["w_in"].astype(BF16), p["gmlp_ln_g"][None],
                       p["gmlp_ln_b"][None], p["gmlp_ws"].astype(BF16), bs_b, p["out_norm_a"][None], tm)

    wri = (0.5 * jnp.concatenate([p["lru_wr"], p["lru_wi"]], axis=-1)).astype(BF16)
    bri = 0.5 * jnp.concatenate([p["lru_br"].reshape(2, heads_l, 1, hd_l),
                                 p["lru_bi"].reshape(2, heads_l, 1, hd_l)], axis=-1)
    lam = p["lru_lam"].reshape(2, heads_l, 1, hd_l)
    unit_seq = jnp.asarray([seq_a] * (ta // unit) + [seq_b] * ((t - ta) // unit), jnp.int32)
    out_b = _rglru(z, unit_seq, unit, 0, d_lru // hd_l,
                   p["conv_w"], p["conv_b"][None], wri, bri, lam, _tile(math.gcd(seq_a, seq_b), 512))

    rw = jnp.zeros((d, LANES), BF16).at[:, :n_exp].set(p["router_w"].astype(BF16))
    rb = jnp.full((1, LANES), NEG_BIG, F32).at[0, :n_exp].set(p["router_b"])
    x1, hp, rt, rg, cnt = _outproj(xa, xb, out_a, out_b, p["out_norm_b"][None], p["w_out"].astype(BF16),
                                   p["ffn_norm_g"][None], rw, rb, tm)

    m = t * TOP_K
    bm = _tile(m, ROW_BLOCK)
    sub = min(SUB_BLOCK, bm)
    kc = _tile(d, WEIGHT_CHUNK_ROWS)
    counts = cnt[0, :n_exp]
    n_blk = m // bm + n_exp
    n_slots = n_blk * bm
    pad_start, plan, zneed = _expert_plan(counts, bm, sub, n_blk, d // kc)
    top_e = rt[:TOP_K]
    base = jnp.sum(jnp.where(top_e[..., None] == jnp.arange(n_exp, dtype=jnp.int32), pad_start, 0), axis=-1)
    dest_kt = base + rt[TOP_K:]

    xs = _dispatch(zneed, dest_kt, hp, n_slots, sub, _tile(t, 1024))
    act = _gmm(plan, xs, p["b_gate_up"][:, None, :], p["w_gate_up"], "gate_up", bm, sub, kc)
    os_ = _gmm(plan, act, p["b_down"][:, None, :], p["w_down"], "down", bm, sub, kc)
    gfin = jnp.ones((1, d), F32) if final_g is None else final_g[None]
    return _combine(dest_kt, x1, rg, gfin, os_, ta, _tile(math.gcd(ta, t - ta), 256), final_g is not None)


_LAYER_KEYS = ("mix_norm_g", "w_in", "gmlp_ln_g", "gmlp_ln_b", "gmlp_ws", "gmlp_bs", "conv_w", "conv_b",
               "lru_wr", "lru_br", "lru_wi", "lru_bi", "lru_lam", "out_norm_a", "out_norm_b", "w_out",
               "ffn_norm_g", "router_w", "router_b", "w_gate_up", "b_gate_up", "w_down", "b_down")


def kernel(x_prompt, x_sample, mix_norm_g, w_in, gmlp_ln_g, gmlp_ln_b, gmlp_ws, gmlp_bs, conv_w, conv_b, lru_wr, lru_br, lru_wi, lru_bi, lru_lam, out_norm_a, out_norm_b, w_out, ffn_norm_g, router_w, router_b, w_gate_up, b_gate_up, w_down, b_down, final_norm_g):
    stacked = dict(zip(_LAYER_KEYS, (mix_norm_g, w_in, gmlp_ln_g, gmlp_ln_b, gmlp_ws, gmlp_bs, conv_w, conv_b,
                                     lru_wr, lru_br, lru_wi, lru_bi, lru_lam, out_norm_a, out_norm_b, w_out,
                                     ffn_norm_g, router_w, router_b, w_gate_up, b_gate_up, w_down, b_down)))
    depth = w_in.shape[0]
    bp, sp, d = x_prompt.shape
    bs, ss, _ = x_sample.shape
    xa = x_prompt.reshape(bp * sp, d)
    xb = x_sample.reshape(bs * ss, d)
    for layer in range(depth):
        p = {k: v[layer] for k, v in stacked.items()}
        xa, xb = _layer(xa, xb, sp, ss, p, final_norm_g if layer == depth - 1 else None)
    return xa.reshape(bp, sp, d), xb.reshape(bs, ss, d)
```

```python
import functools
import math

import jax
import jax.numpy as jnp
from jax import lax
from jax.experimental import pallas as pl
from jax.experimental.pallas import tpu as pltpu

TOP_K = 4
LRU_C = 8.0
SWIGLU_LIMIT = 7.0
SWIGLU_ALPHA = 1.702
EPS = 1e-6

LANES = 128
SUBLANES = 8
VMEM_LIMIT = 56 * 1024 * 1024
NEG_BIG = -1e30
ROW_BLOCK = 512
SUB_BLOCK = 128
WEIGHT_CHUNK_ROWS = 128
EXPERT_DOT_COLUMNS = 512
COMBINE_COLUMNS = 256
OUTPROJ_PIECES = 2

F32 = jnp.float32
BF16 = jnp.bfloat16
U32 = jnp.uint32


def _params(sem):
    return pltpu.CompilerParams(dimension_semantics=sem, vmem_limit_bytes=VMEM_LIMIT)


def _gelu(x):
    return 0.5 * x * (1.0 + jnp.tanh(0.7978845608028654 * (x + 0.044715 * (x * x * x))))


def _rms(x, g):
    return x * lax.rsqrt(jnp.mean(x * x, axis=-1, keepdims=True) + EPS) * g


def _pack_bf16_pair(lo, hi):
    lo_bits = lax.bitcast_convert_type(lo.astype(BF16).astype(F32), U32)
    hi_bits = lax.bitcast_convert_type(hi.astype(BF16).astype(F32), U32)
    return (lo_bits >> 16) | (hi_bits & jnp.uint32(0xFFFF0000))


def _unpack_lo(v):
    return lax.bitcast_convert_type(v << 16, F32)


def _unpack_hi(v):
    return lax.bitcast_convert_type(v & jnp.uint32(0xFFFF0000), F32)


def _pair_specs(tm, d, n_first):
    return [pl.BlockSpec((tm, d), lambda i: (jnp.minimum(i, n_first - 1), 0)),
            pl.BlockSpec((tm, d), lambda i: (jnp.maximum(i - n_first, 0), 0))]


def _pair_tile(xa_ref, xb_ref, n_first):
    return jnp.where(pl.program_id(0) < n_first, xa_ref[...], xb_ref[...])


def _inproj_kernel(xa_ref, xb_ref, g_ref, w_ref, lng_ref, lnb_ref, ws_ref, bs_ref, ga_ref, z_ref, oa_ref, mix_ref,
                   *, n_first, heads, chunk):
    xn = _rms(_pair_tile(xa_ref, xb_ref, n_first), g_ref[...]).astype(BF16)
    tm = xn.shape[0]
    dg = lng_ref.shape[1]
    hd = dg // heads

    def project(c0, width):
        return jnp.dot(xn, w_ref[:, c0:c0 + width], preferred_element_type=F32)

    v = _gelu(project(dg, dg))
    u = project(0, dg)
    mu = jnp.mean(v, axis=-1, keepdims=True)
    vc = v - mu
    var = jnp.mean(vc * vc, axis=-1, keepdims=True)
    vn = (vc * lax.rsqrt(var + EPS) * lng_ref[...] + lnb_ref[...]).astype(BF16)
    n_rest = z_ref.shape[1] // dg
    for n in range(n_rest):
        z_ref[:, n * dg:(n + 1) * dg] = project((2 + n) * dg, dg).astype(z_ref.dtype)
    for c in range(tm // chunk):
        rows = slice(c * chunk, (c + 1) * chunk)
        for h in range(heads):
            cols = slice(h * hd, (h + 1) * hd)
            mixed = jnp.dot(ws_ref[h], vn[rows, cols], preferred_element_type=F32) + bs_ref[h]
            mix_ref[rows, cols] = mixed
    out = _gelu(u) * mix_ref[...]
    oa_ref[...] = _rms(out, ga_ref[...]).astype(oa_ref.dtype)


def _inproj(xa, xb, g, w, ln_g, ln_b, ws, bs_b, ga, tm):
    d = xa.shape[1]
    t = xa.shape[0] + xb.shape[0]
    d_in = w.shape[1]
    heads, chunk, _ = ws.shape
    dg = ln_g.shape[1]
    assert (d_in - 2 * dg) % dg == 0
    n_first = xa.shape[0] // tm
    vec = pl.BlockSpec((1, dg), lambda i: (0, 0))
    return pl.pallas_call(
        functools.partial(_inproj_kernel, n_first=n_first, heads=heads, chunk=chunk),
        grid=(t // tm,),
        in_specs=_pair_specs(tm, d, n_first) + [
            pl.BlockSpec((1, d), lambda i: (0, 0)),
            pl.BlockSpec((d, d_in), lambda i: (0, 0), pipeline_mode=pl.Buffered(1)),
            vec, vec,
            pl.BlockSpec(ws.shape, lambda i: (0, 0, 0)),
            pl.BlockSpec(bs_b.shape, lambda i: (0, 0, 0)),
            vec,
        ],
        out_specs=[pl.BlockSpec((tm, d_in - 2 * dg), lambda i: (i, 0)), pl.BlockSpec((tm, dg), lambda i: (i, 0))],
        out_shape=[jax.ShapeDtypeStruct((t, d_in - 2 * dg), BF16), jax.ShapeDtypeStruct((t, dg), BF16)],
        scratch_shapes=[pltpu.VMEM((tm, dg), F32)],
        compiler_params=_params(("arbitrary",)),
        name="inproj_gmlp",
    )(xa, xb, g, w, ln_g, ln_b, ws, bs_b, ga)


def _rot8(x, s):
    n, w = x.shape
    return pltpu.roll(x.reshape(n // 8, 8, w), s, axis=1).reshape(n, w)


def _row_in_group(shape):
    return lax.broadcasted_iota(jnp.int32, shape, 0) & 7


def _lru_scan_chunk(a, b, h, reverse):
    row = _row_in_group(a.shape)
    for s in (1, 2, 4):
        outside = (row >= 8 - s) if reverse else (row < s)
        rot = 8 - s if reverse else s
        b = jnp.where(outside, 0.0, a) * _rot8(b, rot) + b
        a = jnp.where(outside, a, a * _rot8(a, rot))
    n_grp = a.shape[0] // 8
    out = [None] * n_grp
    order = range(n_grp - 1, -1, -1) if reverse else range(n_grp)
    edge = 0 if reverse else 7
    for gi in order:
        rows = slice(gi * 8, gi * 8 + 8)
        hg = b[rows] + a[rows] * h
        out[gi] = hg
        h = hg[edge:edge + 1]
    return jnp.concatenate(out, axis=0), h


def _rglru_kernel(seqlen_ref, xr_ref, gr_ref, cw_ref, cb_ref, wri_ref, bri_ref, lam_ref, o_ref, hf_ref, xc_ref,
                  *, tc):
    rows, hd = xr_ref.shape
    n_ck = rows // tc
    halo = 16
    cw = cw_ref[...]
    cb = cb_ref[...]
    seq = seqlen_ref[pl.program_id(0)]

    def starts_seq(c):
        return lax.rem(c * tc, seq) == 0

    def ends_seq(c):
        return lax.rem((c + 1) * tc, seq) == 0

    def conv_chunk(c):
        t0 = pl.multiple_of(c * tc, tc)
        main = xr_ref[pl.ds(t0, tc), :].astype(F32)
        p0 = pl.multiple_of(jnp.maximum(t0 - halo, 0), halo)
        n0 = pl.multiple_of(jnp.minimum(t0 + tc, rows - halo), halo)
        prev = jnp.where(starts_seq(c), 0.0, xr_ref[pl.ds(p0, halo), :].astype(F32))
        nxt = jnp.where(ends_seq(c), 0.0, xr_ref[pl.ds(n0, halo), :].astype(F32))
        ext = jnp.concatenate([prev, main, nxt], axis=0)
        row = _row_in_group(main.shape)

        def earlier(s):
            r = _rot8(ext, s)
            return jnp.where(row < s, r[halo - 8:halo - 8 + tc], r[halo:halo + tc])

        r7 = _rot8(ext, 7)
        later = jnp.where(row >= 7, r7[halo + 8:halo + 8 + tc], r7[halo:halo + tc])
        acc = cb + cw[2:3] * main
        acc = acc + cw[0:1] * earlier(2)
        acc = acc + cw[1:2] * earlier(1)
        acc = acc + cw[3:4] * later
        return acc

    def direction(d, xc, h):
        g = jnp.tanh(jnp.dot(xc.astype(BF16), wri_ref[d], preferred_element_type=F32) + bri_ref[d])
        i = 0.5 * g[:, hd:] + 0.5
        nl = -lam_ref[d]
        softplus = jnp.maximum(nl, 0.0) + jnp.log1p(jnp.exp(-jnp.abs(nl)))
        half_c = (-0.5 * LRU_C) * softplus
        log_a = half_c * g[:, :hd] + half_c
        a = jnp.exp(log_a)
        y = -jnp.tanh(log_a) * (a * a + 1.0)
        b = jnp.where(y > 0.0, y * lax.rsqrt(y), 0.0) * i * xc
        return _lru_scan_chunk(a, b, h, reverse=(d == 1))

    def fwd(c, h):
        t0 = pl.multiple_of(c * tc, tc)
        xc = conv_chunk(c)
        xc_ref[pl.ds(t0, tc), :] = xc
        hs, h = direction(0, xc, jnp.where(starts_seq(c), 0.0, h))
        hf_ref[pl.ds(t0, tc), :] = hs
        return h

    def bwd(k, h):
        c = n_ck - 1 - k
        t0 = pl.multiple_of(c * tc, tc)
        hs, h = direction(1, xc_ref[pl.ds(t0, tc), :], jnp.where(ends_seq(c), 0.0, h))
        tot = hf_ref[pl.ds(t0, tc), :] + hs
        o_ref[pl.ds(t0, tc), :] = (_gelu(gr_ref[pl.ds(t0, tc), :].astype(F32)) * tot).astype(o_ref.dtype)
        return h

    h0 = jnp.zeros((1, hd), F32)
    lax.fori_loop(0, n_ck, fwd, h0)
    lax.fori_loop(0, n_ck, bwd, h0)


def _rglru(z, unit_seq, unit, col_x, col_g, cw, cb, wri, bri, lam, tc):
    t = z.shape[0]
    _, heads, hd, _ = wri.shape
    grid_spec = pltpu.PrefetchScalarGridSpec(
        num_scalar_prefetch=1,
        grid=(t // unit, heads),
        in_specs=[
            pl.BlockSpec((unit, hd), lambda u, h, sl: (u, col_x + h)),
            pl.BlockSpec((unit, hd), lambda u, h, sl: (u, col_g + h)),
            pl.BlockSpec((cw.shape[0], hd), lambda u, h, sl: (0, h)),
            pl.BlockSpec((1, hd), lambda u, h, sl: (0, h)),
            pl.BlockSpec((2, None, hd, 2 * hd), lambda u, h, sl: (0, h, 0, 0)),
            pl.BlockSpec((2, None, 1, 2 * hd), lambda u, h, sl: (0, h, 0, 0)),
            pl.BlockSpec((2, None, 1, hd), lambda u, h, sl: (0, h, 0, 0)),
        ],
        out_specs=pl.BlockSpec((unit, hd), lambda u, h, sl: (u, h)),
        scratch_shapes=[pltpu.VMEM((unit, hd), F32), pltpu.VMEM((unit, hd), F32)],
    )
    return pl.pallas_call(
        functools.partial(_rglru_kernel, tc=tc),
        grid_spec=grid_spec,
        out_shape=jax.ShapeDtypeStruct((t, heads * hd), BF16),
        compiler_params=_params(("arbitrary", "arbitrary")),
        name="rglru",
    )(unit_seq, z, z, cw, cb, wri, bri, lam)


def _outproj_kernel(xa_ref, xb_ref, a_ref, b_ref, gb_ref, wo_ref, gf_ref, rw_ref, rb_ref,
                    x1_ref, hp_ref, rt_ref, rg_ref, cnt_ref, carry_ref, *, n_first):
    step = pl.program_id(0)

    @pl.when(step == 0)
    def _():
        carry_ref[...] = jnp.zeros_like(carry_ref)

    da = a_ref.shape[1]
    tm = a_ref.shape[0]
    x_tile = _pair_tile(xa_ref, xb_ref, n_first)

    def project(rows):
        obn = _rms(b_ref[rows, :].astype(F32), gb_ref[...]).astype(BF16)
        y = jnp.dot(a_ref[rows, :], wo_ref[:da], preferred_element_type=F32)
        return y + jnp.dot(obn, wo_ref[da:], preferred_element_type=F32)

    def route(rows, y, carry):
        n = rows.stop - rows.start
        x1 = x_tile[rows, :] + y
        x1_ref[rows, :] = x1
        hf = _rms(x1, gf_ref[...])
        half = hf.shape[1] // 2
        hp_ref[rows, :] = _pack_bf16_pair(hf[:, :half], hf[:, half:])

        logits = jnp.dot(hf.astype(BF16), rw_ref[...], preferred_element_type=F32) + rb_ref[...]
        lane = lax.broadcasted_iota(jnp.int32, logits.shape, 1)
        lane_f = lane.astype(F32)
        work = logits
        hits, ids, vals = [], [], []
        for _ in range(TOP_K):
            m = jnp.max(work, axis=-1, keepdims=True)
            sel = jnp.min(jnp.where(work == m, lane_f, float(LANES)), axis=-1, keepdims=True)
            hit = lane_f == sel
            hits.append(hit)
            ids.append(sel)
            vals.append(m)
            work = jnp.where(hit, -jnp.inf, work)
        exps = [jnp.exp(v - vals[0]) for v in vals]
        denom = exps[0]
        for e in exps[1:]:
            denom = denom + e

        onehot = hits[0]
        for hit in hits[1:]:
            onehot = onehot | hit
        cnt = onehot.astype(F32)
        rr = lax.broadcasted_iota(jnp.int32, (n, n), 0)
        cc = lax.broadcasted_iota(jnp.int32, (n, n), 1)
        tril = (rr > cc).astype(BF16)
        before = jnp.dot(tril, cnt.astype(BF16), preferred_element_type=F32) + carry

        ri = jnp.zeros(logits.shape, F32)
        rg = jnp.zeros(logits.shape, F32)
        for k in range(TOP_K):
            rank_k = jnp.sum(jnp.where(hits[k], before, 0.0), axis=-1, keepdims=True)
            ri = jnp.where(lane == k, ids[k], ri)
            ri = jnp.where(lane == TOP_K + k, rank_k, ri)
            rg = jnp.where(lane == k, exps[k] / denom, rg)
        rt_ref[:, rows] = ri.T[:2 * TOP_K, :].astype(jnp.int32)
        rg_ref[rows, :] = rg
        return carry + jnp.sum(cnt, axis=0, keepdims=True)

    n_piece = OUTPROJ_PIECES if tm % (OUTPROJ_PIECES * LANES) == 0 else 1
    pieces = [slice(i * tm // n_piece, (i + 1) * tm // n_piece) for i in range(n_piece)]
    carry = carry_ref[...]
    ys = [project(pieces[0])]
    for i in range(n_piece):
        if i + 1 < n_piece:
            ys.append(project(pieces[i + 1]))
        carry = route(pieces[i], ys[i], carry)
    carry_ref[...] = carry
    cnt_ref[...] = carry.astype(jnp.int32)


def _outproj(xa, xb, out_a, out_b, gb, wo, gf, rw, rb, tm):
    d = xa.shape[1]
    t = xa.shape[0] + xb.shape[0]
    n_first = xa.shape[0] // tm
    da = out_a.shape[1]
    db = out_b.shape[1]
    row = lambda i: (i, 0)
    fixed = lambda i: (0, 0)
    return pl.pallas_call(
        functools.partial(_outproj_kernel, n_first=n_first),
        grid=(t // tm,),
        in_specs=_pair_specs(tm, d, n_first) + [
            pl.BlockSpec((tm, da), row),
            pl.BlockSpec((tm, db), row),
            pl.BlockSpec((1, db), fixed),
            pl.BlockSpec(wo.shape, fixed, pipeline_mode=pl.Buffered(1)),
            pl.BlockSpec((1, d), fixed),
            pl.BlockSpec(rw.shape, fixed),
            pl.BlockSpec((1, LANES), fixed),
        ],
        out_specs=[
            pl.BlockSpec((tm, d), row),
            pl.BlockSpec((tm, d // 2), row),
            pl.BlockSpec((2 * TOP_K, tm), lambda i: (0, i)),
            pl.BlockSpec((tm, LANES), row),
            pl.BlockSpec((1, LANES), fixed),
        ],
        out_shape=[
            jax.ShapeDtypeStruct((t, d), F32),
            jax.ShapeDtypeStruct((t, d // 2), U32),
            jax.ShapeDtypeStruct((2 * TOP_K, t), jnp.int32),
            jax.ShapeDtypeStruct((t, LANES), F32),
            jax.ShapeDtypeStruct((1, LANES), jnp.int32),
        ],
        scratch_shapes=[pltpu.VMEM((1, LANES), F32)],
        compiler_params=_params(("arbitrary",)),
        name="outproj_router",
    )(xa, xb, out_a, out_b, gb, wo, gf, rw, rb)


def _tile_slots(dest_kt, tm):
    k, t = dest_kt.shape
    return dest_kt.reshape(k, t // tm, tm).transpose(1, 0, 2).reshape(t // tm, 1, k * tm)


def _dispatch_kernel(zneed_ref, dest_ref, hp_ref, xs_ref, zero_ref, sem, zsem, *, sub):
    step = pl.program_id(0)
    n_grp, group, _ = hp_ref.shape
    tm = n_grp * group
    n_zc = zneed_ref.shape[0]

    def zero_copy(j):
        return pltpu.make_async_copy(zero_ref, xs_ref.at[pl.ds(pl.multiple_of(j * sub, sub), sub), :], zsem)

    @pl.when(step == 0)
    def _():
        zero_ref[...] = jnp.zeros_like(zero_ref)

        def start(j, c):
            @pl.when(zneed_ref[j] != 0)
            def _():
                zero_copy(j).start()
            return c

        def wait(j, c):
            @pl.when(zneed_ref[j] != 0)
            def _():
                zero_copy(j).wait()
            return c

        lax.fori_loop(0, n_zc, start, 0)
        lax.fori_loop(0, n_zc, wait, 0)

    def issue(g, c):
        base = g * group
        for r in range(group):
            for k in range(TOP_K):
                d = dest_ref[0, 0, k * tm + base + r]
                pltpu.make_async_copy(hp_ref.at[g, pl.ds(r, 1), :], xs_ref.at[pl.ds(d, 1), :],
                                      sem).start(priority=k % 2)
        return c

    lax.fori_loop(0, n_grp, issue, 0)
    for _ in range(TOP_K):
        pltpu.make_async_copy(xs_ref.at[pl.ds(0, tm), :], xs_ref.at[pl.ds(0, tm), :], sem).wait()


def _dispatch(zneed, dest_kt, hp, n_slots, sub, tm):
    t, half = hp.shape
    grid_spec = pltpu.PrefetchScalarGridSpec(
        num_scalar_prefetch=1,
        grid=(t // tm,),
        in_specs=[
            pl.BlockSpec((1, 1, tm * TOP_K), lambda i, *_: (i, 0, 0), memory_space=pltpu.SMEM),
            pl.BlockSpec((tm // SUBLANES, SUBLANES, half), lambda i, *_: (i, 0, 0)),
        ],
        out_specs=pl.BlockSpec(memory_space=pl.ANY),
        scratch_shapes=[pltpu.VMEM((sub, half), U32), pltpu.SemaphoreType.DMA, pltpu.SemaphoreType.DMA],
    )
    return pl.pallas_call(
        functools.partial(_dispatch_kernel, sub=sub),
        grid_spec=grid_spec,
        out_shape=jax.ShapeDtypeStruct((n_slots, half), U32),
        compiler_params=_params(("arbitrary",)),
        name="dispatch",
    )(zneed, _tile_slots(dest_kt, tm), hp.reshape(t // SUBLANES, SUBLANES, half))


def _gmm_kernel(plan_ref, x_ref, bias_ref, w_hbm, o_ref, wbuf, stage, sem, *rest, mode, kc, sub, nw):
    blk = pl.program_id(0)
    expert = plan_ref[0, blk]
    half_idx = plan_ref[1, blk]
    nxt = plan_ref[3, blk]
    valid = blk < plan_ref[7, 0]
    n_sub = jnp.where(valid, plan_ref[6, blk], 0)
    n_ch = w_hbm.shape[1] // kc

    def chunk_rows(k):
        return pl.ds(pl.multiple_of(k * kc, kc), kc)

    def chunk_copy(e, k, st):
        return pltpu.make_async_copy(w_hbm.at[e, chunk_rows(k), :], stage.at[st], sem.at[st])

    def land(e, k, dst):
        st = k % 2
        chunk_copy(e, k, st).wait()
        wbuf[dst, chunk_rows(k), :] = stage[st].astype(BF16)

        @pl.when(k + 2 < n_ch)
        def _():
            chunk_copy(e, k + 2, st).start()

    @pl.when(blk == 0)
    def _():
        chunk_copy(expert, 0, 0).start()
        chunk_copy(expert, 1, 1).start()
        lax.fori_loop(0, n_ch, lambda k, c: (land(expert, k, half_idx), c)[1], 0)

    @pl.when(valid & (plan_ref[2, blk] == 1) & (nxt >= 0))
    def _():
        chunk_copy(nxt, 0, 0).start()
        chunk_copy(nxt, 1, 1).start()

    if mode == "gate_up":
        xb_ref, = rest
        half = x_ref.shape[1]
        de = o_ref.shape[1]

        def run(rows):
            v = x_ref[rows, :]
            xb_ref[rows, :half] = _unpack_lo(v).astype(BF16)
            xb_ref[rows, half:] = _unpack_hi(v).astype(BF16)
            x = xb_ref[rows, :]
            for n in range(de // nw):
                cg = slice(n * nw, (n + 1) * nw)
                cu = slice(de + n * nw, de + (n + 1) * nw)
                gate = jnp.dot(x, wbuf[half_idx, :, cg], preferred_element_type=F32) + bias_ref[:, cg]
                up = jnp.dot(x, wbuf[half_idx, :, cu], preferred_element_type=F32) + bias_ref[:, cu]
                gate = jnp.minimum(gate, SWIGLU_LIMIT)
                up = jnp.clip(up, -SWIGLU_LIMIT, SWIGLU_LIMIT)
                act = (up + 1.0) * (gate * jax.nn.sigmoid(SWIGLU_ALPHA * gate))
                o_ref[rows, cg] = act.astype(o_ref.dtype)
    else:
        half = o_ref.shape[1]

        def run(rows):
            x = x_ref[rows, :]
            for n in range(half // nw):
                cl = slice(n * nw, (n + 1) * nw)
                ch = slice(half + n * nw, half + (n + 1) * nw)
                lo = jnp.dot(x, wbuf[half_idx, :, cl], preferred_element_type=F32) + bias_ref[:, cl]
                hi = jnp.dot(x, wbuf[half_idx, :, ch], preferred_element_type=F32) + bias_ref[:, ch]
                o_ref[rows, cl] = _pack_bf16_pair(lo, hi)

    full = x_ref.shape[0] // sub

    @pl.when(n_sub < full)
    def _():
        o_ref[...] = jnp.zeros_like(o_ref)

    @pl.when(n_sub == full)
    def _():
        run(slice(None))

    @pl.when((n_sub > 0) & (n_sub < full))
    def _():
        def body(i, c):
            run(pl.ds(pl.multiple_of(i * sub, sub), sub))
            return c
        lax.fori_loop(0, n_sub, body, 0)

    lax.fori_loop(plan_ref[4, blk], plan_ref[5, blk], lambda k, c: (land(nxt, k, 1 - half_idx), c)[1], 0)


def _gmm(plan, x, bias, w, mode, bm, sub, kc):
    n_slots = x.shape[0]
    n_exp, kdim, ndim = w.shape
    n_blk = n_slots // bm
    if mode == "gate_up":
        out_cols, out_dtype = ndim // 2, BF16
        extra = [pltpu.VMEM((bm, kdim), BF16)]
    else:
        out_cols, out_dtype = ndim // 2, U32
        extra = []

    def live(b, plan):
        return (jnp.minimum(b, plan[7, 0] - 1), 0)

    grid_spec = pltpu.PrefetchScalarGridSpec(
        num_scalar_prefetch=1,
        grid=(n_blk,),
        in_specs=[
            pl.BlockSpec((bm, x.shape[1]), live),
            pl.BlockSpec((None, 1, ndim), lambda b, plan: (plan[0, b], 0, 0)),
            pl.BlockSpec(memory_space=pl.ANY),
        ],
        out_specs=pl.BlockSpec((bm, out_cols), lambda b, plan: (b, 0)),
        scratch_shapes=[
            pltpu.VMEM((2, kdim, ndim), BF16),
            pltpu.VMEM((2, kc, ndim), F32),
            pltpu.SemaphoreType.DMA((2,)),
        ] + extra,
    )
    return pl.pallas_call(
        functools.partial(_gmm_kernel, mode=mode, kc=kc, sub=sub, nw=min(EXPERT_DOT_COLUMNS, out_cols)),
        grid_spec=grid_spec,
        out_shape=jax.ShapeDtypeStruct((n_slots, out_cols), out_dtype),
        compiler_params=_params(("arbitrary",)),
        name="experts_" + mode,
    )(plan, x, bias, w)


def _expert_plan(counts, bm, sub, n_blk, n_ch):
    n_exp = counts.shape[0]
    ar = jnp.arange(n_exp, dtype=jnp.int32)
    nb_e = (counts + bm - 1) // bm
    blk_end = jnp.cumsum(nb_e)
    blk_start = blk_end - nb_e
    n_valid = blk_end[-1]
    blk = jnp.arange(n_blk, dtype=jnp.int32)
    last = jnp.minimum(blk, n_valid - 1)
    be = jnp.minimum(jnp.sum(blk_end[None, :] <= last[:, None], axis=1), n_exp - 1).astype(jnp.int32)
    mine = be[:, None] == ar[None, :]

    def of_block(per_expert):
        return jnp.sum(jnp.where(mine, per_expert[None, :], 0), axis=1)

    valid = blk < n_valid
    i_in = last - of_block(blk_start)
    nb_b = of_block(nb_e)
    nonempty = nb_e > 0
    ordinal = jnp.cumsum(nonempty.astype(jnp.int32)) - 1
    later = (ar[None, :] > ar[:, None]) & nonempty[None, :]
    nxt_e = jnp.min(jnp.where(later, ar[None, :], n_exp), axis=1)
    nxt_e = jnp.where(nxt_e >= n_exp, -1, nxt_e)
    nxt_b = of_block(nxt_e)
    has_next = valid & (nxt_b >= 0)
    lo = jnp.where(has_next, i_in * n_ch // nb_b, 0)
    hi = jnp.where(has_next, (i_in + 1) * n_ch // nb_b, 0)
    rows = jnp.where(valid, jnp.clip(of_block(counts) - i_in * bm, 0, bm), 0)
    nsub = (rows + sub - 1) // sub
    plan = jnp.stack([
        be, of_block(ordinal) % 2, (valid & (i_in == 0)).astype(jnp.int32), nxt_b, lo, hi, nsub,
        jnp.full((n_blk,), n_valid, jnp.int32),
    ]).astype(jnp.int32)
    per = bm // sub
    sub_rows = rows[:, None] - jnp.arange(per, dtype=jnp.int32)[None, :] * sub
    zneed = (sub_rows < sub).astype(jnp.int32).reshape(n_blk * per)
    return blk_start * bm, plan, zneed


def _combine_kernel(dcur_ref, dnxt_ref, x1_ref, rg_ref, gfin_ref, os_ref, ya_ref, yb_ref, gbuf_ref, sem,
                    *, final_norm, n_first, cw):
    step = pl.program_id(0)
    last = pl.num_programs(0) - 1
    tm = x1_ref.shape[0]
    half = gbuf_ref.shape[4]
    cur = step % 2

    def token_copies(dref, grp, row, buf):
        for k in range(TOP_K):
            d = dref[0, 0, k * tm + grp * SUBLANES + row]
            pltpu.make_async_copy(os_ref.at[pl.ds(d, 1), :], gbuf_ref.at[buf, k, grp, pl.ds(row, 1), :],
                                  sem.at[buf]).start(priority=k % 2)

    def wait_tile(buf):
        for k in range(TOP_K):
            pltpu.make_async_copy(os_ref.at[pl.ds(0, tm), :], os_ref.at[pl.ds(0, tm), :], sem.at[buf]).wait()

    @pl.when(step == 0)
    def _():
        def body(g, c):
            for r in range(SUBLANES):
                token_copies(dcur_ref, g, r, 0)
            return c
        lax.fori_loop(0, tm // SUBLANES, body, 0)

    def finish(y_ref, cur):
        wait_tile(cur)
        rg = rg_ref[...]
        ss = jnp.zeros((tm, 1), F32)
        n_cc = half // cw
        share = tm // n_cc
        for c in range(n_cc):
            cl = slice(c * cw, (c + 1) * cw)
            ch = slice(half + c * cw, half + (c + 1) * cw)
            lo = x1_ref[:, cl]
            hi = x1_ref[:, ch]
            for k in range(TOP_K):
                g = rg[:, k:k + 1]
                v = gbuf_ref[cur, k, :, :, cl].reshape(tm, cw)
                lo = lo + g * _unpack_lo(v)
                hi = hi + g * _unpack_hi(v)
            if final_norm:
                ss = ss + jnp.sum(lo * lo, axis=-1, keepdims=True) + jnp.sum(hi * hi, axis=-1, keepdims=True)
            y_ref[:, cl] = lo
            y_ref[:, ch] = hi
            for r in range(c * share, (c + 1) * share):
                token_copies(dnxt_ref, r // SUBLANES, r % SUBLANES, 1 - cur)
        if final_norm:
            y_ref[...] = y_ref[...] * lax.rsqrt(ss / (2 * half) + EPS) * gfin_ref[...]

    for parity in range(2):
        @pl.when((step < n_first) & (cur == parity))
        def _():
            finish(ya_ref, parity)

        @pl.when((step >= n_first) & (cur == parity))
        def _():
            finish(yb_ref, parity)

    @pl.when(step == last)
    def _():
        wait_tile(1 - cur)


def _combine(dest_kt, x1, rg, gfin, os_, t_first, tm, final_norm):
    t, d = x1.shape
    half = d // 2
    n_tiles = t // tm
    n_first = t_first // tm
    dest3 = _tile_slots(dest_kt, tm)
    return pl.pallas_call(
        functools.partial(_combine_kernel, final_norm=final_norm, n_first=n_first,
                          cw=min(COMBINE_COLUMNS, half)),
        grid=(n_tiles,),
        in_specs=[
            pl.BlockSpec((1, 1, tm * TOP_K), lambda i: (i, 0, 0), memory_space=pltpu.SMEM),
            pl.BlockSpec((1, 1, tm * TOP_K), lambda i: (jnp.minimum(i + 1, n_tiles - 1), 0, 0),
                         memory_space=pltpu.SMEM),
            pl.BlockSpec((tm, d), lambda i: (i, 0)),
            pl.BlockSpec((tm, LANES), lambda i: (i, 0)),
            pl.BlockSpec((1, d), lambda i: (0, 0)),
            pl.BlockSpec(memory_space=pl.ANY),
        ],
        out_specs=_pair_specs(tm, d, n_first),
        out_shape=[jax.ShapeDtypeStruct((t_first, d), F32), jax.ShapeDtypeStruct((t - t_first, d), F32)],
        scratch_shapes=[pltpu.VMEM((2, TOP_K, tm // SUBLANES, SUBLANES, half), U32),
                        pltpu.SemaphoreType.DMA((2,))],
        compiler_params=_params(("arbitrary",)),
        name="combine",
    )(dest3, dest3, x1, rg, gfin, os_)


def _tile(n, pref):
    while n % pref:
        pref //= 2
    return pref


def _layer(xa, xb, seq_a, seq_b, p, final_g):
    ta, d = xa.shape
    t = ta + xb.shape[0]
    dg = p["gmlp_ln_g"].shape[-1]
    chunk = p["gmlp_ws"].shape[-1]
    d_lru = p["conv_w"].shape[-1]
    heads_l, hd_l = p["lru_wr"].shape[1], p["lru_wr"].shape[2]
    n_exp = p["router_w"].shape[-1]
    assert p["w_down"].shape[1] == d, "expert width must equal the model width (shared weight-chunk plan)"

    unit = max(seq_a, seq_b)
    assert unit % seq_a == 0 and unit % seq_b == 0 and ta % unit == 0 and (t - ta) % unit == 0
    tm = _tile(math.gcd(ta, t - ta), 512)
    assert tm % chunk == 0 and seq_a % chunk == 0 and seq_b % chunk == 0
    bs_b = jnp.broadcast_to(p["gmlp_bs"][:, :, None], p["gmlp_ws"].shape).astype(F32)
    z, out_a = _inproj(xa, xb, p["mix_norm_g"][None], p["w_in"].astype(BF16), p["gmlp_ln_g"][None],
                       p["gmlp_ln_b"][None], p["gmlp_ws"].astype(BF16), bs_b, p["out_norm_a"][None], tm)

    wri = (0.5 * jnp.concatenate([p["lru_wr"], p["lru_wi"]], axis=-1)).astype(BF16)
    bri = 0.5 * jnp.concatenate([p["lru_br"].reshape(2, heads_l, 1, hd_l),
                                 p["lru_bi"].reshape(2, heads_l, 1, hd_l)], axis=-1)
    lam = p["lru_lam"].reshape(2, heads_l, 1, hd_l)
    unit_seq = jnp.asarray([seq_a] * (ta // unit) + [seq_b] * ((t - ta) // unit), jnp.int32)
    out_b = _rglru(z, unit_seq, unit, 0, d_lru // hd_l,
                   p["conv_w"], p["conv_b"][None], wri, bri, lam, _tile(math.gcd(seq_a, seq_b), 512))

    rw = jnp.zeros((d, LANES), BF16).at[:, :n_exp].set(p["router_w"].astype(BF16))
    rb = jnp.full((1, LANES), NEG_BIG, F32).at[0, :n_exp].set(p["router_b"])
    x1, hp, rt, rg, cnt = _outproj(xa, xb, out_a, out_b, p["out_norm_b"][None], p["w_out"].astype(BF16),
                                   p["ffn_norm_g"][None], rw, rb, tm)

    m = t * TOP_K
    bm = _tile(m, ROW_BLOCK)
    sub = min(SUB_BLOCK, bm)
    kc = _tile(d, WEIGHT_CHUNK_ROWS)
    counts = cnt[0, :n_exp]
    n_blk = m // bm + n_exp
    n_slots = n_blk * bm
    pad_start, plan, zneed = _expert_plan(counts, bm, sub, n_blk, d // kc)
    top_e = rt[:TOP_K]
    base = jnp.sum(jnp.where(top_e[..., None] == jnp.arange(n_exp, dtype=jnp.int32), pad_start, 0), axis=-1)
    dest_kt = base + rt[TOP_K:]

    xs = _dispatch(zneed, dest_kt, hp, n_slots, sub, _tile(t, 1024))
    act = _gmm(plan, xs, p["b_gate_up"][:, None, :], p["w_gate_up"], "gate_up", bm, sub, kc)
    os_ = _gmm(plan, act, p["b_down"][:, None, :], p["w_down"], "down", bm, sub, kc)
    gfin = jnp.ones((1, d), F32) if final_g is None else final_g[None]
    return _combine(dest_kt, x1, rg, gfin, os_, ta, _tile(math.gcd(ta, t - ta), 256), final_g is not None)


_LAYER_KEYS = ("mix_norm_g", "w_in", "gmlp_ln_g", "gmlp_ln_b", "gmlp_ws", "gmlp_bs", "conv_w", "conv_b",
               "lru_wr", "lru_br", "lru_wi", "lru_bi", "lru_lam", "out_norm_a", "out_norm_b", "w_out",
               "ffn_norm_g", "router_w", "router_b", "w_gate_up", "b_gate_up", "w_down", "b_down")


def kernel(x_prompt, x_sample, mix_norm_g, w_in, gmlp_ln_g, gmlp_ln_b, gmlp_ws, gmlp_bs, conv_w, conv_b, lru_wr, lru_br, lru_wi, lru_bi, lru_lam, out_norm_a, out_norm_b, w_out, ffn_norm_g, router_w, router_b, w_gate_up, b_gate_up, w_down, b_down, final_norm_g):
    stacked = dict(zip(_LAYER_KEYS, (mix_norm_g, w_in, gmlp_ln_g, gmlp_ln_b, gmlp_ws, gmlp_bs, conv_w, conv_b,
                                     lru_wr, lru_br, lru_wi, lru_bi, lru_lam, out_norm_a, out_norm_b, w_out,
                                     ffn_norm_g, router_w, router_b, w_gate_up, b_gate_up, w_down, b_down)))
    depth = w_in.shape[0]
    bp, sp, d = x_prompt.shape
    bs, ss, _ = x_sample.shape
    xa = x_prompt.reshape(bp * sp, d)
    xb = x_sample.reshape(bs * ss, d)
    for layer in range(depth):
        p = {k: v[layer] for k, v in stacked.items()}
        xa, xb = _layer(xa, xb, sp, ss, p, final_norm_g if layer == depth - 1 else None)
    return xa.reshape(bp, sp, d), xb.reshape(bs, ss, d)
```
